```python
import jax, jax.numpy as jnp
from jax import lax
import numpy as np

D_MODEL = 1024
BATCH = 2
SEQ = 8192
DEPTH = 1

NSA_HEADS = 8
NSA_KV_GROUPS = 2
NSA_HEAD_DIM = 64
NSA_HEADS_PER_GROUP = NSA_HEADS // NSA_KV_GROUPS
NSA_WIDTH = NSA_HEADS * NSA_HEAD_DIM
NSA_KV_WIDTH = NSA_KV_GROUPS * NSA_HEAD_DIM
CMP_LEN = 32
CMP_STRIDE = 16
SLC_LEN = 64
SLC_TOPN = 16
WINDOW = 512
Q_BLOCK = 128
ROPE_THETA = 500000.0
ROPE_DIM = NSA_HEAD_DIM // 4
HG_HEADS = 4
HG_KEY_DIM = 128
HG_VAL_DIM = 128
HG_WIDTH = HG_HEADS * HG_VAL_DIM
HG_CHUNK = 64
N_GROUPS = 4
EXPERTS_PER_GROUP = 8
N_EXPERTS = N_GROUPS * EXPERTS_PER_GROUP
TOP_K_IN_GROUP = 2
D_FF_EXPERT = 512
RMS_EPS = 1e-6
NEG_INF = -1e30
FORCE_SCORE = 1e4

IN_SPLITS = [NSA_WIDTH, 6 * NSA_KV_WIDTH, 3 * NSA_HEADS,
             HG_HEADS * HG_KEY_DIM, HG_HEADS * HG_KEY_DIM, HG_WIDTH, HG_WIDTH, 2 * D_MODEL]
IN_COLS = sum(IN_SPLITS)

kernel_name = 'hybrid_nsa_hgrn2_hmoe'


def rmsnorm(x, g):
    x32 = x.astype(jnp.float32)
    y = x32 * lax.rsqrt(jnp.mean(x32 * x32, axis=-1, keepdims=True) + RMS_EPS)
    return (y * g.astype(jnp.float32)).astype(x.dtype)


def partial_rope(t, pos):
    half = ROPE_DIM // 2
    inv_freq = ROPE_THETA ** (-jnp.arange(half, dtype=jnp.float32) / half)
    ang = pos.astype(jnp.float32)[:, None] * inv_freq[None, :]
    cos = jnp.cos(ang)[:, None, :]
    sin = jnp.sin(ang)[:, None, :]
    t1 = t[..., :half]
    t2 = t[..., half:ROPE_DIM]
    return jnp.concatenate([t1 * cos - t2 * sin, t2 * cos + t1 * sin, t[..., ROPE_DIM:]], axis=-1)


def masked_softmax(s, mask):
    s = jnp.where(mask, s, NEG_INF)
    return jax.nn.softmax(s, axis=-1) * mask


def nsa_mixer(q, k_cmp_in, v_cmp_in, k_slc, v_slc, k_win, v_win, gate_logits, w_cmp_k, w_cmp_v, cmp_pos):
    f32 = jnp.float32
    b, s = q.shape[0], q.shape[1]
    G, R, dk = NSA_KV_GROUPS, NSA_HEADS_PER_GROUP, NSA_HEAD_DIM
    pos = jnp.arange(s)
    scale = dk ** -0.5
    q = partial_rope(q.astype(f32), pos).reshape(b, s, G, R, dk)
    k_slc = partial_rope(k_slc.astype(f32), pos)
    k_win = partial_rope(k_win.astype(f32), pos)
    v_slc = v_slc.astype(f32)
    v_win = v_win.astype(f32)
    gates = jax.nn.sigmoid(gate_logits.astype(f32)).reshape(b, s, G, R, 3)

    n_cmp = (s - CMP_LEN) // CMP_STRIDE + 1
    cmp_start = jnp.arange(n_cmp) * CMP_STRIDE
    cmp_end = cmp_start + CMP_LEN - 1
    cmp_idx = cmp_start[:, None] + jnp.arange(CMP_LEN)[None, :]

    def compress(t, w):
        blk = t.astype(f32)[:, cmp_idx] + cmp_pos.astype(f32)[None, None, :, None, :]
        blk = blk.transpose(0, 1, 3, 2, 4).reshape(b, n_cmp, G, CMP_LEN * dk)
        return jnp.einsum('bngc,cd->bngd', blk, w.astype(f32))

    k_cmp = partial_rope(compress(k_cmp_in, w_cmp_k), cmp_start)
    v_cmp = compress(v_cmp_in, w_cmp_v)

    n_sel = s // SLC_LEN
    top_n = min(SLC_TOPN, n_sel)
    sel_start = jnp.arange(n_sel) * SLC_LEN
    blk_ids = jnp.arange(n_sel)
    overlap = ((cmp_start[:, None] < sel_start[None, :] + SLC_LEN) &
               (cmp_end[:, None] >= sel_start[None, :])).astype(f32)
    k_sel_blocks = k_slc.reshape(b, n_sel, SLC_LEN, G, dk).transpose(0, 3, 1, 2, 4)
    v_sel_blocks = v_slc.reshape(b, n_sel, SLC_LEN, G, dk).transpose(0, 3, 1, 2, 4)
    gather_blocks = jax.vmap(jax.vmap(lambda blocks, ids: blocks[ids]))
    slc_offsets = jnp.arange(SLC_LEN)

    k_win_pad = jnp.pad(k_win, ((0, 0), (WINDOW, 0), (0, 0), (0, 0)))
    v_win_pad = jnp.pad(v_win, ((0, 0), (WINDOW, 0), (0, 0), (0, 0)))
    win_offsets = jnp.arange(WINDOW + Q_BLOCK) - WINDOW
    q_offsets = jnp.arange(Q_BLOCK)

    def query_block(c):
        t0 = c * Q_BLOCK
        tpos = t0 + q_offsets
        qb = lax.dynamic_slice_in_dim(q, t0, Q_BLOCK, axis=1)
        gb = lax.dynamic_slice_in_dim(gates, t0, Q_BLOCK, axis=1)
        cmask = cmp_end[None, :] <= tpos[:, None]
        p_cmp = masked_softmax(jnp.einsum('bqgrd,bngd->bgrqn', qb, k_cmp) * scale, cmask)
        o_cmp = jnp.einsum('bgrqn,bngd->bqgrd', p_cmp, v_cmp)
        imp = jnp.einsum('bgrqn,nj->bgqj', p_cmp, overlap)
        force = (blk_ids[None, :] == (tpos // SLC_LEN)[:, None]) | (blk_ids[None, :] == 0)
        valid = sel_start[None, :] <= tpos[:, None]
        score = jnp.where(force, FORCE_SCORE, jnp.where(valid, imp, -1.0))
        _, sel_idx = lax.top_k(score, top_n)
        k_sel = gather_blocks(k_sel_blocks, sel_idx).reshape(b, G, Q_BLOCK, top_n * SLC_LEN, dk)
        v_sel = gather_blocks(v_sel_blocks, sel_idx).reshape(b, G, Q_BLOCK, top_n * SLC_LEN, dk)
        sel_pos = (sel_idx[..., None] * SLC_LEN + slc_offsets).reshape(b, G, Q_BLOCK, top_n * SLC_LEN)
        smask = (sel_pos <= tpos[None, None, :, None])[:, :, None]
        p_slc = masked_softmax(jnp.einsum('bqgrd,bgqmd->bgrqm', qb, k_sel) * scale, smask)
        o_slc = jnp.einsum('bgrqm,bgqmd->bqgrd', p_slc, v_sel)
        kw = lax.dynamic_slice_in_dim(k_win_pad, t0, WINDOW + Q_BLOCK, axis=1)
        vw = lax.dynamic_slice_in_dim(v_win_pad, t0, WINDOW + Q_BLOCK, axis=1)
        kpos = t0 + win_offsets
        rel = tpos[:, None] - kpos[None, :]
        wmask = (rel >= 0) & (rel < WINDOW) & (kpos[None, :] >= 0)
        p_win = masked_softmax(jnp.einsum('bqgrd,bmgd->bgrqm', qb, kw) * scale, wmask)
        o_win = jnp.einsum('bgrqm,bmgd->bqgrd', p_win, vw)
        return gb[..., 0:1] * o_cmp + gb[..., 1:2] * o_slc + gb[..., 2:3] * o_win

    out = lax.map(query_block, jnp.arange(s // Q_BLOCK))
    return out.transpose(1, 0, 2, 3, 4, 5).reshape(b, s, NSA_WIDTH)


def hgrn2_mixer(q, f_logit, i_in, lb):
    f32 = jnp.float32
    b, s = q.shape[0], q.shape[1]
    q = q.astype(f32).reshape(b, s, HG_HEADS, HG_KEY_DIM)
    f = lb + (1.0 - lb) * jax.nn.sigmoid(f_logit.astype(f32).reshape(b, s, HG_HEADS, HG_KEY_DIM))
    log_f = jnp.log(f)
    k = 1.0 - f
    v = i_in.astype(f32).reshape(b, s, HG_HEADS, HG_VAL_DIM)
    n_ch = s // HG_CHUNK

    def to_chunks(t):
        return t.reshape(b, n_ch, HG_CHUNK, HG_HEADS, t.shape[-1]).transpose(1, 0, 3, 2, 4)

    causal = jnp.tril(jnp.ones((HG_CHUNK, HG_CHUNK), dtype=bool))

    def chunk_step(state, inp):
        qc, kc, vc, gc = inp
        bcum = jnp.cumsum(gc, axis=2)
        o_inter = jnp.einsum('bhtk,bhkv->bhtv', qc * jnp.exp(bcum), state)
        rel = jnp.where(causal[:, :, None], bcum[:, :, :, None, :] - bcum[:, :, None, :, :], -jnp.inf)
        attn = jnp.einsum('bhtk,bhsk,bhtsk->bhts', qc, kc, jnp.exp(rel))
        o_intra = jnp.einsum('bhts,bhsv->bhtv', attn, vc)
        b_last = bcum[:, :, -1:, :]
        new_state = (jnp.exp(b_last[:, :, 0, :])[..., None] * state +
                     jnp.einsum('bhsk,bhsv->bhkv', kc * jnp.exp(b_last - bcum), vc))
        return new_state, o_inter + o_intra

    state0 = jnp.zeros((b, HG_HEADS, HG_KEY_DIM, HG_VAL_DIM), f32)
    _, o = lax.scan(chunk_step, state0, (to_chunks(q), to_chunks(k), to_chunks(v), to_chunks(log_f)))
    return o.transpose(1, 0, 3, 2, 4).reshape(b, s, HG_HEADS, HG_VAL_DIM)


def hier_moe(h, w_grp, b_grp, w_rtr, b_rtr, w_gate, w_up, w_down):
    f32 = jnp.float32
    b, s, d = h.shape
    t = h.reshape(b * s, d)
    grp_logits = (t @ w_grp).astype(f32) + b_grp.astype(f32)
    grp_sel = jnp.argmax(grp_logits, axis=-1)
    grp_prob = jnp.take_along_axis(jax.nn.softmax(grp_logits, axis=-1), grp_sel[:, None], axis=1)
    exp_logits = ((t @ w_rtr).astype(f32) + b_rtr.astype(f32)).reshape(-1, N_GROUPS, EXPERTS_PER_GROUP)
    in_grp = jnp.take_along_axis(exp_logits, grp_sel[:, None, None], axis=1)[:, 0]
    top_logits, top_idx = lax.top_k(in_grp, TOP_K_IN_GROUP)
    top_w = jax.nn.softmax(top_logits, axis=-1) * grp_prob
    expert_ids = grp_sel[:, None] * EXPERTS_PER_GROUP + top_idx
    combine = jnp.sum(jax.nn.one_hot(expert_ids, N_EXPERTS, dtype=f32) * top_w[..., None], axis=1)
    out = jnp.zeros((b * s, d), f32)
    for e in range(N_EXPERTS):
        hid = jax.nn.silu(t @ w_gate[e]) * (t @ w_up[e])
        out = out + combine[:, e:e + 1] * (hid @ w_down[e]).astype(f32)
    return out.reshape(b, s, d).astype(h.dtype)


def setup_inputs(seed: int = 0) -> dict:
    key = jax.random.key(seed)
    ks = jax.random.split(key, 24)
    f32 = jnp.float32
    nrm = lambda k, shape, scale: jax.random.normal(k, shape, f32) * scale
    return {
        'x': nrm(ks[0], (BATCH, SEQ, D_MODEL), 1.0),
        'attn_norm': 1.0 + nrm(ks[1], (DEPTH, D_MODEL), 0.01),
        'w_in': nrm(ks[2], (DEPTH, D_MODEL, IN_COLS), D_MODEL ** -0.5),
        'w_cmp_k': nrm(ks[3], (DEPTH, CMP_LEN * NSA_HEAD_DIM, NSA_HEAD_DIM), (CMP_LEN * NSA_HEAD_DIM) ** -0.5),
        'w_cmp_v': nrm(ks[4], (DEPTH, CMP_LEN * NSA_HEAD_DIM, NSA_HEAD_DIM), (CMP_LEN * NSA_HEAD_DIM) ** -0.5),
        'cmp_pos': nrm(ks[5], (DEPTH, CMP_LEN, NSA_HEAD_DIM), 0.02),
        'hg_lb_logits': nrm(ks[6], (DEPTH + 1, HG_HEADS * HG_KEY_DIM), 0.5),
        'hg_norm': 1.0 + nrm(ks[7], (DEPTH, HG_VAL_DIM), 0.01),
        'w_br_nsa': nrm(ks[8], (DEPTH, NSA_WIDTH, D_MODEL), NSA_WIDTH ** -0.5),
        'w_br_hg': nrm(ks[9], (DEPTH, HG_WIDTH, D_MODEL), HG_WIDTH ** -0.5),
        'w_out': nrm(ks[10], (DEPTH, D_MODEL, D_MODEL), D_MODEL ** -0.5),
        'ffn_norm': 1.0 + nrm(ks[11], (DEPTH, D_MODEL), 0.01),
        'w_grp': nrm(ks[12], (DEPTH, D_MODEL, N_GROUPS), D_MODEL ** -0.5),
        'b_grp': nrm(ks[13], (DEPTH, N_GROUPS), 0.01),
        'w_rtr': nrm(ks[14], (DEPTH, D_MODEL, N_EXPERTS), D_MODEL ** -0.5),
        'b_rtr': nrm(ks[15], (DEPTH, N_EXPERTS), 0.01),
        'w_gate': nrm(ks[16], (DEPTH, N_EXPERTS, D_MODEL, D_FF_EXPERT), D_MODEL ** -0.5),
        'w_up': nrm(ks[17], (DEPTH, N_EXPERTS, D_MODEL, D_FF_EXPERT), D_MODEL ** -0.5),
        'w_down': nrm(ks[18], (DEPTH, N_EXPERTS, D_FF_EXPERT, D_MODEL), D_FF_EXPERT ** -0.5),
        'final_norm': 1.0 + nrm(ks[19], (D_MODEL,), 0.01),
    }


def reference(x, attn_norm, w_in, w_cmp_k, w_cmp_v, cmp_pos, hg_lb_logits, hg_norm, w_br_nsa, w_br_hg,
              w_out, ffn_norm, w_grp, b_grp, w_rtr, b_rtr, w_gate, w_up, w_down, final_norm):
    b, s, _ = x.shape
    lower_bounds = jnp.cumsum(jax.nn.softmax(hg_lb_logits.astype(jnp.float32), axis=0), axis=0)
    split_points = np.cumsum(IN_SPLITS)[:-1].tolist()
    for layer in range(DEPTH):
        h = rmsnorm(x, attn_norm[layer])
        proj = h @ w_in[layer]
        nsa_q, nsa_kv, nsa_gate, hg_q, hg_f, hg_i, hg_g, merge_logits = jnp.split(proj, split_points, axis=-1)
        kv = nsa_kv.reshape(b, s, 6, NSA_KV_GROUPS, NSA_HEAD_DIM)
        o_nsa = nsa_mixer(nsa_q.reshape(b, s, NSA_HEADS, NSA_HEAD_DIM),
                          kv[:, :, 0], kv[:, :, 1], kv[:, :, 2], kv[:, :, 3], kv[:, :, 4], kv[:, :, 5],
                          nsa_gate.reshape(b, s, NSA_HEADS, 3),
                          w_cmp_k[layer], w_cmp_v[layer], cmp_pos[layer])
        lb = lower_bounds[layer].reshape(HG_HEADS, HG_KEY_DIM)
        o_hg = hgrn2_mixer(hg_q, hg_f, hg_i, lb)
        o_hg = rmsnorm(o_hg, hg_norm[layer]) * jax.nn.silu(
            hg_g.astype(jnp.float32).reshape(b, s, HG_HEADS, HG_VAL_DIM))
        o_hg = o_hg.reshape(b, s, HG_WIDTH).astype(x.dtype)
        gate_nsa, gate_hg = jnp.split(jax.nn.sigmoid(merge_logits), 2, axis=-1)
        mixed = (gate_nsa * (o_nsa.astype(x.dtype) @ w_br_nsa[layer]) +
                 gate_hg * (o_hg @ w_br_hg[layer]))
        x = x + mixed @ w_out[layer]
        x = x + hier_moe(rmsnorm(x, ffn_norm[layer]), w_grp[layer], b_grp[layer], w_rtr[layer], b_rtr[layer],
                         w_gate[layer], w_up[layer], w_down[layer])
    return rmsnorm(x, final_norm)
```

```python
import functools

import numpy as np
import jax
import jax.numpy as jnp
from jax import lax
from jax.experimental import pallas as pl
from jax.experimental.pallas import tpu as pltpu

F32 = jnp.float32
BF16 = jnp.bfloat16

NSA_HEADS = 8
NSA_KV_GROUPS = 2
NSA_HEAD_DIM = 64
NSA_R = NSA_HEADS // NSA_KV_GROUPS
CMP_LEN = 32
CMP_STRIDE = 16
SLC_LEN = 64
SLC_SHIFT = 6
SLC_TOPN = 16
WINDOW = 512
Q_BLOCK = 128
ROPE_THETA = 500000.0
ROPE_DIM = NSA_HEAD_DIM // 4
HG_HEADS = 4
HG_DIM = 128
HG_CHUNK = 64
N_GROUPS = 4
EXPERTS_PER_GROUP = 8
N_EXPERTS = N_GROUPS * EXPERTS_PER_GROUP
RMS_EPS = 1e-6
NEG = -1e30
FORCE_SCORE = 1e4

LANES = 128
VMEM_LIMIT = 56 * 1024 * 1024

NT_DIMS = (((1,), (1,)), ((), ()))
TN_DIMS = (((0,), (0,)), ((), ()))


def _cparams(sem):
    return pltpu.CompilerParams(dimension_semantics=sem, vmem_limit_bytes=VMEM_LIMIT)


IN_TM = 256
IN_SEGS = (("q", 512, 512), ("kv", 768, 768), ("gate", 24, 128), ("hq", 512, 512),
           ("hf", 512, 512), ("hi", 512, 512), ("hg", 512, 512), ("merge", 2048, 2048))


def _in_proj_kernel(x_ref, g_ref, w_ref, *out_refs):
    x = x_ref[...]
    h = x * lax.rsqrt(jnp.mean(x * x, axis=-1, keepdims=True) + RMS_EPS) * g_ref[...]
    h = h.astype(BF16)
    off = 0
    for (_, _, wpad), o_ref in zip(IN_SEGS, out_refs):
        y = jnp.dot(h, w_ref[:, off:off + wpad], preferred_element_type=F32)
        o_ref[...] = y.astype(o_ref.dtype)
        off += wpad


def _in_proj(x2, attn_norm, w_in):
    T, D = x2.shape
    pieces, off = [], 0
    for _, w, wpad in IN_SEGS:
        p = w_in[:, off:off + w]
        if wpad != w:
            p = jnp.pad(p, ((0, 0), (0, wpad - w)))
        pieces.append(p)
        off += w
    wcat = jnp.concatenate(pieces, axis=1).astype(BF16)
    NP = wcat.shape[1]
    out_shapes = [jax.ShapeDtypeStruct((T, wpad), F32 if name == "gate" else BF16)
                  for name, _, wpad in IN_SEGS]
    out_specs = [pl.BlockSpec((IN_TM, wpad), lambda i: (i, 0)) for _, _, wpad in IN_SEGS]
    return pl.pallas_call(
        _in_proj_kernel,
        grid=(T // IN_TM,),
        in_specs=[pl.BlockSpec((IN_TM, D), lambda i: (i, 0)),
                  pl.BlockSpec((1, D), lambda i: (0, 0)),
                  pl.BlockSpec((D, NP), lambda i: (0, 0))],
        out_specs=out_specs,
        out_shape=out_shapes,
        compiler_params=_cparams(("parallel",)),
        name="in_proj",
    )(x2, attn_norm.reshape(1, D), wcat)


def _rope_tables(pos):
    n = pos.shape[0]
    half = ROPE_DIM // 2
    inv_freq = ROPE_THETA ** (-jnp.arange(half, dtype=F32) / half)
    ang = pos.astype(F32)[:, None] * inv_freq[None, :]
    cos, sin = jnp.cos(ang), jnp.sin(ang)
    rest = NSA_HEAD_DIM - ROPE_DIM
    z8 = jnp.zeros((n, half), F32)
    c64 = jnp.concatenate([cos, cos, jnp.ones((n, rest), F32)], axis=1)
    s1 = jnp.concatenate([-sin, z8, jnp.zeros((n, rest), F32)], axis=1)
    s2 = jnp.concatenate([z8, sin, jnp.zeros((n, rest), F32)], axis=1)
    tile = lambda a: jnp.concatenate([a, a], axis=1)
    return tile(c64), tile(s1), tile(s2)


def _rope(x, c, s1, s2):
    half = ROPE_DIM // 2
    return x * c + pltpu.roll(x, LANES - half, 1) * s1 + pltpu.roll(x, half, 1) * s2


PREP_TS = 512


def _prep_kernel(q_ref, kv_ref, gate_ref, c_ref, s1_ref, s2_ref,
                 qp_ref, ks_ref, vs_ref, kw_ref, vw_ref, gt_ref):
    c, s1, s2 = c_ref[...], s1_ref[...], s2_ref[...]
    lane = lax.broadcasted_iota(jnp.int32, c.shape, 1)
    lo = lane < NSA_HEAD_DIM

    def split(x):
        return (jnp.where(lo, x, 0.0), jnp.where(lo, pltpu.roll(x, NSA_HEAD_DIM, 1), 0.0))

    scale = NSA_HEAD_DIM ** -0.5
    for blk in range(NSA_HEADS // 2):
        x = q_ref[:, blk * LANES:(blk + 1) * LANES].astype(F32)
        a, b = split(_rope(x, c, s1, s2) * scale)
        qp_ref[0, 2 * blk] = a.astype(BF16)
        qp_ref[0, 2 * blk + 1] = b.astype(BF16)
    for blk, ref, rot in ((2, ks_ref, True), (3, vs_ref, False), (4, kw_ref, True), (5, vw_ref, False)):
        x = kv_ref[:, blk * LANES:(blk + 1) * LANES].astype(F32)
        if rot:
            x = _rope(x, c, s1, s2)
        a, b = split(x)
        ref[0, 0] = a.astype(BF16)
        ref[0, 1] = b.astype(BF16)
    sg = jax.nn.sigmoid(gate_ref[...])
    gt_ref[0, 0] = sg
    gt_ref[0, 1] = pltpu.roll(sg, LANES - 3 * NSA_R, 1)


def _nsa_prep(q, kv, gate, B, S):
    c, s1, s2 = _rope_tables(jnp.arange(S))
    nj = S // PREP_TS
    tok = lambda w: pl.BlockSpec((PREP_TS, w), lambda b, j: (b * nj + j, 0))
    tab = pl.BlockSpec((PREP_TS, LANES), lambda b, j: (j, 0))
    grp = lambda n, dt: (jax.ShapeDtypeStruct((B, n, S, LANES), dt),
                         pl.BlockSpec((1, n, PREP_TS, LANES), lambda b, j: (b, 0, j, 0)))
    outs = [grp(NSA_HEADS, BF16)] + [grp(NSA_KV_GROUPS, BF16)] * 4 + [grp(NSA_KV_GROUPS, F32)]
    return pl.pallas_call(
        _prep_kernel,
        grid=(B, nj),
        in_specs=[tok(512), tok(768), tok(LANES), tab, tab, tab],
        out_specs=[o[1] for o in outs],
        out_shape=[o[0] for o in outs],
        compiler_params=_cparams(("parallel", "parallel")),
        name="nsa_prep",
    )(q, kv, gate, c, s1, s2)


def _compress_kernel(tk_ref, tv_ref, wk_ref, wv_ref, pos_ref, c_ref, s1_ref, s2_ref, kc_ref, vc_ref):
    half = CMP_STRIDE * NSA_HEAD_DIM
    nc = tk_ref.shape[2]

    def comp(t_ref, w_ref):
        t = t_ref[0, 0]
        a = jnp.dot(t, w_ref[0:half, :], preferred_element_type=F32)
        b = jnp.dot(t, w_ref[half:2 * half, :], preferred_element_type=F32)
        c0 = jnp.dot(pos_ref[...], w_ref[...], preferred_element_type=F32)[0:1, :]
        return a + pltpu.roll(b, nc - 1, 0) + c0

    kc = _rope(comp(tk_ref, wk_ref), c_ref[...], s1_ref[...], s2_ref[...])
    kc_ref[0, 0] = kc.astype(BF16)
    vc_ref[0, 0] = comp(tv_ref, wv_ref).astype(BF16)


def _compress(kv, w_cmp_k, w_cmp_v, cmp_pos, B, S):
    G, dk = NSA_KV_GROUPS, NSA_HEAD_DIM
    nc = S // CMP_STRIDE
    width = CMP_STRIDE * dk

    def chunks(t):
        return t.reshape(B, nc, CMP_STRIDE, G, dk).transpose(0, 3, 1, 2, 4).reshape(B, G, nc, width)

    tk = chunks(kv[:, 0:G * dk])
    tv = chunks(kv[:, G * dk:2 * G * dk])
    padw = lambda w: jnp.pad(w, ((0, 0), (0, LANES - dk))).astype(BF16)
    pos8 = jnp.broadcast_to(cmp_pos.reshape(1, CMP_LEN * dk), (8, CMP_LEN * dk)).astype(BF16)
    c, s1, s2 = _rope_tables(jnp.arange(nc) * CMP_STRIDE)
    tspec = pl.BlockSpec((1, 1, nc, width), lambda b, g: (b, g, 0, 0))
    wspec = pl.BlockSpec((CMP_LEN * dk, LANES), lambda b, g: (0, 0))
    tab = pl.BlockSpec((nc, LANES), lambda b, g: (0, 0))
    ospec = pl.BlockSpec((1, 1, nc, LANES), lambda b, g: (b, g, 0, 0))
    oshape = jax.ShapeDtypeStruct((B, G, nc, LANES), BF16)
    return pl.pallas_call(
        _compress_kernel,
        grid=(B, G),
        in_specs=[tspec, tspec, wspec, wspec,
                  pl.BlockSpec((8, CMP_LEN * dk), lambda b, g: (0, 0)), tab, tab, tab],
        out_specs=[ospec, ospec],
        out_shape=[oshape, oshape],
        compiler_params=_cparams(("parallel", "parallel")),
        name="compress",
    )(tk, tv, padw(w_cmp_k), padw(w_cmp_v), pos8, c, s1, s2)


SEL_KC = 512
WIN_KEYS = WINDOW + Q_BLOCK


def _softmax_rows(s):
    m = jnp.max(s, axis=1, keepdims=True)
    e = jnp.exp(s - m)
    return e, jnp.sum(e, axis=1, keepdims=True)


def _nsa_kernel(q_ref, kc_ref, vc_ref, ks_ref, vs_ref, kw_ref, vw_ref, gt_ref, nege_ref, ov_ref,
                o_ref, m_sc, l_sc, acc_sc, *, n_sel):
    R, QB = NSA_R, Q_BLOCK
    rows = R * QB
    i = pl.program_id(2)
    t0 = i * QB
    q = q_ref[0].reshape(rows, LANES)

    def tpos_like(n):
        return t0 + (lax.broadcasted_iota(jnp.int32, (rows, n), 0) & (QB - 1))

    nc = kc_ref.shape[2]
    sc = lax.dot_general(q, kc_ref[0, 0], NT_DIMS, preferred_element_type=F32)
    n_idx = lax.broadcasted_iota(jnp.int32, (rows, nc), 1)
    cmask = (n_idx * CMP_STRIDE + (CMP_LEN - 1)) <= tpos_like(nc)
    ec, lc = _softmax_rows(jnp.where(cmask, sc, NEG))
    pc = jnp.where(cmask, ec * (1.0 / lc), 0.0)
    o_cmp = jnp.dot(pc.astype(BF16), vc_ref[0, 0], preferred_element_type=F32)

    psum = pc[0:QB] + pc[QB:2 * QB] + pc[2 * QB:3 * QB] + pc[3 * QB:4 * QB]
    p_hi = psum.astype(BF16)
    p_lo = (psum - p_hi.astype(F32)).astype(BF16)
    ov = ov_ref[...]
    imp = (jnp.dot(p_hi, ov, preferred_element_type=F32) +
           jnp.dot(p_lo, ov, preferred_element_type=F32))
    j = lax.broadcasted_iota(jnp.int32, (QB, LANES), 1)
    tq = t0 + lax.broadcasted_iota(jnp.int32, (QB, LANES), 0)
    force = (j == jnp.right_shift(tq, SLC_SHIFT)) | (j == 0)
    valid = (j * SLC_LEN) <= tq
    score = jnp.where(force, FORCE_SCORE, jnp.where(valid, imp, -1.0))
    work = jnp.where(j < n_sel, score, -jnp.inf)
    notsel = jnp.ones((QB, LANES), F32)
    jf = j.astype(F32)
    for _ in range(min(SLC_TOPN, n_sel)):
        mx = jnp.max(work, axis=1, keepdims=True)
        idx = jnp.min(jnp.where(work == mx, jf, float(LANES)), axis=1, keepdims=True)
        hit = jf == idx
        notsel = jnp.where(hit, 0.0, notsel)
        work = jnp.where(hit, -jnp.inf, work)

    ns_b = notsel.astype(BF16)
    lhs = jnp.concatenate([q, jnp.concatenate([ns_b] * R, axis=0)], axis=1)
    m_sc[...] = jnp.full(m_sc.shape, NEG, F32)
    l_sc[...] = jnp.zeros(l_sc.shape, F32)
    acc_sc[...] = jnp.zeros(acc_sc.shape, F32)
    n_chunks = (t0 + QB + SEL_KC - 1) // SEL_KC

    def sel_step(c, carry):
        k0 = pl.multiple_of(c * SEL_KC, SEL_KC)
        kaug = jnp.concatenate([ks_ref[0, 0, pl.ds(k0, SEL_KC), :],
                                nege_ref[pl.ds(k0, SEL_KC), :]], axis=1)
        s = lax.dot_general(lhs, kaug, NT_DIMS, preferred_element_type=F32)
        kpos = k0 + lax.broadcasted_iota(jnp.int32, (rows, SEL_KC), 1)
        s = jnp.where(kpos <= tpos_like(SEL_KC), s, NEG)
        m_old = m_sc[...]
        m_new = jnp.maximum(m_old, jnp.max(s, axis=1, keepdims=True))
        alpha = jnp.exp(m_old - m_new)
        p = jnp.exp(s - m_new)
        l_sc[...] = alpha * l_sc[...] + jnp.sum(p, axis=1, keepdims=True)
        acc_sc[...] = alpha * acc_sc[...] + jnp.dot(
            p.astype(BF16), vs_ref[0, 0, pl.ds(k0, SEL_KC), :], preferred_element_type=F32)
        m_sc[...] = m_new
        return carry

    lax.fori_loop(0, n_chunks, sel_step, 0)
    o_slc = acc_sc[...] * (1.0 / l_sc[...])

    start = pl.multiple_of(jnp.maximum(t0 - WINDOW, 0), QB)
    sw = lax.dot_general(q, kw_ref[0, 0, pl.ds(start, WIN_KEYS), :], NT_DIMS,
                         preferred_element_type=F32)
    rel = tpos_like(WIN_KEYS) - (start + lax.broadcasted_iota(jnp.int32, (rows, WIN_KEYS), 1))
    wmask = (rel >= 0) & (rel < WINDOW)
    ew, lw = _softmax_rows(jnp.where(wmask, sw, NEG))
    pw = jnp.where(wmask, ew, 0.0)
    o_win = jnp.dot(pw.astype(BF16), vw_ref[0, 0, pl.ds(start, WIN_KEYS), :],
                    preferred_element_type=F32) * (1.0 / lw)

    gt = gt_ref[0, 0]
    outs = []
    for r in range(R):
        sl = slice(r * QB, (r + 1) * QB)
        outs.append(gt[:, 3 * r:3 * r + 1] * o_cmp[sl] + gt[:, 3 * r + 1:3 * r + 2] * o_slc[sl] +
                    gt[:, 3 * r + 2:3 * r + 3] * o_win[sl])
    pairs = [outs[2 * p] + pltpu.roll(outs[2 * p + 1], NSA_HEAD_DIM, 1) for p in range(R // 2)]
    o_ref[0] = jnp.concatenate(pairs, axis=1).astype(o_ref.dtype)


def _nsa_attn(qp, kc, vc, ks, vs, kw, vw, gt, B, S):
    G, R, QB = NSA_KV_GROUPS, NSA_R, Q_BLOCK
    nc = S // CMP_STRIDE
    n_sel = S // SLC_LEN
    assert n_sel <= LANES and n_sel >= SLC_TOPN and S % SEL_KC == 0 and S >= WIN_KEYS
    key = np.arange(S)
    nege = jnp.asarray(np.where(key[:, None] // SLC_LEN == np.arange(LANES)[None, :], NEG, 0.0), BF16)
    cs = np.arange(nc)[:, None] * CMP_STRIDE
    ss = np.arange(LANES)[None, :] * SLC_LEN
    ovm = (cs < ss + SLC_LEN) & (cs + CMP_LEN - 1 >= ss) & (np.arange(LANES)[None, :] < n_sel) \
        & (np.arange(nc)[:, None] < nc - 1)
    ov = jnp.asarray(ovm, BF16)
    full = lambda n: pl.BlockSpec((1, 1, n, LANES), lambda b, g, i: (b, g, 0, 0))
    rows = R * QB
    return pl.pallas_call(
        functools.partial(_nsa_kernel, n_sel=n_sel),
        grid=(B, G, S // QB),
        in_specs=[pl.BlockSpec((1, R, QB, LANES), lambda b, g, i: (b, g, i, 0)),
                  full(nc), full(nc), full(S), full(S), full(S), full(S),
                  pl.BlockSpec((1, 1, QB, LANES), lambda b, g, i: (b, g, i, 0)),
                  pl.BlockSpec((S, LANES), lambda b, g, i: (0, 0)),
                  pl.BlockSpec((nc, LANES), lambda b, g, i: (0, 0))],
        out_specs=pl.BlockSpec((1, QB, R * NSA_HEAD_DIM), lambda b, g, i: (b, i, g)),
        out_shape=jax.ShapeDtypeStruct((B, S, NSA_HEADS * NSA_HEAD_DIM), BF16),
        scratch_shapes=[pltpu.VMEM((rows, 1), F32), pltpu.VMEM((rows, 1), F32),
                        pltpu.VMEM((rows, LANES), F32)],
        compiler_params=_cparams(("parallel", "parallel", "arbitrary")),
        name="nsa_attn",
    )(qp, kc, vc, ks, vs, kw, vw, gt, nege, ov)


HG_TB = 256
HG_LEVELS = 6


def _hgrn_consts():
    C = HG_CHUNK
    M = np.zeros((HG_LEVELS + 2, C, C), np.float32)
    RM = np.zeros((HG_LEVELS, C, 1), np.float32)
    BM = np.zeros((HG_LEVELS + 1, C, C), np.float32)
    for lv in range(HG_LEVELS):
        m = 1 << lv
        for t in range(C):
            bs = (t // (2 * m)) * (2 * m)
            mid = bs + m - 1
            if t - bs >= m:
                RM[lv, t, 0] = 1.0
                M[lv, t, mid + 1:t + 1] = 1.0
            else:
                M[lv, t, t + 1:mid + 1] = 1.0
            BM[lv, t, bs:bs + 2 * m] = 1.0
    BM[HG_LEVELS] = np.eye(C, dtype=np.float32)
    M[HG_LEVELS] = np.tril(np.ones((C, C), np.float32))
    M[HG_LEVELS + 1] = np.triu(np.ones((C, C), np.float32), 1)
    mm = M.reshape(-1, C)
    mm3 = np.concatenate([mm, mm, mm], axis=1)
    rm = np.broadcast_to(RM, (HG_LEVELS, C, HG_DIM)).copy()
    return jnp.asarray(mm3, BF16), jnp.asarray(rm, F32), jnp.asarray(BM, F32)


def _hgrn_kernel(q_ref, f_ref, i_ref, g_ref, lbl_ref, gn_ref, mm_ref, rm_ref, bm_ref, o_ref, st_sc):
    C, D = HG_CHUNK, HG_DIM

    @pl.when(pl.program_id(1) == 0)
    def _():
        st_sc[...] = jnp.zeros(st_sc.shape, F32)

    lbl = lbl_ref[...]
    mxl = jnp.max(lbl, axis=0, keepdims=True)
    el = jnp.exp(lbl - mxl)
    lb_all = el[0:1, :] / jnp.sum(el, axis=0, keepdims=True)
    mm = mm_ref[...]
    gn = gn_ref[...]
    for h in range(HG_HEADS):
        hs = slice(h * D, (h + 1) * D)
        lb = lb_all[:, hs]
        for ch in range(HG_TB // C):
            rs = slice(ch * C, (ch + 1) * C)
            q = q_ref[0, rs, hs].astype(F32)
            z = f_ref[0, rs, hs].astype(F32)
            v = i_ref[0, rs, hs]
            gg = g_ref[0, rs, hs].astype(F32)
            f = lb + (1.0 - lb) * jax.nn.sigmoid(z)
            lg = jnp.log(f)
            kk = 1.0 - f
            g1 = lg.astype(BF16)
            r1 = lg - g1.astype(F32)
            g2 = r1.astype(BF16)
            g3 = (r1 - g2.astype(F32)).astype(BF16)
            ez = jnp.exp(jnp.dot(mm, jnp.concatenate([g1, g2, g3], axis=0),
                                 preferred_element_type=F32))
            attn = bm_ref[HG_LEVELS] * lax.dot_general(
                q.astype(BF16), kk.astype(BF16), NT_DIMS, preferred_element_type=F32)
            for lv in range(HG_LEVELS):
                zl = ez[lv * C:(lv + 1) * C]
                rm = rm_ref[lv]
                ql = (q * zl * rm).astype(BF16)
                kl = (kk * zl * (1.0 - rm)).astype(BF16)
                attn = attn + bm_ref[lv] * lax.dot_general(ql, kl, NT_DIMS, preferred_element_type=F32)
            zc = ez[HG_LEVELS * C:(HG_LEVELS + 1) * C]
            zs = ez[(HG_LEVELS + 1) * C:(HG_LEVELS + 2) * C]
            st = st_sc[h]
            o = jnp.dot(attn.astype(BF16), v, preferred_element_type=F32)
            o = o + lax.dot_general((q * zc).astype(BF16), st.astype(BF16), NT_DIMS,
                                    preferred_element_type=F32)
            st_sc[h] = zc[C - 1:C, :] * st + lax.dot_general(
                v, (kk * zs).astype(BF16), TN_DIMS, preferred_element_type=F32)
            y = o * lax.rsqrt(jnp.mean(o * o, axis=-1, keepdims=True) + RMS_EPS) * gn
            o_ref[0, rs, hs] = (y * (gg * jax.nn.sigmoid(gg))).astype(o_ref.dtype)


def _hgrn(hq, hf, hi, hg, lb_logits, hg_norm, B, S):
    W = HG_HEADS * HG_DIM
    mm3, rm, bm = _hgrn_consts()
    r3 = lambda a: a.reshape(B, S, W)
    tok = pl.BlockSpec((1, HG_TB, W), lambda b, j: (b, j, 0))
    const = lambda a: pl.BlockSpec(a.shape, lambda b, j: (0,) * a.ndim)
    gn = hg_norm.reshape(1, HG_DIM)
    return pl.pallas_call(
        _hgrn_kernel,
        grid=(B, S // HG_TB),
        in_specs=[tok, tok, tok, tok, const(lb_logits), const(gn), const(mm3), const(rm), const(bm)],
        out_specs=tok,
        out_shape=jax.ShapeDtypeStruct((B, S, W), BF16),
        scratch_shapes=[pltpu.VMEM((HG_HEADS, HG_DIM, HG_DIM), F32)],
        compiler_params=_cparams(("parallel", "arbitrary")),
        name="hgrn2",
    )(r3(hq), r3(hf), r3(hi), r3(hg), lb_logits, gn, mm3, rm, bm)


MG_TM = 256


def _first_lane(cond, lane_f):
    return jnp.min(jnp.where(cond, lane_f, float(LANES)), axis=1, keepdims=True)


def _merge_kernel(on_ref, oh_ref, mg_ref, x_ref, wn_ref, wh_ref, wo_ref, fn_ref, wr1_ref, wr2_ref, br_ref,
                  x1_ref, h2_ref, rt_ref):
    D = x_ref.shape[1]
    a = jnp.dot(on_ref[...], wn_ref[...], preferred_element_type=F32)
    b = jnp.dot(oh_ref[...], wh_ref[...], preferred_element_type=F32)
    mixed = (jax.nn.sigmoid(mg_ref[:, 0:D].astype(F32)) * a +
             jax.nn.sigmoid(mg_ref[:, D:2 * D].astype(F32)) * b)
    x1 = x_ref[...] + jnp.dot(mixed.astype(BF16), wo_ref[...], preferred_element_type=F32)
    x1_ref[...] = x1
    h2 = x1 * lax.rsqrt(jnp.mean(x1 * x1, axis=-1, keepdims=True) + RMS_EPS) * fn_ref[...]
    h2_ref[...] = h2
    h_hi = h2.astype(BF16)
    h_lo = (h2 - h_hi.astype(F32)).astype(BF16)
    logits = (jnp.dot(h_lo, wr1_ref[...], preferred_element_type=F32) +
              jnp.dot(h_hi, wr2_ref[...], preferred_element_type=F32) +
              jnp.dot(h_hi, wr1_ref[...], preferred_element_type=F32)) + br_ref[...]
    lane = lax.broadcasted_iota(jnp.int32, logits.shape, 1)
    ninf = -jnp.inf
    gl = jnp.where(lane < N_GROUPS, logits, ninf)
    gmax = jnp.max(gl, axis=1, keepdims=True)
    lane_f = lane.astype(F32)
    gsel = _first_lane(gl == gmax, lane_f)
    gprob = 1.0 / jnp.sum(jnp.exp(gl - gmax), axis=1, keepdims=True)
    lo = N_GROUPS + gsel * EXPERTS_PER_GROUP
    el = jnp.where((lane_f >= lo) & (lane_f < lo + EXPERTS_PER_GROUP), logits, ninf)
    v1 = jnp.max(el, axis=1, keepdims=True)
    i1 = _first_lane(el == v1, lane_f)
    el2 = jnp.where(lane_f == i1, ninf, el)
    v2 = jnp.max(el2, axis=1, keepdims=True)
    i2 = _first_lane(el2 == v2, lane_f)
    e2 = jnp.exp(v2 - v1)
    w1 = gprob / (1.0 + e2)
    w2 = gprob * e2 / (1.0 + e2)
    rt = jnp.where(lane == 0, i1 - N_GROUPS,
                   jnp.where(lane == 1, i2 - N_GROUPS,
                             jnp.where(lane == 2, w1, jnp.where(lane == 3, w2, 0.0))))
    rt_ref[...] = rt


def _merge(o_nsa, o_hg, mg, x2, w_br_nsa, w_br_hg, w_out, ffn_norm, w_grp, b_grp, w_rtr, b_rtr):
    T, D = x2.shape
    nr = N_GROUPS + N_EXPERTS
    wr = jnp.pad(jnp.concatenate([w_grp, w_rtr], axis=1), ((0, 0), (0, LANES - nr)))
    wr1 = wr.astype(BF16)
    wr2 = (wr - wr1.astype(F32)).astype(BF16)
    br = jnp.pad(jnp.concatenate([b_grp, b_rtr]), (0, LANES - nr)).reshape(1, LANES)
    tok = lambda w: pl.BlockSpec((MG_TM, w), lambda i: (i, 0))
    const = lambda a: pl.BlockSpec(a.shape, lambda i: (0,) * a.ndim)
    wn, wh, wo = w_br_nsa.astype(BF16), w_br_hg.astype(BF16), w_out.astype(BF16)
    fn = ffn_norm.reshape(1, D)
    return pl.pallas_call(
        _merge_kernel,
        grid=(T // MG_TM,),
        in_specs=[tok(o_nsa.shape[1]), tok(o_hg.shape[1]), tok(2 * D), tok(D),
                  const(wn), const(wh), const(wo), const(fn), const(wr1), const(wr2), const(br)],
        out_specs=[tok(D), tok(D), tok(LANES)],
        out_shape=[jax.ShapeDtypeStruct((T, D), F32), jax.ShapeDtypeStruct((T, D), F32),
                   jax.ShapeDtypeStruct((T, LANES), F32)],
        compiler_params=_cparams(("parallel",)),
        name="merge",
    )(o_nsa, o_hg, mg, x2, wn, wh, wo, fn, wr1, wr2, br)


MOE_TM = 256


def _moe_kernel(te_ref, na_ref, src_ref, dst_ref, h_hbm, wg_ref, wu_ref, wd_ref, y_hbm,
                xbuf, ybuf, gsem, ssem):
    i = pl.program_id(0)
    tm = xbuf.shape[0]

    def gather_copy(r):
        return pltpu.make_async_copy(h_hbm.at[pl.ds(src_ref[0, 0, r], 1)], xbuf.at[pl.ds(r, 1)], gsem)

    def scatter_copy(r):
        return pltpu.make_async_copy(ybuf.at[pl.ds(r, 1)], y_hbm.at[pl.ds(dst_ref[0, 0, r], 1)], ssem)

    def each_row(fn):
        def body(r, c):
            fn(r)
            return c
        lax.fori_loop(0, tm, body, 0, unroll=8)

    @pl.when(i < na_ref[0])
    def _():
        each_row(lambda r: gather_copy(r).start())
        each_row(lambda r: gather_copy(r).wait())
        x = xbuf[...].astype(BF16)
        hg = jnp.dot(x, wg_ref[0], preferred_element_type=F32)
        hu = jnp.dot(x, wu_ref[0], preferred_element_type=F32)
        hid = (hg * jax.nn.sigmoid(hg) * hu).astype(BF16)
        ybuf[...] = jnp.dot(hid, wd_ref[0], preferred_element_type=F32)
        each_row(lambda r: pl.when(dst_ref[0, 0, r] >= 0)(lambda: scatter_copy(r).start()))
        each_row(lambda r: pl.when(dst_ref[0, 0, r] >= 0)(lambda: scatter_copy(r).wait()))


def _moe(h2, route, w_gate, w_up, w_down):
    T, D = h2.shape
    E, _, F = w_gate.shape
    tm = MOE_TM
    n_tiles = 2 * T // tm + E
    n_rows = n_tiles * tm
    ex = jnp.concatenate([route[:, 0], route[:, 1]]).astype(jnp.int32)
    order = jnp.argsort(ex, stable=True).astype(jnp.int32)
    ex_sorted = ex[order]
    counts = jnp.sum(ex[:, None] == jnp.arange(E, dtype=jnp.int32)[None, :], axis=0).astype(jnp.int32)
    padded = ((counts + tm - 1) // tm) * tm
    pstart = jnp.cumsum(padded) - padded
    cstart = jnp.cumsum(counts) - counts
    pos = pstart[ex_sorted] + (jnp.arange(2 * T, dtype=jnp.int32) - cstart[ex_sorted])
    src = jnp.zeros((n_rows,), jnp.int32).at[pos].set(order % T)
    dst = jnp.full((n_rows,), -1, jnp.int32).at[pos].set(order)
    n_active = (jnp.sum(padded) // tm).astype(jnp.int32)
    tile_start = jnp.arange(n_tiles, dtype=jnp.int32) * tm
    pend = jnp.cumsum(padded)
    te = jnp.sum(tile_start[:, None] >= pend[None, :], axis=1).astype(jnp.int32)
    last = jnp.maximum(n_active - 1, 0)
    te = jnp.where(jnp.arange(n_tiles) < n_active, te, te[last])
    te = jnp.minimum(te, E - 1)

    wg, wu, wd = w_gate.astype(BF16), w_up.astype(BF16), w_down.astype(BF16)
    idx_spec = pl.BlockSpec((1, 1, tm), lambda i, te, na: (i, 0, 0), memory_space=pltpu.SMEM)
    grid_spec = pltpu.PrefetchScalarGridSpec(
        num_scalar_prefetch=2,
        grid=(n_tiles,),
        in_specs=[idx_spec, idx_spec,
                  pl.BlockSpec(memory_space=pl.ANY),
                  pl.BlockSpec((1, D, F), lambda i, te, na: (te[i], 0, 0)),
                  pl.BlockSpec((1, D, F), lambda i, te, na: (te[i], 0, 0)),
                  pl.BlockSpec((1, F, D), lambda i, te, na: (te[i], 0, 0))],
        out_specs=pl.BlockSpec(memory_space=pl.ANY),
        scratch_shapes=[pltpu.VMEM((tm, D), F32), pltpu.VMEM((tm, D), F32),
                        pltpu.SemaphoreType.DMA, pltpu.SemaphoreType.DMA],
    )
    y = pl.pallas_call(
        _moe_kernel,
        grid_spec=grid_spec,
        out_shape=jax.ShapeDtypeStruct((2 * T, D), F32),
        compiler_params=_cparams(("arbitrary",)),
        name="moe",
    )(te, n_active.reshape(1), src.reshape(n_tiles, 1, tm), dst.reshape(n_tiles, 1, tm), h2, wg, wu, wd)
    return y


FIN_TM = 256


def _final_kernel(x1_ref, y1_ref, y2_ref, rt_ref, g_ref, o_ref):
    rt = rt_ref[...]
    x = x1_ref[...] + rt[:, 2:3] * y1_ref[...] + rt[:, 3:4] * y2_ref[...]
    o_ref[...] = x * lax.rsqrt(jnp.mean(x * x, axis=-1, keepdims=True) + RMS_EPS) * g_ref[...]


def _final(x1, y, route, final_norm):
    T, D = x1.shape
    nb = T // FIN_TM
    return pl.pallas_call(
        _final_kernel,
        grid=(nb,),
        in_specs=[pl.BlockSpec((FIN_TM, D), lambda i: (i, 0)),
                  pl.BlockSpec((FIN_TM, D), lambda i: (i, 0)),
                  pl.BlockSpec((FIN_TM, D), lambda i: (i + nb, 0)),
                  pl.BlockSpec((FIN_TM, LANES), lambda i: (i, 0)),
                  pl.BlockSpec((1, D), lambda i: (0, 0))],
        out_specs=pl.BlockSpec((FIN_TM, D), lambda i: (i, 0)),
        out_shape=jax.ShapeDtypeStruct((T, D), F32),
        compiler_params=_cparams(("parallel",)),
        name="final",
    )(x1, y, y, route, final_norm.reshape(1, D))


def kernel(x, attn_norm, w_in, w_cmp_k, w_cmp_v, cmp_pos, hg_lb_logits, hg_norm, w_br_nsa, w_br_hg, w_out,
           ffn_norm, w_grp, b_grp, w_rtr, b_rtr, w_gate, w_up, w_down, final_norm):
    B, S, D = x.shape
    assert attn_norm.shape[0] == 1, "single-layer block"
    x2 = x.reshape(B * S, D)
    q, kv, gate, hq, hf, hi, hg, mg = _in_proj(x2, attn_norm[0], w_in[0])
    qp, ks, vs, kw, vw, gt = _nsa_prep(q, kv, gate, B, S)
    kc, vc = _compress(kv, w_cmp_k[0], w_cmp_v[0], cmp_pos[0], B, S)
    o_nsa = _nsa_attn(qp, kc, vc, ks, vs, kw, vw, gt, B, S).reshape(B * S, -1)
    o_hg = _hgrn(hq, hf, hi, hg, hg_lb_logits, hg_norm[0], B, S).reshape(B * S, -1)
    x1, h2, route = _merge(o_nsa, o_hg, mg, x2, w_br_nsa[0], w_br_hg[0], w_out[0], ffn_norm[0],
                           w_grp[0], b_grp[0], w_rtr[0], b_rtr[0])
    y = _moe(h2, route, w_gate[0], w_up[0], w_down[0])
    out = _final(x1, y, route, final_norm)
    return out.reshape(B, S, D)
```

```python
import functools

import numpy as np
import jax
import jax.numpy as jnp
from jax import lax
from jax.experimental import pallas as pl
from jax.experimental.pallas import tpu as pltpu

F32 = jnp.float32
BF16 = jnp.bfloat16

NSA_HEADS = 8
NSA_KV_GROUPS = 2
NSA_HEAD_DIM = 64
NSA_R = NSA_HEADS // NSA_KV_GROUPS
CMP_LEN = 32
CMP_STRIDE = 16
SLC_LEN = 64
SLC_SHIFT = 6
SLC_TOPN = 16
WINDOW = 512
Q_BLOCK = 128
ROPE_THETA = 500000.0
ROPE_DIM = NSA_HEAD_DIM // 4
HG_HEADS = 4
HG_DIM = 128
HG_CHUNK = 64
N_GROUPS = 4
EXPERTS_PER_GROUP = 8
N_EXPERTS = N_GROUPS * EXPERTS_PER_GROUP
RMS_EPS = 1e-6
NEG = -1e30
FORCE_SCORE = 1e4

LANES = 128
VMEM_LIMIT = 56 * 1024 * 1024

NT_DIMS = (((1,), (1,)), ((), ()))
TN_DIMS = (((0,), (0,)), ((), ()))


def _cparams(sem):
    return pltpu.CompilerParams(dimension_semantics=sem, vmem_limit_bytes=VMEM_LIMIT)


IN_TM = 256
IN_SEGS = (("q", 512, 512), ("kv", 768, 768), ("gate", 24, 128), ("hq", 512, 512),
           ("hf", 512, 512), ("hi", 512, 512), ("hg", 512, 512), ("merge", 2048, 2048))


def _in_proj_kernel(x_ref, g_ref, w_ref, *out_refs):
    x = x_ref[...]
    h = x * lax.rsqrt(jnp.mean(x * x, axis=-1, keepdims=True) + RMS_EPS) * g_ref[...]
    h = h.astype(BF16)
    off = 0
    for (_, _, wpad), o_ref in zip(IN_SEGS, out_refs):
        y = jnp.dot(h, w_ref[:, off:off + wpad], preferred_element_type=F32)
        o_ref[...] = y.astype(o_ref.dtype)
        off += wpad


def _in_proj(x2, attn_norm, w_in):
    T, D = x2.shape
    pieces, off = [], 0
    for _, w, wpad in IN_SEGS:
        p = w_in[:, off:off + w]
        if wpad != w:
            p = jnp.pad(p, ((0, 0), (0, wpad - w)))
        pieces.append(p)
        off += w
    wcat = jnp.concatenate(pieces, axis=1).astype(BF16)
    NP = wcat.shape[1]
    out_shapes = [jax.ShapeDtypeStruct((T, wpad), F32 if name == "gate" else BF16)
                  for name, _, wpad in IN_SEGS]
    out_specs = [pl.BlockSpec((IN_TM, wpad), lambda i: (i, 0)) for _, _, wpad in IN_SEGS]
    return pl.pallas_call(
        _in_proj_kernel,
        grid=(T // IN_TM,),
        in_specs=[pl.BlockSpec((IN_TM, D), lambda i: (i, 0)),
                  pl.BlockSpec((1, D), lambda i: (0, 0)),
                  pl.BlockSpec((D, NP), lambda i: (0, 0))],
        out_specs=out_specs,
        out_shape=out_shapes,
        compiler_params=_cparams(("parallel",)),
        name="in_proj",
    )(x2, attn_norm.reshape(1, D), wcat)


def _rope_tables(pos):
    n = pos.shape[0]
    half = ROPE_DIM // 2
    inv_freq = ROPE_THETA ** (-jnp.arange(half, dtype=F32) / half)
    ang = pos.astype(F32)[:, None] * inv_freq[None, :]
    cos, sin = jnp.cos(ang), jnp.sin(ang)
    rest = NSA_HEAD_DIM - ROPE_DIM
    z8 = jnp.zeros((n, half), F32)
    c64 = jnp.concatenate([cos, cos, jnp.ones((n, rest), F32)], axis=1)
    s1 = jnp.concatenate([-sin, z8, jnp.zeros((n, rest), F32)], axis=1)
    s2 = jnp.concatenate([z8, sin, jnp.zeros((n, rest), F32)], axis=1)
    tile = lambda a: jnp.concatenate([a, a], axis=1)
    return tile(c64), tile(s1), tile(s2)


def _rope(x, c, s1, s2):
    half = ROPE_DIM // 2
    return x * c + pltpu.roll(x, LANES - half, 1) * s1 + pltpu.roll(x, half, 1) * s2


PREP_TS = 512


GATE_ROWS = 16
LOG2E = 1.4426950408889634


def _prep_kernel(q_ref, kv_ref, gate_ref, c_ref, s1_ref, s2_ref, nege_ref,
                 qt_ref, ksa_ref, vst_ref, kw_ref, vwt_ref, gtt_ref):
    c, s1, s2 = c_ref[...], s1_ref[...], s2_ref[...]
    lane = lax.broadcasted_iota(jnp.int32, c.shape, 1)
    lo = lane < NSA_HEAD_DIM
    dk, QB = NSA_HEAD_DIM, Q_BLOCK

    def split(x):
        return (jnp.where(lo, x, 0.0), jnp.where(lo, pltpu.roll(x, dk, 1), 0.0))

    scale = (dk ** -0.5) * LOG2E
    for blk in range(NSA_HEADS // 2):
        x = q_ref[:, blk * LANES:(blk + 1) * LANES].astype(F32)
        for hh, part in enumerate(split(_rope(x, c, s1, s2) * scale)):
            h = 2 * blk + hh
            g, r = h // NSA_R, h % NSA_R
            pt = part.T.astype(BF16)
            for i in range(PREP_TS // QB):
                qt_ref[0, g, i, :, r * QB:(r + 1) * QB] = pt[:, i * QB:(i + 1) * QB]
    nege = nege_ref[...]
    for g, part in enumerate(split(_rope(kv_ref[:, 2 * LANES:3 * LANES].astype(F32), c, s1, s2))):
        ksa_ref[0, g] = jnp.concatenate([part.astype(BF16), nege], axis=1)
    for g, part in enumerate(split(_rope(kv_ref[:, 4 * LANES:5 * LANES].astype(F32), c, s1, s2))):
        kw_ref[0, g] = part.astype(BF16)
    for blk, ref in ((3, vst_ref), (5, vwt_ref)):
        for g, part in enumerate(split(kv_ref[:, blk * LANES:(blk + 1) * LANES].astype(F32))):
            ref[0, g] = part.T[0:dk, :].astype(BF16)
    sg = jax.nn.sigmoid(gate_ref[...])
    gtt_ref[0, 0] = sg.T[0:GATE_ROWS, :]
    gtt_ref[0, 1] = pltpu.roll(sg, LANES - 3 * NSA_R, 1).T[0:GATE_ROWS, :]


def _nsa_prep(q, kv, gate, B, S):
    G, dk, QB = NSA_KV_GROUPS, NSA_HEAD_DIM, Q_BLOCK
    c, s1, s2 = _rope_tables(jnp.arange(S))
    key = np.arange(S)
    nege = jnp.asarray(np.where(key[:, None] // SLC_LEN == np.arange(LANES)[None, :], NEG, 0.0), BF16)
    nj = S // PREP_TS
    tok = lambda w: pl.BlockSpec((PREP_TS, w), lambda b, j: (b * nj + j, 0))
    tab = pl.BlockSpec((PREP_TS, LANES), lambda b, j: (j, 0))
    rows_out = lambda w, dt: (jax.ShapeDtypeStruct((B, G, S, w), dt),
                              pl.BlockSpec((1, G, PREP_TS, w), lambda b, j: (b, 0, j, 0)))
    cols_out = lambda n, dt: (jax.ShapeDtypeStruct((B, G, n, S), dt),
                              pl.BlockSpec((1, G, n, PREP_TS), lambda b, j: (b, 0, 0, j)))
    qt_out = (jax.ShapeDtypeStruct((B, G, S // QB, LANES, NSA_R * QB), BF16),
              pl.BlockSpec((1, G, PREP_TS // QB, LANES, NSA_R * QB), lambda b, j: (b, 0, j, 0, 0)))
    outs = [qt_out, rows_out(2 * LANES, BF16), cols_out(dk, BF16), rows_out(LANES, BF16),
            cols_out(dk, BF16), cols_out(GATE_ROWS, F32)]
    return pl.pallas_call(
        _prep_kernel,
        grid=(B, nj),
        in_specs=[tok(512), tok(768), tok(LANES), tab, tab, tab, tab],
        out_specs=[o[1] for o in outs],
        out_shape=[o[0] for o in outs],
        compiler_params=_cparams(("parallel", "parallel")),
        name="nsa_prep",
    )(q, kv, gate, c, s1, s2, nege)


def _compress_kernel(tk_ref, tv_ref, wk_ref, wv_ref, pos_ref, c_ref, s1_ref, s2_ref, kc_ref, vc_ref):
    half = CMP_STRIDE * NSA_HEAD_DIM
    nc = tk_ref.shape[2]

    def comp(t_ref, w_ref):
        t = t_ref[0, 0]
        a = jnp.dot(t, w_ref[0:half, :], preferred_element_type=F32)
        b = jnp.dot(t, w_ref[half:2 * half, :], preferred_element_type=F32)
        c0 = jnp.dot(pos_ref[...], w_ref[...], preferred_element_type=F32)[0:1, :]
        return a + pltpu.roll(b, nc - 1, 0) + c0

    kc = _rope(comp(tk_ref, wk_ref), c_ref[...], s1_ref[...], s2_ref[...])
    kc_ref[0, 0] = kc.astype(BF16)
    vc_ref[0, 0] = comp(tv_ref, wv_ref).T[0:NSA_HEAD_DIM, :].astype(BF16)


def _compress(kv, w_cmp_k, w_cmp_v, cmp_pos, B, S):
    G, dk = NSA_KV_GROUPS, NSA_HEAD_DIM
    nc = S // CMP_STRIDE
    width = CMP_STRIDE * dk

    def chunks(t):
        return t.reshape(B, nc, CMP_STRIDE, G, dk).transpose(0, 3, 1, 2, 4).reshape(B, G, nc, width)

    tk = chunks(kv[:, 0:G * dk])
    tv = chunks(kv[:, G * dk:2 * G * dk])
    padw = lambda w: jnp.pad(w, ((0, 0), (0, LANES - dk))).astype(BF16)
    pos8 = jnp.broadcast_to(cmp_pos.reshape(1, CMP_LEN * dk), (8, CMP_LEN * dk)).astype(BF16)
    c, s1, s2 = _rope_tables(jnp.arange(nc) * CMP_STRIDE)
    tspec = pl.BlockSpec((1, 1, nc, width), lambda b, g: (b, g, 0, 0))
    wspec = pl.BlockSpec((CMP_LEN * dk, LANES), lambda b, g: (0, 0))
    tab = pl.BlockSpec((nc, LANES), lambda b, g: (0, 0))
    ospec = pl.BlockSpec((1, 1, nc, LANES), lambda b, g: (b, g, 0, 0))
    oshape = jax.ShapeDtypeStruct((B, G, nc, LANES), BF16)
    return pl.pallas_call(
        _compress_kernel,
        grid=(B, G),
        in_specs=[tspec, tspec, wspec, wspec,
                  pl.BlockSpec((8, CMP_LEN * dk), lambda b, g: (0, 0)), tab, tab, tab],
        out_specs=[ospec, pl.BlockSpec((1, 1, dk, nc), lambda b, g: (b, g, 0, 0))],
        out_shape=[oshape, jax.ShapeDtypeStruct((B, G, dk, nc), BF16)],
        compiler_params=_cparams(("parallel", "parallel")),
        name="compress",
    )(tk, tv, padw(w_cmp_k), padw(w_cmp_v), pos8, c, s1, s2)


SEL_KC = 256
SEL_KC_SHIFT = 8
WIN_KEYS = WINDOW + Q_BLOCK


def _col_reduce(x, op, fin):
    n = x.shape[0]
    while (n // 2) % 8 == 0 and n > 8:
        n //= 2
        x = op(x[:n], x[n:])
    return fin(x, axis=0, keepdims=True)


def _col_max(x):
    return _col_reduce(x, jnp.maximum, jnp.max)


def _col_min(x):
    return _col_reduce(x, jnp.minimum, jnp.min)


def _col_sum(x):
    return _col_reduce(x, jnp.add, jnp.sum)


def _softmax_cols(s):
    e = jnp.exp2(s - _col_max(s))
    return e, _col_sum(e)


def _nsa_kernel(qt_ref, kc_ref, vct_ref, ksa_ref, vst_ref, kw_ref, vwt_ref, gtt_ref, ovt_ref,
                o_ref, rhs_sc, m0_sc, m1_sc, l0_sc, l1_sc, a0_sc, a1_sc, sa_sc, sb_sc, *, n_sel):
    m_scs, l_scs, acc_scs = (m0_sc, m1_sc), (l0_sc, l1_sc), (a0_sc, a1_sc)
    s_bufs = (sa_sc, sb_sc)
    R, QB = NSA_R, Q_BLOCK
    cols = R * QB
    i = pl.program_id(2)
    t0 = i * QB
    qt = qt_ref[0, 0, 0]

    def per_head(x):
        return jnp.concatenate([x] * (cols // x.shape[1]), axis=1)

    def rc(n):
        return (lax.broadcasted_iota(jnp.int32, (n, QB), 0),
                t0 + lax.broadcasted_iota(jnp.int32, (n, QB), 1))

    nc = kc_ref.shape[2]
    sc = jnp.dot(kc_ref[0, 0], qt, preferred_element_type=F32)
    n_idx, tq_c = rc(nc)
    cvis = (n_idx * CMP_STRIDE + (CMP_LEN - 1)) <= tq_c
    ec, lc = _softmax_cols(sc + per_head(jnp.where(cvis, 0.0, NEG)))
    pc = ec * per_head(jnp.where(cvis, 1.0, 0.0)) * (1.0 / lc)
    o_cmp = jnp.dot(vct_ref[0, 0], pc.astype(BF16), preferred_element_type=F32)

    start = pl.multiple_of(jnp.maximum(t0 - WINDOW, 0), QB)
    sw = jnp.dot(kw_ref[0, 0, pl.ds(start, WIN_KEYS), :], qt, preferred_element_type=F32)
    k_idx, tq_w = rc(WIN_KEYS)
    rel = tq_w - (start + k_idx)
    pw, lw = _softmax_cols(sw + per_head(jnp.where((rel >= 0) & (rel < WINDOW), 0.0, NEG)))
    o_win = jnp.dot(vwt_ref[0, 0, :, pl.ds(start, WIN_KEYS)], pw.astype(BF16),
                    preferred_element_type=F32) * (1.0 / lw)

    psum = pc[:, 0:QB] + pc[:, QB:2 * QB] + pc[:, 2 * QB:3 * QB] + pc[:, 3 * QB:4 * QB]
    p_hi = psum.astype(BF16)
    p_lo = (psum - p_hi.astype(F32)).astype(BF16)
    ovt = ovt_ref[...]
    imp = (jnp.dot(ovt, p_hi, preferred_element_type=F32) +
           jnp.dot(ovt, p_lo, preferred_element_type=F32))
    j = lax.broadcasted_iota(jnp.int32, (LANES, QB), 0)
    tq = t0 + lax.broadcasted_iota(jnp.int32, (LANES, QB), 1)
    force = (j == jnp.right_shift(tq, SLC_SHIFT)) | (j == 0)
    valid = (j * SLC_LEN) <= tq
    score = jnp.where(force, FORCE_SCORE, jnp.where(valid, imp, -1.0))
    work = jnp.where(j < n_sel, score, -jnp.inf)
    notsel = jnp.ones((LANES, QB), F32)
    jf = j.astype(F32)
    for _ in range(min(SLC_TOPN, n_sel)):
        mx = _col_max(work)
        idx = _col_min(jnp.where(work == mx, jf, float(LANES)))
        hit = jf == idx
        notsel = jnp.where(hit, 0.0, notsel)
        work = jnp.where(hit, -jnp.inf, work)

    rhs_sc[0:LANES, :] = qt
    rhs_sc[LANES:2 * LANES, :] = jnp.concatenate([notsel.astype(BF16)] * R, axis=1)
    halves = tuple(zip(m_scs, l_scs, acc_scs))
    hw = cols // len(halves)
    for m_sc, l_sc, acc_sc in halves:
        m_sc[...] = jnp.full(m_sc.shape, NEG, F32)
        l_sc[...] = jnp.zeros(l_sc.shape, F32)
        acc_sc[...] = jnp.zeros(acc_sc.shape, F32)

    def sel_scores(chunk_ids):
        kas = [ksa_ref[0, 0, pl.ds(pl.multiple_of(c * SEL_KC, SEL_KC), SEL_KC), :] for c in chunk_ids]
        return [[jnp.dot(ka, rhs_sc[:, hp * hw:(hp + 1) * hw], preferred_element_type=F32)
                 for ka in kas] for hp in range(len(halves))]

    def sel_update(chunk_ids, scores, diagonal):
        k0s = [pl.multiple_of(c * SEL_KC, SEL_KC) for c in chunk_ids]
        vts = [vst_ref[0, 0, :, pl.ds(k0, SEL_KC)] for k0 in k0s]
        for hp, (m_sc, l_sc, acc_sc) in enumerate(halves):
            ss = scores[hp]
            if diagonal:
                k_idx, tq_s = rc(SEL_KC)
                ss = [s + jnp.concatenate([jnp.where(k0 + k_idx <= tq_s, 0.0, NEG)] * (hw // QB), axis=1)
                      for k0, s in zip(k0s, ss)]
            m_old = m_sc[...]
            m_new = functools.reduce(jnp.maximum, [_col_max(s) for s in ss], m_old)
            alpha = jnp.exp2(m_old - m_new)
            ps = [jnp.exp2(s - m_new) for s in ss]
            l_sc[...] = alpha * l_sc[...] + functools.reduce(jnp.add, [_col_sum(p) for p in ps])
            pv = [jnp.dot(vt, p.astype(BF16), preferred_element_type=F32) for vt, p in zip(vts, ps)]
            acc_sc[...] = alpha * acc_sc[...] + functools.reduce(jnp.add, pv)
            m_sc[...] = m_new

    n_units = jnp.right_shift(t0, SEL_KC_SHIFT + 1)
    last_unit = jnp.maximum(n_units - 1, 0)
    nh = len(halves)

    def unit(u):
        return [2 * u, 2 * u + 1]

    def put_scores(buf, scores):
        for hp in range(nh):
            for c in range(2):
                buf[hp * 2 + c] = scores[hp][c]

    def get_scores(buf):
        return [[buf[hp * 2 + c] for c in range(2)] for hp in range(nh)]

    put_scores(s_bufs[0], sel_scores(unit(0)))

    def sel_step(u, carry):
        for parity in range(2):
            @pl.when((u & 1) == parity)
            def _():
                put_scores(s_bufs[1 - parity], sel_scores(unit(jnp.minimum(u + 1, last_unit))))
                sel_update(unit(u), get_scores(s_bufs[parity]), False)
        return carry

    lax.fori_loop(0, n_units, sel_step, 0)
    sel_update(unit(n_units), sel_scores(unit(n_units)), True)
    o_slc = jnp.concatenate([a[...] * (1.0 / l[...]) for _, l, a in halves], axis=1)

    gtt = gtt_ref[0, 0]
    outs = []
    for r in range(R):
        cs = slice(r * QB, (r + 1) * QB)
        outs.append(gtt[3 * r:3 * r + 1, :] * o_cmp[:, cs] + gtt[3 * r + 1:3 * r + 2, :] * o_slc[:, cs] +
                    gtt[3 * r + 2:3 * r + 3, :] * o_win[:, cs])
    o_ref[0] = jnp.concatenate(outs, axis=0).T.astype(o_ref.dtype)


def _nsa_attn(qt, kc, vct, ksa, vst, kw, vwt, gtt, B, S):
    G, R, QB, dk = NSA_KV_GROUPS, NSA_R, Q_BLOCK, NSA_HEAD_DIM
    nc = S // CMP_STRIDE
    n_sel = S // SLC_LEN
    assert n_sel <= LANES and n_sel >= SLC_TOPN and S % (2 * SEL_KC) == 0 and S >= WIN_KEYS
    assert SEL_KC == 1 << SEL_KC_SHIFT and SEL_KC % QB == 0
    cs = np.arange(nc)[None, :] * CMP_STRIDE
    ss = np.arange(LANES)[:, None] * SLC_LEN
    ovm = (cs < ss + SLC_LEN) & (cs + CMP_LEN - 1 >= ss) & (np.arange(LANES)[:, None] < n_sel) \
        & (np.arange(nc)[None, :] < nc - 1)
    ovt = jnp.asarray(ovm, BF16)
    rows_in = lambda n, w: pl.BlockSpec((1, 1, n, w), lambda b, g, i: (b, g, 0, 0))
    cols = R * QB
    return pl.pallas_call(
        functools.partial(_nsa_kernel, n_sel=n_sel),
        grid=(B, G, S // QB),
        in_specs=[pl.BlockSpec((1, 1, 1, LANES, cols), lambda b, g, i: (b, g, i, 0, 0)),
                  rows_in(nc, LANES), rows_in(dk, nc), rows_in(S, 2 * LANES), rows_in(dk, S),
                  rows_in(S, LANES), rows_in(dk, S),
                  pl.BlockSpec((1, 1, GATE_ROWS, QB), lambda b, g, i: (b, g, 0, i)),
                  pl.BlockSpec((LANES, nc), lambda b, g, i: (0, 0))],
        out_specs=pl.BlockSpec((1, QB, R * dk), lambda b, g, i: (b, i, g)),
        out_shape=jax.ShapeDtypeStruct((B, S, NSA_HEADS * dk), BF16),
        scratch_shapes=[pltpu.VMEM((2 * LANES, cols), BF16)] +
                       [pltpu.VMEM((1, cols // 2), F32)] * 4 + [pltpu.VMEM((dk, cols // 2), F32)] * 2 +
                       [pltpu.VMEM((4, SEL_KC, cols // 2), F32)] * 2,
        compiler_params=_cparams(("parallel", "parallel", "arbitrary")),
        name="nsa_attn",
    )(qt, kc, vct, ksa, vst, kw, vwt, gtt, ovt)


HG_TB = 256
HG_LEVELS = 6


def _hgrn_consts():
    C = HG_CHUNK
    M = np.zeros((HG_LEVELS + 2, C, C), np.float32)
    RM = np.zeros((HG_LEVELS, C, 1), np.float32)
    BM = np.zeros((HG_LEVELS + 1, C, C), np.float32)
    for lv in range(HG_LEVELS):
        m = 1 << lv
        for t in range(C):
            bs = (t // (2 * m)) * (2 * m)
            mid = bs + m - 1
            if t - bs >= m:
                RM[lv, t, 0] = 1.0
                M[lv, t, mid + 1:t + 1] = 1.0
            else:
                M[lv, t, t + 1:mid + 1] = 1.0
            BM[lv, t, bs:bs + 2 * m] = 1.0
    BM[HG_LEVELS] = np.eye(C, dtype=np.float32)
    M[HG_LEVELS] = np.tril(np.ones((C, C), np.float32))
    M[HG_LEVELS + 1] = np.triu(np.ones((C, C), np.float32), 1)
    mm = M.reshape(-1, C)
    mm3 = np.concatenate([mm, mm, mm], axis=1)
    rm = np.broadcast_to(RM, (HG_LEVELS, C, HG_DIM)).copy()
    return jnp.asarray(mm3, BF16), jnp.asarray(rm, F32), jnp.asarray(BM, F32)


def _hgrn_kernel(q_ref, f_ref, i_ref, g_ref, lbl_ref, gn_ref, mm_ref, rm_ref, bm_ref, o_ref, st_sc):
    C, D = HG_CHUNK, HG_DIM

    @pl.when(pl.program_id(1) == 0)
    def _():
        st_sc[...] = jnp.zeros(st_sc.shape, F32)

    lbl = lbl_ref[...]
    mxl = jnp.max(lbl, axis=0, keepdims=True)
    el = jnp.exp(lbl - mxl)
    lb_all = el[0:1, :] / jnp.sum(el, axis=0, keepdims=True)
    mm = mm_ref[...]
    gn = gn_ref[...]
    for h in range(HG_HEADS):
        hs = slice(h * D, (h + 1) * D)
        lb = lb_all[:, hs]
        for ch in range(HG_TB // C):
            rs = slice(ch * C, (ch + 1) * C)
            q = q_ref[0, rs, hs].astype(F32)
            z = f_ref[0, rs, hs].astype(F32)
            v = i_ref[0, rs, hs]
            gg = g_ref[0, rs, hs].astype(F32)
            f = lb + (1.0 - lb) * jax.nn.sigmoid(z)
            lg = jnp.log(f)
            kk = 1.0 - f
            g1 = lg.astype(BF16)
            r1 = lg - g1.astype(F32)
            g2 = r1.astype(BF16)
            g3 = (r1 - g2.astype(F32)).astype(BF16)
            ez = jnp.exp(jnp.dot(mm, jnp.concatenate([g1, g2, g3], axis=0),
                                 preferred_element_type=F32))
            attn = bm_ref[HG_LEVELS] * lax.dot_general(
                q.astype(BF16), kk.astype(BF16), NT_DIMS, preferred_element_type=F32)
            for lv in range(HG_LEVELS):
                zl = ez[lv * C:(lv + 1) * C]
                rm = rm_ref[lv]
                ql = (q * zl * rm).astype(BF16)
                kl = (kk * zl * (1.0 - rm)).astype(BF16)
                attn = attn + bm_ref[lv] * lax.dot_general(ql, kl, NT_DIMS, preferred_element_type=F32)
            zc = ez[HG_LEVELS * C:(HG_LEVELS + 1) * C]
            zs = ez[(HG_LEVELS + 1) * C:(HG_LEVELS + 2) * C]
            st = st_sc[h]
            o = jnp.dot(attn.astype(BF16), v, preferred_element_type=F32)
            o = o + lax.dot_general((q * zc).astype(BF16), st.astype(BF16), NT_DIMS,
                                    preferred_element_type=F32)
            st_sc[h] = zc[C - 1:C, :] * st + lax.dot_general(
                v, (kk * zs).astype(BF16), TN_DIMS, preferred_element_type=F32)
            y = o * lax.rsqrt(jnp.mean(o * o, axis=-1, keepdims=True) + RMS_EPS) * gn
            o_ref[0, rs, hs] = (y * (gg * jax.nn.sigmoid(gg))).astype(o_ref.dtype)


def _hgrn(hq, hf, hi, hg, lb_logits, hg_norm, B, S):
    W = HG_HEADS * HG_DIM
    mm3, rm, bm = _hgrn_consts()
    r3 = lambda a: a.reshape(B, S, W)
    tok = pl.BlockSpec((1, HG_TB, W), lambda b, j: (b, j, 0))
    const = lambda a: pl.BlockSpec(a.shape, lambda b, j: (0,) * a.ndim)
    gn = hg_norm.reshape(1, HG_DIM)
    return pl.pallas_call(
        _hgrn_kernel,
        grid=(B, S // HG_TB),
        in_specs=[tok, tok, tok, tok, const(lb_logits), const(gn), const(mm3), const(rm), const(bm)],
        out_specs=tok,
        out_shape=jax.ShapeDtypeStruct((B, S, W), BF16),
        scratch_shapes=[pltpu.VMEM((HG_HEADS, HG_DIM, HG_DIM), F32)],
        compiler_params=_cparams(("parallel", "arbitrary")),
        name="hgrn2",
    )(r3(hq), r3(hf), r3(hi), r3(hg), lb_logits, gn, mm3, rm, bm)


MG_TM = 256


def _first_lane(cond, lane_f):
    return jnp.min(jnp.where(cond, lane_f, float(LANES)), axis=1, keepdims=True)


def _merge_kernel(on_ref, oh_ref, mg_ref, x_ref, wn_ref, wh_ref, wo_ref, fn_ref, wr1_ref, wr2_ref, br_ref,
                  x1_ref, h2_ref, rt_ref):
    D = x_ref.shape[1]
    a = jnp.dot(on_ref[...], wn_ref[...], preferred_element_type=F32)
    b = jnp.dot(oh_ref[...], wh_ref[...], preferred_element_type=F32)
    mixed = (jax.nn.sigmoid(mg_ref[:, 0:D].astype(F32)) * a +
             jax.nn.sigmoid(mg_ref[:, D:2 * D].astype(F32)) * b)
    x1 = x_ref[...] + jnp.dot(mixed.astype(BF16), wo_ref[...], preferred_element_type=F32)
    x1_ref[...] = x1
    h2 = x1 * lax.rsqrt(jnp.mean(x1 * x1, axis=-1, keepdims=True) + RMS_EPS) * fn_ref[...]
    h2_ref[...] = h2
    h_hi = h2.astype(BF16)
    h_lo = (h2 - h_hi.astype(F32)).astype(BF16)
    logits = (jnp.dot(h_lo, wr1_ref[...], preferred_element_type=F32) +
              jnp.dot(h_hi, wr2_ref[...], preferred_element_type=F32) +
              jnp.dot(h_hi, wr1_ref[...], preferred_element_type=F32)) + br_ref[...]
    lane = lax.broadcasted_iota(jnp.int32, logits.shape, 1)
    ninf = -jnp.inf
    gl = jnp.where(lane < N_GROUPS, logits, ninf)
    gmax = jnp.max(gl, axis=1, keepdims=True)
    lane_f = lane.astype(F32)
    gsel = _first_lane(gl == gmax, lane_f)
    gprob = 1.0 / jnp.sum(jnp.exp(gl - gmax), axis=1, keepdims=True)
    lo = N_GROUPS + gsel * EXPERTS_PER_GROUP
    el = jnp.where((lane_f >= lo) & (lane_f < lo + EXPERTS_PER_GROUP), logits, ninf)
    v1 = jnp.max(el, axis=1, keepdims=True)
    i1 = _first_lane(el == v1, lane_f)
    el2 = jnp.where(lane_f == i1, ninf, el)
    v2 = jnp.max(el2, axis=1, keepdims=True)
    i2 = _first_lane(el2 == v2, lane_f)
    e2 = jnp.exp(v2 - v1)
    w1 = gprob / (1.0 + e2)
    w2 = gprob * e2 / (1.0 + e2)
    rt = jnp.where(lane == 0, i1 - N_GROUPS,
                   jnp.where(lane == 1, i2 - N_GROUPS,
                             jnp.where(lane == 2, w1, jnp.where(lane == 3, w2, 0.0))))
    rt_ref[...] = rt


def _merge(o_nsa, o_hg, mg, x2, w_br_nsa, w_br_hg, w_out, ffn_norm, w_grp, b_grp, w_rtr, b_rtr):
    T, D = x2.shape
    nr = N_GROUPS + N_EXPERTS
    wr = jnp.pad(jnp.concatenate([w_grp, w_rtr], axis=1), ((0, 0), (0, LANES - nr)))
    wr1 = wr.astype(BF16)
    wr2 = (wr - wr1.astype(F32)).astype(BF16)
    br = jnp.pad(jnp.concatenate([b_grp, b_rtr]), (0, LANES - nr)).reshape(1, LANES)
    tok = lambda w: pl.BlockSpec((MG_TM, w), lambda i: (i, 0))
    const = lambda a: pl.BlockSpec(a.shape, lambda i: (0,) * a.ndim)
    wn, wh, wo = w_br_nsa.astype(BF16), w_br_hg.astype(BF16), w_out.astype(BF16)
    fn = ffn_norm.reshape(1, D)
    return pl.pallas_call(
        _merge_kernel,
        grid=(T // MG_TM,),
        in_specs=[tok(o_nsa.shape[1]), tok(o_hg.shape[1]), tok(2 * D), tok(D),
                  const(wn), const(wh), const(wo), const(fn), const(wr1), const(wr2), const(br)],
        out_specs=[tok(D), tok(D), tok(LANES)],
        out_shape=[jax.ShapeDtypeStruct((T, D), F32), jax.ShapeDtypeStruct((T, D), F32),
                   jax.ShapeDtypeStruct((T, LANES), F32)],
        compiler_params=_cparams(("parallel",)),
        name="merge",
    )(o_nsa, o_hg, mg, x2, wn, wh, wo, fn, wr1, wr2, br)


MOE_TM = 256


def _moe_kernel(te_ref, na_ref, src_ref, dst_ref, h_hbm, wg_ref, wu_ref, wd_ref, y_hbm,
                xbuf, ybuf, gsem, ssem):
    i = pl.program_id(0)
    tm = xbuf.shape[0]

    def gather_copy(r):
        return pltpu.make_async_copy(h_hbm.at[pl.ds(src_ref[0, 0, r], 1)], xbuf.at[pl.ds(r, 1)], gsem)

    def scatter_copy(r):
        return pltpu.make_async_copy(ybuf.at[pl.ds(r, 1)], y_hbm.at[pl.ds(dst_ref[0, 0, r], 1)], ssem)

    def each_row(fn):
        def body(r, c):
            fn(r)
            return c
        lax.fori_loop(0, tm, body, 0, unroll=8)

    @pl.when(i < na_ref[0])
    def _():
        each_row(lambda r: gather_copy(r).start())
        each_row(lambda r: gather_copy(r).wait())
        x = xbuf[...].astype(BF16)
        hg = jnp.dot(x, wg_ref[0], preferred_element_type=F32)
        hu = jnp.dot(x, wu_ref[0], preferred_element_type=F32)
        hid = (hg * jax.nn.sigmoid(hg) * hu).astype(BF16)
        ybuf[...] = jnp.dot(hid, wd_ref[0], preferred_element_type=F32)
        each_row(lambda r: pl.when(dst_ref[0, 0, r] >= 0)(lambda: scatter_copy(r).start()))
        each_row(lambda r: pl.when(dst_ref[0, 0, r] >= 0)(lambda: scatter_copy(r).wait()))


def _moe(h2, route, w_gate, w_up, w_down):
    T, D = h2.shape
    E, _, F = w_gate.shape
    tm = MOE_TM
    n_tiles = 2 * T // tm + E
    n_rows = n_tiles * tm
    ex = jnp.concatenate([route[:, 0], route[:, 1]]).astype(jnp.int32)
    order = jnp.argsort(ex, stable=True).astype(jnp.int32)
    ex_sorted = ex[order]
    counts = jnp.sum(ex[:, None] == jnp.arange(E, dtype=jnp.int32)[None, :], axis=0).astype(jnp.int32)
    padded = ((counts + tm - 1) // tm) * tm
    pstart = jnp.cumsum(padded) - padded
    cstart = jnp.cumsum(counts) - counts
    pos = pstart[ex_sorted] + (jnp.arange(2 * T, dtype=jnp.int32) - cstart[ex_sorted])
    src = jnp.zeros((n_rows,), jnp.int32).at[pos].set(order % T)
    dst = jnp.full((n_rows,), -1, jnp.int32).at[pos].set(order)
    n_active = (jnp.sum(padded) // tm).astype(jnp.int32)
    tile_start = jnp.arange(n_tiles, dtype=jnp.int32) * tm
    pend = jnp.cumsum(padded)
    te = jnp.sum(tile_start[:, None] >= pend[None, :], axis=1).astype(jnp.int32)
    last = jnp.maximum(n_active - 1, 0)
    te = jnp.where(jnp.arange(n_tiles) < n_active, te, te[last])
    te = jnp.minimum(te, E - 1)

    wg, wu, wd = w_gate.astype(BF16), w_up.astype(BF16), w_down.astype(BF16)
    idx_spec = pl.BlockSpec((1, 1, tm), lambda i, te, na: (i, 0, 0), memory_space=pltpu.SMEM)
    grid_spec = pltpu.PrefetchScalarGridSpec(
        num_scalar_prefetch=2,
        grid=(n_tiles,),
        in_specs=[idx_spec, idx_spec,
                  pl.BlockSpec(memory_space=pl.ANY),
                  pl.BlockSpec((1, D, F), lambda i, te, na: (te[i], 0, 0)),
                  pl.BlockSpec((1, D, F), lambda i, te, na: (te[i], 0, 0)),
                  pl.BlockSpec((1, F, D), lambda i, te, na: (te[i], 0, 0))],
        out_specs=pl.BlockSpec(memory_space=pl.ANY),
        scratch_shapes=[pltpu.VMEM((tm, D), F32), pltpu.VMEM((tm, D), F32),
                        pltpu.SemaphoreType.DMA, pltpu.SemaphoreType.DMA],
    )
    y = pl.pallas_call(
        _moe_kernel,
        grid_spec=grid_spec,
        out_shape=jax.ShapeDtypeStruct((2 * T, D), F32),
        compiler_params=_cparams(("arbitrary",)),
        name="moe",
    )(te, n_active.reshape(1), src.reshape(n_tiles, 1, tm), dst.reshape(n_tiles, 1, tm), h2, wg, wu, wd)
    return y


FIN_TM = 256


def _final_kernel(x1_ref, y1_ref, y2_ref, rt_ref, g_ref, o_ref):
    rt = rt_ref[...]
    x = x1_ref[...] + rt[:, 2:3] * y1_ref[...] + rt[:, 3:4] * y2_ref[...]
    o_ref[...] = x * lax.rsqrt(jnp.mean(x * x, axis=-1, keepdims=True) + RMS_EPS) * g_ref[...]


def _final(x1, y, route, final_norm):
    T, D = x1.shape
    nb = T // FIN_TM
    return pl.pallas_call(
        _final_kernel,
        grid=(nb,),
        in_specs=[pl.BlockSpec((FIN_TM, D), lambda i: (i, 0)),
                  pl.BlockSpec((FIN_TM, D), lambda i: (i, 0)),
                  pl.BlockSpec((FIN_TM, D), lambda i: (i + nb, 0)),
                  pl.BlockSpec((FIN_TM, LANES), lambda i: (i, 0)),
                  pl.BlockSpec((1, D), lambda i: (0, 0))],
        out_specs=pl.BlockSpec((FIN_TM, D), lambda i: (i, 0)),
        out_shape=jax.ShapeDtypeStruct((T, D), F32),
        compiler_params=_cparams(("parallel",)),
        name="final",
    )(x1, y, y, route, final_norm.reshape(1, D))


def kernel(x, attn_norm, w_in, w_cmp_k, w_cmp_v, cmp_pos, hg_lb_logits, hg_norm, w_br_nsa, w_br_hg, w_out,
           ffn_norm, w_grp, b_grp, w_rtr, b_rtr, w_gate, w_up, w_down, final_norm):
    B, S, D = x.shape
    assert attn_norm.shape[0] == 1, "single-layer block"
    x2 = x.reshape(B * S, D)
    q, kv, gate, hq, hf, hi, hg, mg = _in_proj(x2, attn_norm[0], w_in[0])
    qt, ksa, vst, kw, vwt, gtt = _nsa_prep(q, kv, gate, B, S)
    kc, vct = _compress(kv, w_cmp_k[0], w_cmp_v[0], cmp_pos[0], B, S)
    o_nsa = _nsa_attn(qt, kc, vct, ksa, vst, kw, vwt, gtt, B, S).reshape(B * S, -1)
    o_hg = _hgrn(hq, hf, hi, hg, hg_lb_logits, hg_norm[0], B, S).reshape(B * S, -1)
    x1, h2, route = _merge(o_nsa, o_hg, mg, x2, w_br_nsa[0], w_br_hg[0], w_out[0], ffn_norm[0],
                           w_grp[0], b_grp[0], w_rtr[0], b_rtr[0])
    y = _moe(h2, route, w_gate[0], w_up[0], w_down[0])
    out = _final(x1, y, route, final_norm)
    return out.reshape(B, S, D)
```

```python
import functools

import numpy as np
import jax
import jax.numpy as jnp
from jax import lax
from jax.experimental import pallas as pl
from jax.experimental.pallas import tpu as pltpu

F32 = jnp.float32
BF16 = jnp.bfloat16

NSA_HEADS = 8
NSA_KV_GROUPS = 2
NSA_HEAD_DIM = 64
NSA_R = NSA_HEADS // NSA_KV_GROUPS
CMP_LEN = 32
CMP_STRIDE = 16
SLC_LEN = 64
SLC_SHIFT = 6
SLC_TOPN = 16
WINDOW = 512
Q_BLOCK = 128
ROPE_THETA = 500000.0
ROPE_DIM = NSA_HEAD_DIM // 4
HG_HEADS = 4
HG_DIM = 128
HG_CHUNK = 64
N_GROUPS = 4
EXPERTS_PER_GROUP = 8
N_EXPERTS = N_GROUPS * EXPERTS_PER_GROUP
RMS_EPS = 1e-6
NEG = -1e30
FORCE_SCORE = 1e4

LANES = 128
VMEM_LIMIT = 56 * 1024 * 1024

NT_DIMS = (((1,), (1,)), ((), ()))
TN_DIMS = (((0,), (0,)), ((), ()))


def _cparams(sem):
    return pltpu.CompilerParams(dimension_semantics=sem, vmem_limit_bytes=VMEM_LIMIT)


IN_TM = 256
IN_SEGS = (("q", 512, 512), ("kc", 128, 128), ("vc", 128, 128), ("kv", 512, 512), ("gate", 24, 128),
           ("hq", 512, 512), ("hf", 512, 512), ("hi", 512, 512), ("hg", 512, 512), ("merge", 2048, 2048))
IN_F32 = ("kc", "vc", "gate")


def _in_proj_kernel(x_ref, g_ref, w_ref, *out_refs):
    x = x_ref[...]
    h = x * lax.rsqrt(jnp.mean(x * x, axis=-1, keepdims=True) + RMS_EPS) * g_ref[...]
    h = h.astype(BF16)
    off = 0
    for (_, _, wpad), o_ref in zip(IN_SEGS, out_refs):
        y = jnp.dot(h, w_ref[:, off:off + wpad], preferred_element_type=F32)
        o_ref[...] = y.astype(o_ref.dtype)
        off += wpad


def _in_proj(x2, attn_norm, w_in):
    T, D = x2.shape
    pieces, off = [], 0
    for _, w, wpad in IN_SEGS:
        p = w_in[:, off:off + w]
        if wpad != w:
            p = jnp.pad(p, ((0, 0), (0, wpad - w)))
        pieces.append(p)
        off += w
    wcat = jnp.concatenate(pieces, axis=1).astype(BF16)
    NP = wcat.shape[1]
    out_shapes = [jax.ShapeDtypeStruct((T, wpad), F32 if name in IN_F32 else BF16)
                  for name, _, wpad in IN_SEGS]
    out_specs = [pl.BlockSpec((IN_TM, wpad), lambda i: (i, 0)) for _, _, wpad in IN_SEGS]
    return pl.pallas_call(
        _in_proj_kernel,
        grid=(T // IN_TM,),
        in_specs=[pl.BlockSpec((IN_TM, D), lambda i: (i, 0)),
                  pl.BlockSpec((1, D), lambda i: (0, 0)),
                  pl.BlockSpec((D, NP), lambda i: (0, 0))],
        out_specs=out_specs,
        out_shape=out_shapes,
        compiler_params=_cparams(("parallel",)),
        name="in_proj",
    )(x2, attn_norm.reshape(1, D), wcat)


def _rope_tables(pos):
    n = pos.shape[0]
    half = ROPE_DIM // 2
    inv_freq = ROPE_THETA ** (-jnp.arange(half, dtype=F32) / half)
    ang = pos.astype(F32)[:, None] * inv_freq[None, :]
    cos, sin = jnp.cos(ang), jnp.sin(ang)
    rest = NSA_HEAD_DIM - ROPE_DIM
    z8 = jnp.zeros((n, half), F32)
    c64 = jnp.concatenate([cos, cos, jnp.ones((n, rest), F32)], axis=1)
    s1 = jnp.concatenate([-sin, z8, jnp.zeros((n, rest), F32)], axis=1)
    s2 = jnp.concatenate([z8, sin, jnp.zeros((n, rest), F32)], axis=1)
    tile = lambda a: jnp.concatenate([a, a], axis=1)
    return tile(c64), tile(s1), tile(s2)


def _rope(x, c, s1, s2):
    half = ROPE_DIM // 2
    return x * c + pltpu.roll(x, LANES - half, 1) * s1 + pltpu.roll(x, half, 1) * s2


PREP_TS = 512


GATE_ROWS = 16
LOG2E = 1.4426950408889634


def _prep_kernel(q_ref, kv_ref, gate_ref, c_ref, s1_ref, s2_ref, nege_ref,
                 qt_ref, ksa_ref, vst_ref, kw_ref, vwt_ref, gtt_ref):
    c, s1, s2 = c_ref[...], s1_ref[...], s2_ref[...]
    lane = lax.broadcasted_iota(jnp.int32, c.shape, 1)
    lo = lane < NSA_HEAD_DIM
    dk, QB = NSA_HEAD_DIM, Q_BLOCK

    def split(x):
        return (jnp.where(lo, x, 0.0), jnp.where(lo, pltpu.roll(x, dk, 1), 0.0))

    scale = (dk ** -0.5) * LOG2E
    for blk in range(NSA_HEADS // 2):
        x = q_ref[:, blk * LANES:(blk + 1) * LANES].astype(F32)
        for hh, part in enumerate(split(_rope(x, c, s1, s2) * scale)):
            h = 2 * blk + hh
            g, r = h // NSA_R, h % NSA_R
            pt = part.T.astype(BF16)
            for i in range(PREP_TS // QB):
                qt_ref[0, g, i, :, r * QB:(r + 1) * QB] = pt[:, i * QB:(i + 1) * QB]
    nege = nege_ref[...]
    for g, part in enumerate(split(_rope(kv_ref[:, 0:LANES].astype(F32), c, s1, s2))):
        ksa_ref[0, g] = jnp.concatenate([part.astype(BF16), nege], axis=1)
    for g, part in enumerate(split(_rope(kv_ref[:, 2 * LANES:3 * LANES].astype(F32), c, s1, s2))):
        kw_ref[0, g] = part.astype(BF16)
    for blk, ref in ((1, vst_ref), (3, vwt_ref)):
        for g, part in enumerate(split(kv_ref[:, blk * LANES:(blk + 1) * LANES].astype(F32))):
            ref[0, g] = part.T[0:dk, :].astype(BF16)
    sg = jax.nn.sigmoid(gate_ref[...])
    gtt_ref[0, 0] = sg.T[0:GATE_ROWS, :]
    gtt_ref[0, 1] = pltpu.roll(sg, LANES - 3 * NSA_R, 1).T[0:GATE_ROWS, :]


def _nsa_prep(q, kv, gate, B, S):
    G, dk, QB = NSA_KV_GROUPS, NSA_HEAD_DIM, Q_BLOCK
    c, s1, s2 = _rope_tables(jnp.arange(S))
    key = np.arange(S)
    nege = jnp.asarray(np.where(key[:, None] // SLC_LEN == np.arange(LANES)[None, :], NEG, 0.0), BF16)
    nj = S // PREP_TS
    tok = lambda w: pl.BlockSpec((PREP_TS, w), lambda b, j: (b * nj + j, 0))
    tab = pl.BlockSpec((PREP_TS, LANES), lambda b, j: (j, 0))
    rows_out = lambda w, dt: (jax.ShapeDtypeStruct((B, G, S, w), dt),
                              pl.BlockSpec((1, G, PREP_TS, w), lambda b, j: (b, 0, j, 0)))
    cols_out = lambda n, dt: (jax.ShapeDtypeStruct((B, G, n, S), dt),
                              pl.BlockSpec((1, G, n, PREP_TS), lambda b, j: (b, 0, 0, j)))
    qt_out = (jax.ShapeDtypeStruct((B, G, S // QB, LANES, NSA_R * QB), BF16),
              pl.BlockSpec((1, G, PREP_TS // QB, LANES, NSA_R * QB), lambda b, j: (b, 0, j, 0, 0)))
    outs = [qt_out, rows_out(2 * LANES, BF16), cols_out(dk, BF16), rows_out(LANES, BF16),
            cols_out(dk, BF16), cols_out(GATE_ROWS, F32)]
    return pl.pallas_call(
        _prep_kernel,
        grid=(B, nj),
        in_specs=[tok(512), tok(512), tok(LANES), tab, tab, tab, tab],
        out_specs=[o[1] for o in outs],
        out_shape=[o[0] for o in outs],
        compiler_params=_cparams(("parallel", "parallel")),
        name="nsa_prep",
    )(q, kv, gate, c, s1, s2, nege)


def _compress_kernel(tk_ref, tv_ref, wk_ref, wv_ref, pos_ref, c_ref, s1_ref, s2_ref, kc_ref, vct_ref):
    nc = c_ref.shape[0]
    dk = NSA_HEAD_DIM
    lane = lax.broadcasted_iota(jnp.int32, (nc, LANES), 1)
    lo = lane < dk

    def comp(t_ref, w_ref):
        a = jnp.zeros((nc, LANES), F32)
        b = jnp.zeros((nc, LANES), F32)
        c0 = jnp.zeros((SUBLANES, LANES), F32)
        for j in range(CMP_STRIDE):
            u = t_ref[pl.ds(j, nc, stride=CMP_STRIDE), :].astype(BF16)
            a = a + jnp.dot(u, w_ref[j], preferred_element_type=F32)
            b = b + jnp.dot(u, w_ref[CMP_STRIDE + j], preferred_element_type=F32)
        for l in range(CMP_LEN):
            c0 = c0 + jnp.dot(pos_ref[l], w_ref[l], preferred_element_type=F32)
        return a + pltpu.roll(b, nc - 1, 0) + c0[0:1, :]

    def split(x):
        return (jnp.where(lo, x, 0.0), jnp.where(lo, pltpu.roll(x, dk, 1), 0.0))

    for g, part in enumerate(split(_rope(comp(tk_ref, wk_ref), c_ref[...], s1_ref[...], s2_ref[...]))):
        kc_ref[0, g] = part.astype(BF16)
    for g, part in enumerate(split(comp(tv_ref, wv_ref))):
        vct_ref[0, g] = part.T[0:dk, :].astype(BF16)


def _compress(kcin, vcin, w_cmp_k, w_cmp_v, cmp_pos, B, S):
    G, dk = NSA_KV_GROUPS, NSA_HEAD_DIM
    nc = S // CMP_STRIDE

    def block_diag(w):
        w3 = w.reshape(CMP_LEN, dk, dk)
        z = jnp.zeros_like(w3)
        return jnp.concatenate([jnp.concatenate([w3, z], axis=2),
                                jnp.concatenate([z, w3], axis=2)], axis=1).astype(BF16)

    pos = jnp.concatenate([cmp_pos, cmp_pos], axis=1)
    pos = jnp.broadcast_to(pos[:, None, :], (CMP_LEN, SUBLANES, G * dk)).astype(BF16)
    c, s1, s2 = _rope_tables(jnp.arange(nc) * CMP_STRIDE)
    tspec = pl.BlockSpec((S, LANES), lambda b: (b, 0))
    wspec = pl.BlockSpec((CMP_LEN, LANES, LANES), lambda b: (0, 0, 0))
    tab = pl.BlockSpec((nc, LANES), lambda b: (0, 0))
    return pl.pallas_call(
        _compress_kernel,
        grid=(B,),
        in_specs=[tspec, tspec, wspec, wspec,
                  pl.BlockSpec((CMP_LEN, SUBLANES, LANES), lambda b: (0, 0, 0)), tab, tab, tab],
        out_specs=[pl.BlockSpec((1, G, nc, LANES), lambda b: (b, 0, 0, 0)),
                   pl.BlockSpec((1, G, dk, nc), lambda b: (b, 0, 0, 0))],
        out_shape=[jax.ShapeDtypeStruct((B, G, nc, LANES), BF16),
                   jax.ShapeDtypeStruct((B, G, dk, nc), BF16)],
        compiler_params=_cparams(("parallel",)),
        name="compress",
    )(kcin, vcin, block_diag(w_cmp_k), block_diag(w_cmp_v), pos, c, s1, s2)


SEL_KC = 256
SEL_KC_SHIFT = 8
WIN_KEYS = WINDOW + Q_BLOCK


def _col_reduce(x, op, fin):
    n = x.shape[0]
    while (n // 2) % 8 == 0 and n > 8:
        n //= 2
        x = op(x[:n], x[n:])
    return fin(x, axis=0, keepdims=True)


def _col_max(x):
    return _col_reduce(x, jnp.maximum, jnp.max)


def _col_min(x):
    return _col_reduce(x, jnp.minimum, jnp.min)


def _col_sum(x):
    return _col_reduce(x, jnp.add, jnp.sum)


def _softmax_cols(s):
    e = jnp.exp2(s - _col_max(s))
    return e, _col_sum(e)


def _nsa_kernel(qt_ref, kc_ref, vct_ref, ksa_ref, vst_ref, kw_ref, vwt_ref, gtt_ref, ovt_ref,
                o_ref, rhs_sc, m0_sc, m1_sc, l0_sc, l1_sc, a0_sc, a1_sc, sa_sc, sb_sc, *, n_sel):
    m_scs, l_scs, acc_scs = (m0_sc, m1_sc), (l0_sc, l1_sc), (a0_sc, a1_sc)
    s_bufs = (sa_sc, sb_sc)
    R, QB = NSA_R, Q_BLOCK
    cols = R * QB
    i = pl.program_id(2)
    t0 = i * QB
    qt = qt_ref[0, 0, 0]

    def per_head(x):
        return jnp.concatenate([x] * (cols // x.shape[1]), axis=1)

    def rc(n):
        return (lax.broadcasted_iota(jnp.int32, (n, QB), 0),
                t0 + lax.broadcasted_iota(jnp.int32, (n, QB), 1))

    nc = kc_ref.shape[2]
    sc = jnp.dot(kc_ref[0, 0], qt, preferred_element_type=F32)
    n_idx, tq_c = rc(nc)
    cvis = (n_idx * CMP_STRIDE + (CMP_LEN - 1)) <= tq_c
    ec, lc = _softmax_cols(sc + per_head(jnp.where(cvis, 0.0, NEG)))
    pc = ec * per_head(jnp.where(cvis, 1.0, 0.0)) * (1.0 / lc)
    o_cmp = jnp.dot(vct_ref[0, 0], pc.astype(BF16), preferred_element_type=F32)

    start = pl.multiple_of(jnp.maximum(t0 - WINDOW, 0), QB)
    sw = jnp.dot(kw_ref[0, 0, pl.ds(start, WIN_KEYS), :], qt, preferred_element_type=F32)
    k_idx, tq_w = rc(WIN_KEYS)
    rel = tq_w - (start + k_idx)
    pw, lw = _softmax_cols(sw + per_head(jnp.where((rel >= 0) & (rel < WINDOW), 0.0, NEG)))
    o_win = jnp.dot(vwt_ref[0, 0, :, pl.ds(start, WIN_KEYS)], pw.astype(BF16),
                    preferred_element_type=F32) * (1.0 / lw)

    psum = pc[:, 0:QB] + pc[:, QB:2 * QB] + pc[:, 2 * QB:3 * QB] + pc[:, 3 * QB:4 * QB]
    p_hi = psum.astype(BF16)
    p_lo = (psum - p_hi.astype(F32)).astype(BF16)
    ovt = ovt_ref[...]
    imp = (jnp.dot(ovt, p_hi, preferred_element_type=F32) +
           jnp.dot(ovt, p_lo, preferred_element_type=F32))
    j = lax.broadcasted_iota(jnp.int32, (LANES, QB), 0)
    tq = t0 + lax.broadcasted_iota(jnp.int32, (LANES, QB), 1)
    force = (j == jnp.right_shift(tq, SLC_SHIFT)) | (j == 0)
    valid = (j * SLC_LEN) <= tq
    score = jnp.where(force, FORCE_SCORE, jnp.where(valid, imp, -1.0))
    work = jnp.where(j < n_sel, score, -jnp.inf)
    notsel = jnp.ones((LANES, QB), F32)
    jf = j.astype(F32)
    for _ in range(min(SLC_TOPN, n_sel)):
        mx = _col_max(work)
        idx = _col_min(jnp.where(work == mx, jf, float(LANES)))
        hit = jf == idx
        notsel = jnp.where(hit, 0.0, notsel)
        work = jnp.where(hit, -jnp.inf, work)

    rhs_sc[0:LANES, :] = qt
    rhs_sc[LANES:2 * LANES, :] = jnp.concatenate([notsel.astype(BF16)] * R, axis=1)
    halves = tuple(zip(m_scs, l_scs, acc_scs))
    hw = cols // len(halves)
    for m_sc, l_sc, acc_sc in halves:
        m_sc[...] = jnp.full(m_sc.shape, NEG, F32)
        l_sc[...] = jnp.zeros(l_sc.shape, F32)
        acc_sc[...] = jnp.zeros(acc_sc.shape, F32)

    def sel_scores(chunk_ids):
        kas = [ksa_ref[0, 0, pl.ds(pl.multiple_of(c * SEL_KC, SEL_KC), SEL_KC), :] for c in chunk_ids]
        return [[jnp.dot(ka, rhs_sc[:, hp * hw:(hp + 1) * hw], preferred_element_type=F32)
                 for ka in kas] for hp in range(len(halves))]

    def sel_update(chunk_ids, scores, diagonal):
        k0s = [pl.multiple_of(c * SEL_KC, SEL_KC) for c in chunk_ids]
        vts = [vst_ref[0, 0, :, pl.ds(k0, SEL_KC)] for k0 in k0s]
        for hp, (m_sc, l_sc, acc_sc) in enumerate(halves):
            ss = scores[hp]
            if diagonal:
                k_idx, tq_s = rc(SEL_KC)
                ss = [s + jnp.concatenate([jnp.where(k0 + k_idx <= tq_s, 0.0, NEG)] * (hw // QB), axis=1)
                      for k0, s in zip(k0s, ss)]
            m_old = m_sc[...]
            m_new = functools.reduce(jnp.maximum, [_col_max(s) for s in ss], m_old)
            alpha = jnp.exp2(m_old - m_new)
            ps = [jnp.exp2(s - m_new) for s in ss]
            l_sc[...] = alpha * l_sc[...] + functools.reduce(jnp.add, [_col_sum(p) for p in ps])
            pv = [jnp.dot(vt, p.astype(BF16), preferred_element_type=F32) for vt, p in zip(vts, ps)]
            acc_sc[...] = alpha * acc_sc[...] + functools.reduce(jnp.add, pv)
            m_sc[...] = m_new

    n_units = jnp.right_shift(t0, SEL_KC_SHIFT + 1)
    last_unit = jnp.maximum(n_units - 1, 0)
    nh = len(halves)

    def unit(u):
        return [2 * u, 2 * u + 1]

    def put_scores(buf, scores):
        for hp in range(nh):
            for c in range(2):
                buf[hp * 2 + c] = scores[hp][c]

    def get_scores(buf):
        return [[buf[hp * 2 + c] for c in range(2)] for hp in range(nh)]

    put_scores(s_bufs[0], sel_scores(unit(0)))

    def sel_step(u, carry):
        for parity in range(2):
            @pl.when((u & 1) == parity)
            def _():
                put_scores(s_bufs[1 - parity], sel_scores(unit(jnp.minimum(u + 1, last_unit))))
                sel_update(unit(u), get_scores(s_bufs[parity]), False)
        return carry

    lax.fori_loop(0, n_units, sel_step, 0)
    sel_update(unit(n_units), sel_scores(unit(n_units)), True)
    o_slc = jnp.concatenate([a[...] * (1.0 / l[...]) for _, l, a in halves], axis=1)

    gtt = gtt_ref[0, 0]
    outs = []
    for r in range(R):
        cs = slice(r * QB, (r + 1) * QB)
        outs.append(gtt[3 * r:3 * r + 1, :] * o_cmp[:, cs] + gtt[3 * r + 1:3 * r + 2, :] * o_slc[:, cs] +
                    gtt[3 * r + 2:3 * r + 3, :] * o_win[:, cs])
    o_ref[0] = jnp.concatenate(outs, axis=0).T.astype(o_ref.dtype)


def _nsa_attn(qt, kc, vct, ksa, vst, kw, vwt, gtt, B, S):
    G, R, QB, dk = NSA_KV_GROUPS, NSA_R, Q_BLOCK, NSA_HEAD_DIM
    nc = S // CMP_STRIDE
    n_sel = S // SLC_LEN
    assert n_sel <= LANES and n_sel >= SLC_TOPN and S % (2 * SEL_KC) == 0 and S >= WIN_KEYS
    assert SEL_KC == 1 << SEL_KC_SHIFT and SEL_KC % QB == 0
    cs = np.arange(nc)[None, :] * CMP_STRIDE
    ss = np.arange(LANES)[:, None] * SLC_LEN
    ovm = (cs < ss + SLC_LEN) & (cs + CMP_LEN - 1 >= ss) & (np.arange(LANES)[:, None] < n_sel) \
        & (np.arange(nc)[None, :] < nc - 1)
    ovt = jnp.asarray(ovm, BF16)
    rows_in = lambda n, w: pl.BlockSpec((1, 1, n, w), lambda b, g, i: (b, g, 0, 0))
    cols = R * QB
    return pl.pallas_call(
        functools.partial(_nsa_kernel, n_sel=n_sel),
        grid=(B, G, S // QB),
        in_specs=[pl.BlockSpec((1, 1, 1, LANES, cols), lambda b, g, i: (b, g, i, 0, 0)),
                  rows_in(nc, LANES), rows_in(dk, nc), rows_in(S, 2 * LANES), rows_in(dk, S),
                  rows_in(S, LANES), rows_in(dk, S),
                  pl.BlockSpec((1, 1, GATE_ROWS, QB), lambda b, g, i: (b, g, 0, i)),
                  pl.BlockSpec((LANES, nc), lambda b, g, i: (0, 0))],
        out_specs=pl.BlockSpec((1, QB, R * dk), lambda b, g, i: (b, i, g)),
        out_shape=jax.ShapeDtypeStruct((B, S, NSA_HEADS * dk), BF16),
        scratch_shapes=[pltpu.VMEM((2 * LANES, cols), BF16)] +
                       [pltpu.VMEM((1, cols // 2), F32)] * 4 + [pltpu.VMEM((dk, cols // 2), F32)] * 2 +
                       [pltpu.VMEM((4, SEL_KC, cols // 2), F32)] * 2,
        compiler_params=_cparams(("parallel", "parallel", "arbitrary")),
        name="nsa_attn",
    )(qt, kc, vct, ksa, vst, kw, vwt, gtt, ovt)


HG_TB = 256
HG_LEVELS = 6


def _hgrn_consts():
    C = HG_CHUNK
    M = np.zeros((HG_LEVELS + 2, C, C), np.float32)
    RM = np.zeros((HG_LEVELS, C, 1), np.float32)
    BM = np.zeros((HG_LEVELS + 1, C, C), np.float32)
    for lv in range(HG_LEVELS):
        m = 1 << lv
        for t in range(C):
            bs = (t // (2 * m)) * (2 * m)
            mid = bs + m - 1
            if t - bs >= m:
                RM[lv, t, 0] = 1.0
                M[lv, t, mid + 1:t + 1] = 1.0
            else:
                M[lv, t, t + 1:mid + 1] = 1.0
            BM[lv, t, bs:bs + 2 * m] = 1.0
    BM[HG_LEVELS] = np.eye(C, dtype=np.float32)
    M[HG_LEVELS] = np.tril(np.ones((C, C), np.float32))
    M[HG_LEVELS + 1] = np.triu(np.ones((C, C), np.float32), 1)
    mm = M.reshape(-1, C)
    mm3 = np.concatenate([mm, mm, mm], axis=1)
    rm = np.broadcast_to(RM, (HG_LEVELS, C, HG_DIM)).copy()
    return jnp.asarray(mm3, BF16), jnp.asarray(rm, F32), jnp.asarray(BM, F32)


def _hgrn_kernel(q_ref, f_ref, i_ref, g_ref, lbl_ref, gn_ref, mm_ref, rm_ref, bm_ref, o_ref, st_sc):
    C, D = HG_CHUNK, HG_DIM

    @pl.when(pl.program_id(1) == 0)
    def _():
        st_sc[...] = jnp.zeros(st_sc.shape, F32)

    lbl = lbl_ref[...]
    mxl = jnp.max(lbl, axis=0, keepdims=True)
    el = jnp.exp(lbl - mxl)
    lb_all = el[0:1, :] / jnp.sum(el, axis=0, keepdims=True)
    mm = mm_ref[...]
    gn = gn_ref[...]
    for h in range(HG_HEADS):
        hs = slice(h * D, (h + 1) * D)
        lb = lb_all[:, hs]
        for ch in range(HG_TB // C):
            rs = slice(ch * C, (ch + 1) * C)
            q = q_ref[0, rs, hs].astype(F32)
            z = f_ref[0, rs, hs].astype(F32)
            v = i_ref[0, rs, hs]
            gg = g_ref[0, rs, hs].astype(F32)
            f = lb + (1.0 - lb) * jax.nn.sigmoid(z)
            lg = jnp.log(f)
            kk = 1.0 - f
            g1 = lg.astype(BF16)
            r1 = lg - g1.astype(F32)
            g2 = r1.astype(BF16)
            g3 = (r1 - g2.astype(F32)).astype(BF16)
            ez = jnp.exp(jnp.dot(mm, jnp.concatenate([g1, g2, g3], axis=0),
                                 preferred_element_type=F32))
            attn = bm_ref[HG_LEVELS] * lax.dot_general(
                q.astype(BF16), kk.astype(BF16), NT_DIMS, preferred_element_type=F32)
            for lv in range(HG_LEVELS):
                zl = ez[lv * C:(lv + 1) * C]
                rm = rm_ref[lv]
                ql = (q * zl * rm).astype(BF16)
                kl = (kk * zl * (1.0 - rm)).astype(BF16)
                attn = attn + bm_ref[lv] * lax.dot_general(ql, kl, NT_DIMS, preferred_element_type=F32)
            zc = ez[HG_LEVELS * C:(HG_LEVELS + 1) * C]
            zs = ez[(HG_LEVELS + 1) * C:(HG_LEVELS + 2) * C]
            st = st_sc[h]
            o = jnp.dot(attn.astype(BF16), v, preferred_element_type=F32)
            o = o + lax.dot_general((q * zc).astype(BF16), st.astype(BF16), NT_DIMS,
                                    preferred_element_type=F32)
            st_sc[h] = zc[C - 1:C, :] * st + lax.dot_general(
                v, (kk * zs).astype(BF16), TN_DIMS, preferred_element_type=F32)
            y = o * lax.rsqrt(jnp.mean(o * o, axis=-1, keepdims=True) + RMS_EPS) * gn
            o_ref[0, rs, hs] = (y * (gg * jax.nn.sigmoid(gg))).astype(o_ref.dtype)


def _hgrn(hq, hf, hi, hg, lb_logits, hg_norm, B, S):
    W = HG_HEADS * HG_DIM
    mm3, rm, bm = _hgrn_consts()
    r3 = lambda a: a.reshape(B, S, W)
    tok = pl.BlockSpec((1, HG_TB, W), lambda b, j: (b, j, 0))
    const = lambda a: pl.BlockSpec(a.shape, lambda b, j: (0,) * a.ndim)
    gn = hg_norm.reshape(1, HG_DIM)
    return pl.pallas_call(
        _hgrn_kernel,
        grid=(B, S // HG_TB),
        in_specs=[tok, tok, tok, tok, const(lb_logits), const(gn), const(mm3), const(rm), const(bm)],
        out_specs=tok,
        out_shape=jax.ShapeDtypeStruct((B, S, W), BF16),
        scratch_shapes=[pltpu.VMEM((HG_HEADS, HG_DIM, HG_DIM), F32)],
        compiler_params=_cparams(("parallel", "arbitrary")),
        name="hgrn2",
    )(r3(hq), r3(hf), r3(hi), r3(hg), lb_logits, gn, mm3, rm, bm)


MG_TM = 256


SUBLANES = 8


def _to_row_tiles(ref, x, lead=()):
    for s in range(SUBLANES):
        ref[lead + (slice(None), s, slice(None))] = x[:, s * LANES:(s + 1) * LANES]


def _from_row_tiles(ref, lead=()):
    return jnp.concatenate([ref[lead + (slice(None), s, slice(None))] for s in range(SUBLANES)], axis=1)


def _first_lane(cond, lane_f):
    return jnp.min(jnp.where(cond, lane_f, float(LANES)), axis=1, keepdims=True)


def _merge_kernel(on_ref, oh_ref, mg_ref, x_ref, wn_ref, wh_ref, wo_ref, fn_ref, wr1_ref, wr2_ref, br_ref,
                  x1_ref, h2_ref, rt_ref):
    D = x_ref.shape[1]
    a = jnp.dot(on_ref[...], wn_ref[...], preferred_element_type=F32)
    b = jnp.dot(oh_ref[...], wh_ref[...], preferred_element_type=F32)
    mixed = (jax.nn.sigmoid(mg_ref[:, 0:D].astype(F32)) * a +
             jax.nn.sigmoid(mg_ref[:, D:2 * D].astype(F32)) * b)
    x1 = x_ref[...] + jnp.dot(mixed.astype(BF16), wo_ref[...], preferred_element_type=F32)
    x1_ref[...] = x1
    h2 = x1 * lax.rsqrt(jnp.mean(x1 * x1, axis=-1, keepdims=True) + RMS_EPS) * fn_ref[...]
    _to_row_tiles(h2_ref, h2)
    h_hi = h2.astype(BF16)
    h_lo = (h2 - h_hi.astype(F32)).astype(BF16)
    logits = (jnp.dot(h_lo, wr1_ref[...], preferred_element_type=F32) +
              jnp.dot(h_hi, wr2_ref[...], preferred_element_type=F32) +
              jnp.dot(h_hi, wr1_ref[...], preferred_element_type=F32)) + br_ref[...]
    lane = lax.broadcasted_iota(jnp.int32, logits.shape, 1)
    ninf = -jnp.inf
    gl = jnp.where(lane < N_GROUPS, logits, ninf)
    gmax = jnp.max(gl, axis=1, keepdims=True)
    lane_f = lane.astype(F32)
    gsel = _first_lane(gl == gmax, lane_f)
    gprob = 1.0 / jnp.sum(jnp.exp(gl - gmax), axis=1, keepdims=True)
    lo = N_GROUPS + gsel * EXPERTS_PER_GROUP
    el = jnp.where((lane_f >= lo) & (lane_f < lo + EXPERTS_PER_GROUP), logits, ninf)
    v1 = jnp.max(el, axis=1, keepdims=True)
    i1 = _first_lane(el == v1, lane_f)
    el2 = jnp.where(lane_f == i1, ninf, el)
    v2 = jnp.max(el2, axis=1, keepdims=True)
    i2 = _first_lane(el2 == v2, lane_f)
    e2 = jnp.exp(v2 - v1)
    w1 = gprob / (1.0 + e2)
    w2 = gprob * e2 / (1.0 + e2)
    rt = jnp.where(lane == 0, i1 - N_GROUPS,
                   jnp.where(lane == 1, i2 - N_GROUPS,
                             jnp.where(lane == 2, w1, jnp.where(lane == 3, w2, 0.0))))
    rt_ref[...] = rt


def _merge(o_nsa, o_hg, mg, x2, w_br_nsa, w_br_hg, w_out, ffn_norm, w_grp, b_grp, w_rtr, b_rtr):
    T, D = x2.shape
    nr = N_GROUPS + N_EXPERTS
    wr = jnp.pad(jnp.concatenate([w_grp, w_rtr], axis=1), ((0, 0), (0, LANES - nr)))
    wr1 = wr.astype(BF16)
    wr2 = (wr - wr1.astype(F32)).astype(BF16)
    br = jnp.pad(jnp.concatenate([b_grp, b_rtr]), (0, LANES - nr)).reshape(1, LANES)
    tok = lambda w: pl.BlockSpec((MG_TM, w), lambda i: (i, 0))
    const = lambda a: pl.BlockSpec(a.shape, lambda i: (0,) * a.ndim)
    wn, wh, wo = w_br_nsa.astype(BF16), w_br_hg.astype(BF16), w_out.astype(BF16)
    fn = ffn_norm.reshape(1, D)
    return pl.pallas_call(
        _merge_kernel,
        grid=(T // MG_TM,),
        in_specs=[tok(o_nsa.shape[1]), tok(o_hg.shape[1]), tok(2 * D), tok(D),
                  const(wn), const(wh), const(wo), const(fn), const(wr1), const(wr2), const(br)],
        out_specs=[tok(D), pl.BlockSpec((MG_TM, SUBLANES, D // SUBLANES), lambda i: (i, 0, 0)), tok(LANES)],
        out_shape=[jax.ShapeDtypeStruct((T, D), F32), jax.ShapeDtypeStruct((T, SUBLANES, D // SUBLANES), F32),
                   jax.ShapeDtypeStruct((T, LANES), F32)],
        compiler_params=_cparams(("parallel",)),
        name="merge",
    )(o_nsa, o_hg, mg, x2, wn, wh, wo, fn, wr1, wr2, br)


MOE_TM = 256


def _moe_kernel(te_ref, nv_ref, src_ref, srcn_ref, dst_ref, h_hbm, wg_ref, wu_ref, wd_ref, y_hbm,
                xbuf, ybuf, wgb, wub, wdb, gsem, ssem):
    i = pl.program_id(0)
    n = pl.num_programs(0)
    tm = xbuf.shape[1]
    slot = i & 1
    prv = jnp.maximum(i - 1, 0)
    nxt = jnp.minimum(i + 1, n - 1)
    nv_i = nv_ref[i]
    nv_prev = jnp.where(i > 0, nv_ref[prv], 0)
    nv_next = jnp.where(i + 1 < n, nv_ref[nxt], 0)

    def gather_copy(idx_ref, r, sl):
        return pltpu.make_async_copy(h_hbm.at[idx_ref[0, 0, r]], xbuf.at[sl, r], gsem.at[sl])

    def scatter_copy(row, r, sl):
        return pltpu.make_async_copy(ybuf.at[sl, r], y_hbm.at[row], ssem.at[sl])

    def rows(count, fn):
        groups = jnp.right_shift(count, 3)

        def body8(g, c):
            for k in range(8):
                fn(g * 8 + k)
            return c

        def body1(r, c):
            fn(r)
            return c

        lax.fori_loop(0, groups, body8, 0)
        lax.fori_loop(groups * 8, count, body1, 0)

    def wait_rows(count, copy_of):
        k = tm
        while k >= 1:
            pl.when((count & k) != 0)(lambda k=k: copy_of(k).wait())
            k //= 2

    def gather_block(k, sl):
        return pltpu.make_async_copy(h_hbm.at[pl.ds(0, k)], xbuf.at[sl, pl.ds(0, k)], gsem.at[sl])

    def scatter_block(k, sl):
        return pltpu.make_async_copy(ybuf.at[sl, pl.ds(0, k)], y_hbm.at[pl.ds(0, k)], ssem.at[sl])

    @pl.when(i == 0)
    def _():
        xbuf[...] = jnp.zeros(xbuf.shape, F32)
        rows(nv_i, lambda r: gather_copy(src_ref, r, 0).start())

    rows(nv_next, lambda r: gather_copy(srcn_ref, r, 1 - slot).start())

    @pl.when((i == 0) | (te_ref[i] != te_ref[prv]))
    def _():
        wgb[...] = wg_ref[0].astype(BF16)
        wub[...] = wu_ref[0].astype(BF16)
        wdb[...] = wd_ref[0].astype(BF16)

    @pl.when(nv_i > 0)
    def _():
        wait_rows(nv_i, lambda k: gather_block(k, slot))
        x = _from_row_tiles(xbuf, (slot,)).astype(BF16)
        hg = jnp.dot(x, wgb[...], preferred_element_type=F32)
        hu = jnp.dot(x, wub[...], preferred_element_type=F32)
        hid = (hg * jax.nn.sigmoid(hg) * hu).astype(BF16)
        _to_row_tiles(ybuf, jnp.dot(hid, wdb[...], preferred_element_type=F32), (slot,))
        rows(nv_i, lambda r: scatter_copy(dst_ref[0, 0, r], r, slot).start())

    wait_rows(nv_prev, lambda k: scatter_block(k, 1 - slot))

    @pl.when(i == n - 1)
    def _():
        wait_rows(nv_i, lambda k: scatter_block(k, slot))


def _moe(h2, route, w_gate, w_up, w_down):
    T = h2.shape[0]
    E, D, F = w_gate.shape
    tm = MOE_TM
    n_tiles = 2 * T // tm + E
    i32 = jnp.int32
    ex = jnp.concatenate([route[:, 0], route[:, 1]]).astype(i32)
    order = jnp.argsort(ex, stable=True).astype(i32)
    counts = jnp.sum(ex[:, None] == jnp.arange(E, dtype=i32)[None, :], axis=0).astype(i32)
    padded = ((counts + tm - 1) // tm) * tm
    pend = jnp.cumsum(padded)
    pstart = pend - padded
    cstart = jnp.cumsum(counts) - counts
    tile_start = jnp.arange(n_tiles, dtype=i32) * tm
    te_raw = jnp.sum(tile_start[:, None] >= pend[None, :], axis=1).astype(i32)
    active = te_raw < E
    te_c = jnp.minimum(te_raw, E - 1)
    off = tile_start - pstart[te_c]
    nv = jnp.where(active, jnp.clip(counts[te_c] - off, 0, tm), 0).astype(i32)
    n_active = jnp.sum(active.astype(i32))
    te = jnp.where(active, te_c, te_c[jnp.maximum(n_active - 1, 0)])
    rank = off[:, None] + jnp.arange(tm, dtype=i32)[None, :]
    a = order[jnp.clip(cstart[te_c][:, None] + rank, 0, 2 * T - 1)]
    valid = jnp.arange(tm, dtype=i32)[None, :] < nv[:, None]
    src = jnp.where(valid, a % T, 0).reshape(n_tiles, 1, tm)
    dst = jnp.where(valid, a, 0).reshape(n_tiles, 1, tm)

    idx_spec = lambda f: pl.BlockSpec((1, 1, tm), lambda i, te, nv: (f(i), 0, 0), memory_space=pltpu.SMEM)
    wspec = lambda r, c: pl.BlockSpec((1, r, c), lambda i, te, nv: (te[i], 0, 0))
    grid_spec = pltpu.PrefetchScalarGridSpec(
        num_scalar_prefetch=2,
        grid=(n_tiles,),
        in_specs=[idx_spec(lambda i: i), idx_spec(lambda i: jnp.minimum(i + 1, n_tiles - 1)),
                  idx_spec(lambda i: i), pl.BlockSpec(memory_space=pl.ANY),
                  wspec(D, F), wspec(D, F), wspec(F, D)],
        out_specs=pl.BlockSpec(memory_space=pl.ANY),
        scratch_shapes=[pltpu.VMEM((2, tm, SUBLANES, D // SUBLANES), F32),
                        pltpu.VMEM((2, tm, SUBLANES, D // SUBLANES), F32),
                        pltpu.VMEM((D, F), BF16), pltpu.VMEM((D, F), BF16), pltpu.VMEM((F, D), BF16),
                        pltpu.SemaphoreType.DMA((2,)), pltpu.SemaphoreType.DMA((2,))],
    )
    return pl.pallas_call(
        _moe_kernel,
        grid_spec=grid_spec,
        out_shape=jax.ShapeDtypeStruct((2 * T, SUBLANES, D // SUBLANES), F32),
        compiler_params=_cparams(("arbitrary",)),
        name="moe",
    )(te, nv, src, src, dst, h2, w_gate, w_up, w_down)


FIN_TM = 256


def _final_kernel(x1_ref, y1_ref, y2_ref, rt_ref, g_ref, o_ref):
    rt = rt_ref[...]
    x = x1_ref[...] + rt[:, 2:3] * _from_row_tiles(y1_ref) + rt[:, 3:4] * _from_row_tiles(y2_ref)
    o_ref[...] = x * lax.rsqrt(jnp.mean(x * x, axis=-1, keepdims=True) + RMS_EPS) * g_ref[...]


def _final(x1, y, route, final_norm):
    T, D = x1.shape
    nb = T // FIN_TM
    ytile = (FIN_TM, SUBLANES, D // SUBLANES)
    return pl.pallas_call(
        _final_kernel,
        grid=(nb,),
        in_specs=[pl.BlockSpec((FIN_TM, D), lambda i: (i, 0)),
                  pl.BlockSpec(ytile, lambda i: (i, 0, 0)),
                  pl.BlockSpec(ytile, lambda i: (i + nb, 0, 0)),
                  pl.BlockSpec((FIN_TM, LANES), lambda i: (i, 0)),
                  pl.BlockSpec((1, D), lambda i: (0, 0))],
        out_specs=pl.BlockSpec((FIN_TM, D), lambda i: (i, 0)),
        out_shape=jax.ShapeDtypeStruct((T, D), F32),
        compiler_params=_cparams(("parallel",)),
        name="final",
    )(x1, y, y, route, final_norm.reshape(1, D))


def kernel(x, attn_norm, w_in, w_cmp_k, w_cmp_v, cmp_pos, hg_lb_logits, hg_norm, w_br_nsa, w_br_hg, w_out,
           ffn_norm, w_grp, b_grp, w_rtr, b_rtr, w_gate, w_up, w_down, final_norm):
    B, S, D = x.shape
    assert attn_norm.shape[0] == 1, "single-layer block"
    x2 = x.reshape(B * S, D)
    q, kcin, vcin, kv, gate, hq, hf, hi, hg, mg = _in_proj(x2, attn_norm[0], w_in[0])
    qt, ksa, vst, kw, vwt, gtt = _nsa_prep(q, kv, gate, B, S)
    kc, vct = _compress(kcin, vcin, w_cmp_k[0], w_cmp_v[0], cmp_pos[0], B, S)
    o_nsa = _nsa_attn(qt, kc, vct, ksa, vst, kw, vwt, gtt, B, S).reshape(B * S, -1)
    o_hg = _hgrn(hq, hf, hi, hg, hg_lb_logits, hg_norm[0], B, S).reshape(B * S, -1)
    x1, h2, route = _merge(o_nsa, o_hg, mg, x2, w_br_nsa[0], w_br_hg[0], w_out[0], ffn_norm[0],
                           w_grp[0], b_grp[0], w_rtr[0], b_rtr[0])
    y = _moe(h2, route, w_gate[0], w_up[0], w_down[0])
    out = _final(x1, y, route, final_norm)
    return out.reshape(B, S, D)
```

```python
import functools

import numpy as np
import jax
import jax.numpy as jnp
from jax import lax
from jax.experimental import pallas as pl
from jax.experimental.pallas import tpu as pltpu

F32 = jnp.float32
BF16 = jnp.bfloat16

NSA_HEADS = 8
NSA_KV_GROUPS = 2
NSA_HEAD_DIM = 64
NSA_R = NSA_HEADS // NSA_KV_GROUPS
CMP_LEN = 32
CMP_STRIDE = 16
SLC_LEN = 64
SLC_SHIFT = 6
SLC_TOPN = 16
WINDOW = 512
Q_BLOCK = 128
ROPE_THETA = 500000.0
ROPE_DIM = NSA_HEAD_DIM // 4
HG_HEADS = 4
HG_DIM = 128
HG_CHUNK = 64
N_GROUPS = 4
EXPERTS_PER_GROUP = 8
N_EXPERTS = N_GROUPS * EXPERTS_PER_GROUP
RMS_EPS = 1e-6
NEG = -1e30
FORCE_SCORE = 1e4

LANES = 128
VMEM_LIMIT = 56 * 1024 * 1024

NT_DIMS = (((1,), (1,)), ((), ()))
TN_DIMS = (((0,), (0,)), ((), ()))


def _cparams(sem):
    return pltpu.CompilerParams(dimension_semantics=sem, vmem_limit_bytes=VMEM_LIMIT)


IN_TM = 256
IN_SEGS = (("q", 512, 512), ("kc", 128, 128), ("vc", 128, 128), ("kv", 512, 512), ("gate", 24, 128),
           ("hq", 512, 512), ("hf", 512, 512), ("hi", 512, 512), ("hg", 512, 512), ("merge", 2048, 2048))
IN_F32 = ("kc", "vc", "gate")


def _in_proj_kernel(x_ref, g_ref, w_ref, *out_refs):
    x = x_ref[...]
    h = x * lax.rsqrt(jnp.mean(x * x, axis=-1, keepdims=True) + RMS_EPS) * g_ref[...]
    h = h.astype(BF16)
    off = 0
    for (_, _, wpad), o_ref in zip(IN_SEGS, out_refs):
        y = jnp.dot(h, w_ref[:, off:off + wpad], preferred_element_type=F32)
        o_ref[...] = y.astype(o_ref.dtype)
        off += wpad


def _in_proj(x2, attn_norm, w_in):
    T, D = x2.shape
    pieces, off = [], 0
    for _, w, wpad in IN_SEGS:
        p = w_in[:, off:off + w]
        if wpad != w:
            p = jnp.pad(p, ((0, 0), (0, wpad - w)))
        pieces.append(p)
        off += w
    wcat = jnp.concatenate(pieces, axis=1).astype(BF16)
    NP = wcat.shape[1]
    out_shapes = [jax.ShapeDtypeStruct((T, wpad), F32 if name in IN_F32 else BF16)
                  for name, _, wpad in IN_SEGS]
    out_specs = [pl.BlockSpec((IN_TM, wpad), lambda i: (i, 0)) for _, _, wpad in IN_SEGS]
    return pl.pallas_call(
        _in_proj_kernel,
        grid=(T // IN_TM,),
        in_specs=[pl.BlockSpec((IN_TM, D), lambda i: (i, 0)),
                  pl.BlockSpec((1, D), lambda i: (0, 0)),
                  pl.BlockSpec((D, NP), lambda i: (0, 0))],
        out_specs=out_specs,
        out_shape=out_shapes,
        compiler_params=_cparams(("parallel",)),
        name="in_proj",
    )(x2, attn_norm.reshape(1, D), wcat)


def _rope_tables(pos):
    n = pos.shape[0]
    half = ROPE_DIM // 2
    inv_freq = ROPE_THETA ** (-jnp.arange(half, dtype=F32) / half)
    ang = pos.astype(F32)[:, None] * inv_freq[None, :]
    cos, sin = jnp.cos(ang), jnp.sin(ang)
    rest = NSA_HEAD_DIM - ROPE_DIM
    z8 = jnp.zeros((n, half), F32)
    c64 = jnp.concatenate([cos, cos, jnp.ones((n, rest), F32)], axis=1)
    s1 = jnp.concatenate([-sin, z8, jnp.zeros((n, rest), F32)], axis=1)
    s2 = jnp.concatenate([z8, sin, jnp.zeros((n, rest), F32)], axis=1)
    tile = lambda a: jnp.concatenate([a, a], axis=1)
    return tile(c64), tile(s1), tile(s2)


def _rope(x, c, s1, s2):
    half = ROPE_DIM // 2
    return x * c + pltpu.roll(x, LANES - half, 1) * s1 + pltpu.roll(x, half, 1) * s2


PREP_TS = 512


GATE_ROWS = 16
LOG2E = 1.4426950408889634


def _prep_kernel(q_ref, kv_ref, gate_ref, c_ref, s1_ref, s2_ref, nege_ref,
                 qt_ref, ksa_ref, vst_ref, kw_ref, vwt_ref, gtt_ref):
    c, s1, s2 = c_ref[...], s1_ref[...], s2_ref[...]
    lane = lax.broadcasted_iota(jnp.int32, c.shape, 1)
    lo = lane < NSA_HEAD_DIM
    dk, QB = NSA_HEAD_DIM, Q_BLOCK

    def split(x):
        return (jnp.where(lo, x, 0.0), jnp.where(lo, pltpu.roll(x, dk, 1), 0.0))

    scale = (dk ** -0.5) * LOG2E
    for blk in range(NSA_HEADS // 2):
        x = q_ref[:, blk * LANES:(blk + 1) * LANES].astype(F32)
        for hh, part in enumerate(split(_rope(x, c, s1, s2) * scale)):
            h = 2 * blk + hh
            g, r = h // NSA_R, h % NSA_R
            pt = part.T.astype(BF16)
            for i in range(PREP_TS // QB):
                qt_ref[0, g, i, :, r * QB:(r + 1) * QB] = pt[:, i * QB:(i + 1) * QB]
    nege = nege_ref[...]
    for g, part in enumerate(split(_rope(kv_ref[:, 0:LANES].astype(F32), c, s1, s2))):
        ksa_ref[0, g] = jnp.concatenate([part.astype(BF16), nege], axis=1)
    for g, part in enumerate(split(_rope(kv_ref[:, 2 * LANES:3 * LANES].astype(F32), c, s1, s2))):
        kw_ref[0, g] = part.astype(BF16)
    for blk, ref in ((1, vst_ref), (3, vwt_ref)):
        for g, part in enumerate(split(kv_ref[:, blk * LANES:(blk + 1) * LANES].astype(F32))):
            ref[0, g] = part.T[0:dk, :].astype(BF16)
    sg = jax.nn.sigmoid(gate_ref[...])
    gtt_ref[0, 0] = sg.T[0:GATE_ROWS, :]
    gtt_ref[0, 1] = pltpu.roll(sg, LANES - 3 * NSA_R, 1).T[0:GATE_ROWS, :]


def _nsa_prep(q, kv, gate, B, S):
    G, dk, QB = NSA_KV_GROUPS, NSA_HEAD_DIM, Q_BLOCK
    c, s1, s2 = _rope_tables(jnp.arange(S))
    key = np.arange(S)
    nege = jnp.asarray(np.where(key[:, None] // SLC_LEN == np.arange(LANES)[None, :], NEG, 0.0), BF16)
    nj = S // PREP_TS
    tok = lambda w: pl.BlockSpec((PREP_TS, w), lambda b, j: (b * nj + j, 0))
    tab = pl.BlockSpec((PREP_TS, LANES), lambda b, j: (j, 0))
    rows_out = lambda w, dt: (jax.ShapeDtypeStruct((B, G, S, w), dt),
                              pl.BlockSpec((1, G, PREP_TS, w), lambda b, j: (b, 0, j, 0)))
    cols_out = lambda n, dt: (jax.ShapeDtypeStruct((B, G, n, S), dt),
                              pl.BlockSpec((1, G, n, PREP_TS), lambda b, j: (b, 0, 0, j)))
    qt_out = (jax.ShapeDtypeStruct((B, G, S // QB, LANES, NSA_R * QB), BF16),
              pl.BlockSpec((1, G, PREP_TS // QB, LANES, NSA_R * QB), lambda b, j: (b, 0, j, 0, 0)))
    outs = [qt_out, rows_out(2 * LANES, BF16), cols_out(dk, BF16), rows_out(LANES, BF16),
            cols_out(dk, BF16), cols_out(GATE_ROWS, F32)]
    return pl.pallas_call(
        _prep_kernel,
        grid=(B, nj),
        in_specs=[tok(512), tok(512), tok(LANES), tab, tab, tab, tab],
        out_specs=[o[1] for o in outs],
        out_shape=[o[0] for o in outs],
        compiler_params=_cparams(("parallel", "parallel")),
        name="nsa_prep",
    )(q, kv, gate, c, s1, s2, nege)


def _compress_kernel(tk_ref, tv_ref, wk_ref, wv_ref, pos_ref, c_ref, s1_ref, s2_ref, kc_ref, vct_ref):
    nc = c_ref.shape[0]
    dk = NSA_HEAD_DIM
    lane = lax.broadcasted_iota(jnp.int32, (nc, LANES), 1)
    lo = lane < dk

    def comp(t_ref, w_ref):
        a = jnp.zeros((nc, LANES), F32)
        b = jnp.zeros((nc, LANES), F32)
        c0 = jnp.zeros((SUBLANES, LANES), F32)
        for j in range(CMP_STRIDE):
            u = t_ref[pl.ds(j, nc, stride=CMP_STRIDE), :].astype(BF16)
            a = a + jnp.dot(u, w_ref[j], preferred_element_type=F32)
            b = b + jnp.dot(u, w_ref[CMP_STRIDE + j], preferred_element_type=F32)
        for l in range(CMP_LEN):
            c0 = c0 + jnp.dot(pos_ref[l], w_ref[l], preferred_element_type=F32)
        return a + pltpu.roll(b, nc - 1, 0) + c0[0:1, :]

    def split(x):
        return (jnp.where(lo, x, 0.0), jnp.where(lo, pltpu.roll(x, dk, 1), 0.0))

    for g, part in enumerate(split(_rope(comp(tk_ref, wk_ref), c_ref[...], s1_ref[...], s2_ref[...]))):
        kc_ref[0, g] = part.astype(BF16)
    for g, part in enumerate(split(comp(tv_ref, wv_ref))):
        vct_ref[0, g] = part.T[0:dk, :].astype(BF16)


def _compress(kcin, vcin, w_cmp_k, w_cmp_v, cmp_pos, B, S):
    G, dk = NSA_KV_GROUPS, NSA_HEAD_DIM
    nc = S // CMP_STRIDE

    def block_diag(w):
        w3 = w.reshape(CMP_LEN, dk, dk)
        z = jnp.zeros_like(w3)
        return jnp.concatenate([jnp.concatenate([w3, z], axis=2),
                                jnp.concatenate([z, w3], axis=2)], axis=1).astype(BF16)

    pos = jnp.concatenate([cmp_pos, cmp_pos], axis=1)
    pos = jnp.broadcast_to(pos[:, None, :], (CMP_LEN, SUBLANES, G * dk)).astype(BF16)
    c, s1, s2 = _rope_tables(jnp.arange(nc) * CMP_STRIDE)
    tspec = pl.BlockSpec((S, LANES), lambda b: (b, 0))
    wspec = pl.BlockSpec((CMP_LEN, LANES, LANES), lambda b: (0, 0, 0))
    tab = pl.BlockSpec((nc, LANES), lambda b: (0, 0))
    return pl.pallas_call(
        _compress_kernel,
        grid=(B,),
        in_specs=[tspec, tspec, wspec, wspec,
                  pl.BlockSpec((CMP_LEN, SUBLANES, LANES), lambda b: (0, 0, 0)), tab, tab, tab],
        out_specs=[pl.BlockSpec((1, G, nc, LANES), lambda b: (b, 0, 0, 0)),
                   pl.BlockSpec((1, G, dk, nc), lambda b: (b, 0, 0, 0))],
        out_shape=[jax.ShapeDtypeStruct((B, G, nc, LANES), BF16),
                   jax.ShapeDtypeStruct((B, G, dk, nc), BF16)],
        compiler_params=_cparams(("parallel",)),
        name="compress",
    )(kcin, vcin, block_diag(w_cmp_k), block_diag(w_cmp_v), pos, c, s1, s2)


SEL_KC = 256
SEL_KC_SHIFT = 8
WIN_KEYS = WINDOW + Q_BLOCK


def _col_reduce(x, op, fin):
    n = x.shape[0]
    while (n // 2) % 8 == 0 and n > 8:
        n //= 2
        x = op(x[:n], x[n:])
    return fin(x, axis=0, keepdims=True)


def _col_max(x):
    return _col_reduce(x, jnp.maximum, jnp.max)


def _col_min(x):
    return _col_reduce(x, jnp.minimum, jnp.min)


def _col_sum(x):
    return _col_reduce(x, jnp.add, jnp.sum)


def _softmax_cols(s):
    e = jnp.exp2(s - _col_max(s))
    return e, _col_sum(e)


def _nsa_kernel(qt_ref, kc_ref, vct_ref, ksa_ref, vst_ref, kw_ref, vwt_ref, gtt_ref, ovt_ref,
                o_ref, rhs_sc, m0_sc, m1_sc, l0_sc, l1_sc, a0_sc, a1_sc, sa_sc, sb_sc, *, n_sel):
    m_scs, l_scs, acc_scs = (m0_sc, m1_sc), (l0_sc, l1_sc), (a0_sc, a1_sc)
    s_bufs = (sa_sc, sb_sc)
    R, QB = NSA_R, Q_BLOCK
    cols = R * QB
    i = pl.program_id(2)
    t0 = i * QB
    qt = qt_ref[0, 0, 0]

    def per_head(x):
        return jnp.concatenate([x] * (cols // x.shape[1]), axis=1)

    def rc(n):
        return (lax.broadcasted_iota(jnp.int32, (n, QB), 0),
                t0 + lax.broadcasted_iota(jnp.int32, (n, QB), 1))

    nc = kc_ref.shape[2]
    sc = jnp.dot(kc_ref[0, 0], qt, preferred_element_type=F32)
    n_idx, tq_c = rc(nc)
    cvis = (n_idx * CMP_STRIDE + (CMP_LEN - 1)) <= tq_c
    ec, lc = _softmax_cols(sc + per_head(jnp.where(cvis, 0.0, NEG)))
    pc = ec * per_head(jnp.where(cvis, 1.0, 0.0)) * (1.0 / lc)
    o_cmp = jnp.dot(vct_ref[0, 0], pc.astype(BF16), preferred_element_type=F32)

    start = pl.multiple_of(jnp.maximum(t0 - WINDOW, 0), QB)
    sw = jnp.dot(kw_ref[0, 0, pl.ds(start, WIN_KEYS), :], qt, preferred_element_type=F32)

    psum = pc[:, 0:QB] + pc[:, QB:2 * QB] + pc[:, 2 * QB:3 * QB] + pc[:, 3 * QB:4 * QB]
    p_hi = psum.astype(BF16)
    p_lo = (psum - p_hi.astype(F32)).astype(BF16)
    ovt = ovt_ref[...]
    imp = (jnp.dot(ovt, p_hi, preferred_element_type=F32) +
           jnp.dot(ovt, p_lo, preferred_element_type=F32))
    j = lax.broadcasted_iota(jnp.int32, (LANES, QB), 0)
    tq = t0 + lax.broadcasted_iota(jnp.int32, (LANES, QB), 1)
    force = (j == jnp.right_shift(tq, SLC_SHIFT)) | (j == 0)
    valid = (j * SLC_LEN) <= tq
    score = jnp.where(force, -jnp.inf, jnp.where(valid, imp, -1.0))
    work = jnp.where(j < n_sel, score, -jnp.inf)
    jf = j.astype(F32)
    for _ in range(min(SLC_TOPN, n_sel) - 2):
        mx = _col_max(work)
        idx = _col_min(jnp.where(work == mx, jf, float(LANES)))
        work = jnp.where(jf == idx, -jnp.inf, work)
    notsel = jnp.where(work == -jnp.inf, 0.0, 1.0)

    k_idx, tq_w = rc(WIN_KEYS)
    rel = tq_w - (start + k_idx)
    pw, lw = _softmax_cols(sw + per_head(jnp.where((rel >= 0) & (rel < WINDOW), 0.0, NEG)))
    o_win = jnp.dot(vwt_ref[0, 0, :, pl.ds(start, WIN_KEYS)], pw.astype(BF16),
                    preferred_element_type=F32) * (1.0 / lw)

    rhs_sc[0:LANES, :] = qt
    rhs_sc[LANES:2 * LANES, :] = jnp.concatenate([notsel.astype(BF16)] * R, axis=1)
    halves = tuple(zip(m_scs, l_scs, acc_scs))
    hw = cols // len(halves)
    for m_sc, l_sc, acc_sc in halves:
        m_sc[...] = jnp.full(m_sc.shape, NEG, F32)
        l_sc[...] = jnp.zeros(l_sc.shape, F32)
        acc_sc[...] = jnp.zeros(acc_sc.shape, F32)

    def sel_scores(chunk_ids):
        kas = [ksa_ref[0, 0, pl.ds(pl.multiple_of(c * SEL_KC, SEL_KC), SEL_KC), :] for c in chunk_ids]
        return [[jnp.dot(ka, rhs_sc[:, hp * hw:(hp + 1) * hw], preferred_element_type=F32)
                 for ka in kas] for hp in range(len(halves))]

    def sel_update(chunk_ids, scores, diagonal):
        k0s = [pl.multiple_of(c * SEL_KC, SEL_KC) for c in chunk_ids]
        vts = [vst_ref[0, 0, :, pl.ds(k0, SEL_KC)] for k0 in k0s]
        for hp, (m_sc, l_sc, acc_sc) in enumerate(halves):
            ss = scores[hp]
            if diagonal:
                k_idx, tq_s = rc(SEL_KC)
                ss = [s + jnp.concatenate([jnp.where(k0 + k_idx <= tq_s, 0.0, NEG)] * (hw // QB), axis=1)
                      for k0, s in zip(k0s, ss)]
            m_old = m_sc[...]
            m_new = functools.reduce(jnp.maximum, [_col_max(s) for s in ss], m_old)
            alpha = jnp.exp2(m_old - m_new)
            ps = [jnp.exp2(s - m_new) for s in ss]
            l_sc[...] = alpha * l_sc[...] + functools.reduce(jnp.add, [_col_sum(p) for p in ps])
            pv = [jnp.dot(vt, p.astype(BF16), preferred_element_type=F32) for vt, p in zip(vts, ps)]
            acc_sc[...] = alpha * acc_sc[...] + functools.reduce(jnp.add, pv)
            m_sc[...] = m_new

    n_units = jnp.right_shift(t0, SEL_KC_SHIFT + 1)
    nh = len(halves)

    def unit(u):
        return [2 * u, 2 * u + 1]

    def put_scores(buf, scores):
        for hp in range(nh):
            for c in range(2):
                buf[hp * 2 + c] = scores[hp][c]

    def get_scores(buf):
        return [[buf[hp * 2 + c] for c in range(2)] for hp in range(nh)]

    put_scores(s_bufs[0], sel_scores(unit(0)))

    def sel_step(u, carry):
        for parity in range(2):
            @pl.when((u & 1) == parity)
            def _():
                put_scores(s_bufs[1 - parity], sel_scores(unit(u + 1)))
                sel_update(unit(u), get_scores(s_bufs[parity]), False)
        return carry

    lax.fori_loop(0, n_units, sel_step, 0)
    for parity in range(2):
        @pl.when((n_units & 1) == parity)
        def _():
            sel_update(unit(n_units), get_scores(s_bufs[parity]), True)
    o_slc = jnp.concatenate([a[...] * (1.0 / l[...]) for _, l, a in halves], axis=1)

    gtt = gtt_ref[0, 0]
    outs = []
    for r in range(R):
        cs = slice(r * QB, (r + 1) * QB)
        outs.append(gtt[3 * r:3 * r + 1, :] * o_cmp[:, cs] + gtt[3 * r + 1:3 * r + 2, :] * o_slc[:, cs] +
                    gtt[3 * r + 2:3 * r + 3, :] * o_win[:, cs])
    o_ref[0] = jnp.concatenate(outs, axis=0).T.astype(o_ref.dtype)


def _nsa_attn(qt, kc, vct, ksa, vst, kw, vwt, gtt, B, S):
    G, R, QB, dk = NSA_KV_GROUPS, NSA_R, Q_BLOCK, NSA_HEAD_DIM
    nc = S // CMP_STRIDE
    n_sel = S // SLC_LEN
    assert n_sel <= LANES and n_sel >= SLC_TOPN and S % (2 * SEL_KC) == 0 and S >= WIN_KEYS
    assert SEL_KC == 1 << SEL_KC_SHIFT and SEL_KC % QB == 0
    cs = np.arange(nc)[None, :] * CMP_STRIDE
    ss = np.arange(LANES)[:, None] * SLC_LEN
    ovm = (cs < ss + SLC_LEN) & (cs + CMP_LEN - 1 >= ss) & (np.arange(LANES)[:, None] < n_sel) \
        & (np.arange(nc)[None, :] < nc - 1)
    ovt = jnp.asarray(ovm, BF16)
    rows_in = lambda n, w: pl.BlockSpec((1, 1, n, w), lambda b, g, i: (b, g, 0, 0))
    cols = R * QB
    return pl.pallas_call(
        functools.partial(_nsa_kernel, n_sel=n_sel),
        grid=(B, G, S // QB),
        in_specs=[pl.BlockSpec((1, 1, 1, LANES, cols), lambda b, g, i: (b, g, i, 0, 0)),
                  rows_in(nc, LANES), rows_in(dk, nc), rows_in(S, 2 * LANES), rows_in(dk, S),
                  rows_in(S, LANES), rows_in(dk, S),
                  pl.BlockSpec((1, 1, GATE_ROWS, QB), lambda b, g, i: (b, g, 0, i)),
                  pl.BlockSpec((LANES, nc), lambda b, g, i: (0, 0))],
        out_specs=pl.BlockSpec((1, QB, R * dk), lambda b, g, i: (b, i, g)),
        out_shape=jax.ShapeDtypeStruct((B, S, NSA_HEADS * dk), BF16),
        scratch_shapes=[pltpu.VMEM((2 * LANES, cols), BF16)] +
                       [pltpu.VMEM((1, cols // 2), F32)] * 4 + [pltpu.VMEM((dk, cols // 2), F32)] * 2 +
                       [pltpu.VMEM((4, SEL_KC, cols // 2), F32)] * 2,
        compiler_params=_cparams(("parallel", "parallel", "arbitrary")),
        name="nsa_attn",
    )(qt, kc, vct, ksa, vst, kw, vwt, gtt, ovt)


HG_TB = 256
HG_LEVELS = 6


def _hgrn_consts():
    C = HG_CHUNK
    M = np.zeros((HG_LEVELS + 2, C, C), np.float32)
    RM = np.zeros((HG_LEVELS, C, 1), np.float32)
    BM = np.zeros((HG_LEVELS + 1, C, C), np.float32)
    for lv in range(HG_LEVELS):
        m = 1 << lv
        for t in range(C):
            bs = (t // (2 * m)) * (2 * m)
            mid = bs + m - 1
            if t - bs >= m:
                RM[lv, t, 0] = 1.0
                M[lv, t, mid + 1:t + 1] = 1.0
            else:
                M[lv, t, t + 1:mid + 1] = 1.0
            BM[lv, t, bs:bs + 2 * m] = 1.0
    BM[HG_LEVELS] = np.eye(C, dtype=np.float32)
    M[HG_LEVELS] = np.tril(np.ones((C, C), np.float32))
    M[HG_LEVELS + 1] = np.triu(np.ones((C, C), np.float32), 1)
    mm = M.reshape(-1, C)
    mm3 = np.concatenate([mm, mm, mm], axis=1)
    rm = np.broadcast_to(RM, (HG_LEVELS, C, HG_HEADS * HG_DIM)).copy()
    return jnp.asarray(mm3, BF16), jnp.asarray(rm, F32), jnp.asarray(BM, F32)


def _hgrn_kernel(q_ref, f_ref, i_ref, g_ref, lbl_ref, gn_ref, mm_ref, rm_ref, bm_ref, o_ref, st_sc):
    C, D = HG_CHUNK, HG_DIM

    @pl.when(pl.program_id(1) == 0)
    def _():
        st_sc[...] = jnp.zeros(st_sc.shape, F32)

    lbl = lbl_ref[...]
    mxl = jnp.max(lbl, axis=0, keepdims=True)
    el = jnp.exp(lbl - mxl)
    lb_all = el[0:1, :] / jnp.sum(el, axis=0, keepdims=True)
    mm = mm_ref[...]
    gn = gn_ref[...]
    heads = [slice(h * D, (h + 1) * D) for h in range(HG_HEADS)]
    nt = lambda a, b: lax.dot_general(a, b, NT_DIMS, preferred_element_type=F32)
    for ch in range(HG_TB // C):
        rs = slice(ch * C, (ch + 1) * C)
        q = q_ref[0, rs, :].astype(F32)
        v = i_ref[0, rs, :]
        gg = g_ref[0, rs, :].astype(F32)
        f = lb_all + (1.0 - lb_all) * jax.nn.sigmoid(f_ref[0, rs, :].astype(F32))
        lg = jnp.log(f)
        kk = 1.0 - f
        g1 = lg.astype(BF16)
        r1 = lg - g1.astype(F32)
        g2 = r1.astype(BF16)
        g3 = (r1 - g2.astype(F32)).astype(BF16)
        ez = jnp.exp(jnp.dot(mm, jnp.concatenate([g1, g2, g3], axis=0), preferred_element_type=F32))
        qb, kb = q.astype(BF16), kk.astype(BF16)
        attn = [bm_ref[HG_LEVELS] * nt(qb[:, hs], kb[:, hs]) for hs in heads]
        for lv in range(HG_LEVELS):
            zl = ez[lv * C:(lv + 1) * C]
            rm = rm_ref[lv]
            ql = (q * zl * rm).astype(BF16)
            kl = (kk * zl * (1.0 - rm)).astype(BF16)
            attn = [a + bm_ref[lv] * nt(ql[:, hs], kl[:, hs]) for a, hs in zip(attn, heads)]
        zc = ez[HG_LEVELS * C:(HG_LEVELS + 1) * C]
        zs = ez[(HG_LEVELS + 1) * C:(HG_LEVELS + 2) * C]
        qz = (q * zc).astype(BF16)
        kz = (kk * zs).astype(BF16)
        sts = [st_sc[h] for h in range(HG_HEADS)]
        o_intra = [jnp.dot(a.astype(BF16), v[:, hs], preferred_element_type=F32) for a, hs in zip(attn, heads)]
        o_inter = [nt(qz[:, hs], st.astype(BF16)) for st, hs in zip(sts, heads)]
        upd = [lax.dot_general(v[:, hs], kz[:, hs], TN_DIMS, preferred_element_type=F32) for hs in heads]
        for h, hs in enumerate(heads):
            st_sc[h] = zc[C - 1:C, hs] * sts[h] + upd[h]
            o = o_intra[h] + o_inter[h]
            y = o * lax.rsqrt(jnp.mean(o * o, axis=-1, keepdims=True) + RMS_EPS) * gn
            g = gg[:, hs]
            o_ref[0, rs, hs] = (y * (g * jax.nn.sigmoid(g))).astype(o_ref.dtype)


def _hgrn(hq, hf, hi, hg, lb_logits, hg_norm, B, S):
    W = HG_HEADS * HG_DIM
    mm3, rm, bm = _hgrn_consts()
    r3 = lambda a: a.reshape(B, S, W)
    tok = pl.BlockSpec((1, HG_TB, W), lambda b, j: (b, j, 0))
    const = lambda a: pl.BlockSpec(a.shape, lambda b, j: (0,) * a.ndim)
    gn = hg_norm.reshape(1, HG_DIM)
    return pl.pallas_call(
        _hgrn_kernel,
        grid=(B, S // HG_TB),
        in_specs=[tok, tok, tok, tok, const(lb_logits), const(gn), const(mm3), const(rm), const(bm)],
        out_specs=tok,
        out_shape=jax.ShapeDtypeStruct((B, S, W), BF16),
        scratch_shapes=[pltpu.VMEM((HG_HEADS, HG_DIM, HG_DIM), F32)],
        compiler_params=_cparams(("parallel", "arbitrary")),
        name="hgrn2",
    )(r3(hq), r3(hf), r3(hi), r3(hg), lb_logits, gn, mm3, rm, bm)


MG_TM = 256


SUBLANES = 8


def _to_row_tiles(ref, x, lead=()):
    for s in range(SUBLANES):
        ref[lead + (slice(None), s, slice(None))] = x[:, s * LANES:(s + 1) * LANES]


def _from_row_tiles(ref, lead=()):
    return jnp.concatenate([ref[lead + (slice(None), s, slice(None))] for s in range(SUBLANES)], axis=1)


def _first_lane(cond, lane_f):
    return jnp.min(jnp.where(cond, lane_f, float(LANES)), axis=1, keepdims=True)


def _merge_kernel(on_ref, oh_ref, mg_ref, x_ref, wn_ref, wh_ref, wo_ref, fn_ref, wr1_ref, wr2_ref, br_ref,
                  x1_ref, h2_ref, rt_ref):
    D = x_ref.shape[1]
    a = jnp.dot(on_ref[...], wn_ref[...], preferred_element_type=F32)
    b = jnp.dot(oh_ref[...], wh_ref[...], preferred_element_type=F32)
    mixed = (jax.nn.sigmoid(mg_ref[:, 0:D].astype(F32)) * a +
             jax.nn.sigmoid(mg_ref[:, D:2 * D].astype(F32)) * b)
    x1 = x_ref[...] + jnp.dot(mixed.astype(BF16), wo_ref[...], preferred_element_type=F32)
    x1_ref[...] = x1
    h2 = x1 * lax.rsqrt(jnp.mean(x1 * x1, axis=-1, keepdims=True) + RMS_EPS) * fn_ref[...]
    _to_row_tiles(h2_ref, h2)
    h_hi = h2.astype(BF16)
    h_lo = (h2 - h_hi.astype(F32)).astype(BF16)
    logits = (jnp.dot(h_lo, wr1_ref[...], preferred_element_type=F32) +
              jnp.dot(h_hi, wr2_ref[...], preferred_element_type=F32) +
              jnp.dot(h_hi, wr1_ref[...], preferred_element_type=F32)) + br_ref[...]
    lane = lax.broadcasted_iota(jnp.int32, logits.shape, 1)
    ninf = -jnp.inf
    gl = jnp.where(lane < N_GROUPS, logits, ninf)
    gmax = jnp.max(gl, axis=1, keepdims=True)
    lane_f = lane.astype(F32)
    gsel = _first_lane(gl == gmax, lane_f)
    gprob = 1.0 / jnp.sum(jnp.exp(gl - gmax), axis=1, keepdims=True)
    lo = N_GROUPS + gsel * EXPERTS_PER_GROUP
    el = jnp.where((lane_f >= lo) & (lane_f < lo + EXPERTS_PER_GROUP), logits, ninf)
    v1 = jnp.max(el, axis=1, keepdims=True)
    i1 = _first_lane(el == v1, lane_f)
    el2 = jnp.where(lane_f == i1, ninf, el)
    v2 = jnp.max(el2, axis=1, keepdims=True)
    i2 = _first_lane(el2 == v2, lane_f)
    e2 = jnp.exp(v2 - v1)
    w1 = gprob / (1.0 + e2)
    w2 = gprob * e2 / (1.0 + e2)
    rt = jnp.where(lane == 0, i1 - N_GROUPS,
                   jnp.where(lane == 1, i2 - N_GROUPS,
                             jnp.where(lane == 2, w1, jnp.where(lane == 3, w2, 0.0))))
    rt_ref[...] = rt


def _merge(o_nsa, o_hg, mg, x2, w_br_nsa, w_br_hg, w_out, ffn_norm, w_grp, b_grp, w_rtr, b_rtr):
    T, D = x2.shape
    nr = N_GROUPS + N_EXPERTS
    wr = jnp.pad(jnp.concatenate([w_grp, w_rtr], axis=1), ((0, 0), (0, LANES - nr)))
    wr1 = wr.astype(BF16)
    wr2 = (wr - wr1.astype(F32)).astype(BF16)
    br = jnp.pad(jnp.concatenate([b_grp, b_rtr]), (0, LANES - nr)).reshape(1, LANES)
    tok = lambda w: pl.BlockSpec((MG_TM, w), lambda i: (i, 0))
    const = lambda a: pl.BlockSpec(a.shape, lambda i: (0,) * a.ndim)
    wn, wh, wo = w_br_nsa.astype(BF16), w_br_hg.astype(BF16), w_out.astype(BF16)
    fn = ffn_norm.reshape(1, D)
    return pl.pallas_call(
        _merge_kernel,
        grid=(T // MG_TM,),
        in_specs=[tok(o_nsa.shape[1]), tok(o_hg.shape[1]), tok(2 * D), tok(D),
                  const(wn), const(wh), const(wo), const(fn), const(wr1), const(wr2), const(br)],
        out_specs=[tok(D), pl.BlockSpec((MG_TM, SUBLANES, D // SUBLANES), lambda i: (i, 0, 0)), tok(LANES)],
        out_shape=[jax.ShapeDtypeStruct((T, D), F32), jax.ShapeDtypeStruct((T, SUBLANES, D // SUBLANES), F32),
                   jax.ShapeDtypeStruct((T, LANES), F32)],
        compiler_params=_cparams(("parallel",)),
        name="merge",
    )(o_nsa, o_hg, mg, x2, wn, wh, wo, fn, wr1, wr2, br)


MOE_TM = 256


def _moe_kernel(te_ref, nv_ref, src_ref, srcn_ref, dst_ref, h_hbm, wg_ref, wu_ref, wd_ref, y_hbm,
                xbuf, ybuf, wgb, wub, wdb, gsem, ssem):
    i = pl.program_id(0)
    n = pl.num_programs(0)
    tm = xbuf.shape[1]
    slot = i & 1
    prv = jnp.maximum(i - 1, 0)
    nxt = jnp.minimum(i + 1, n - 1)
    nv_i = nv_ref[i]
    nv_prev = jnp.where(i > 0, nv_ref[prv], 0)
    nv_next = jnp.where(i + 1 < n, nv_ref[nxt], 0)

    def gather_copy(idx_ref, r, sl):
        return pltpu.make_async_copy(h_hbm.at[idx_ref[0, 0, r]], xbuf.at[sl, r], gsem.at[sl])

    def scatter_copy(row, r, sl):
        return pltpu.make_async_copy(ybuf.at[sl, r], y_hbm.at[row], ssem.at[sl])

    def rows(count, fn):
        groups = jnp.right_shift(count, 3)

        def body8(g, c):
            for k in range(8):
                fn(g * 8 + k)
            return c

        def body1(r, c):
            fn(r)
            return c

        lax.fori_loop(0, groups, body8, 0)
        lax.fori_loop(groups * 8, count, body1, 0)

    def wait_rows(count, copy_of):
        k = tm
        while k >= 1:
            pl.when((count & k) != 0)(lambda k=k: copy_of(k).wait())
            k //= 2

    def gather_block(k, sl):
        return pltpu.make_async_copy(h_hbm.at[pl.ds(0, k)], xbuf.at[sl, pl.ds(0, k)], gsem.at[sl])

    def scatter_block(k, sl):
        return pltpu.make_async_copy(ybuf.at[sl, pl.ds(0, k)], y_hbm.at[pl.ds(0, k)], ssem.at[sl])

    @pl.when(i == 0)
    def _():
        xbuf[...] = jnp.zeros(xbuf.shape, F32)
        rows(nv_i, lambda r: gather_copy(src_ref, r, 0).start())

    rows(nv_next, lambda r: gather_copy(srcn_ref, r, 1 - slot).start())

    @pl.when((i == 0) | (te_ref[i] != te_ref[prv]))
    def _():
        wgb[...] = wg_ref[0].astype(BF16)
        wub[...] = wu_ref[0].astype(BF16)
        wdb[...] = wd_ref[0].astype(BF16)

    @pl.when(nv_i > 0)
    def _():
        wait_rows(nv_i, lambda k: gather_block(k, slot))
        x = _from_row_tiles(xbuf, (slot,)).astype(BF16)
        hg = jnp.dot(x, wgb[...], preferred_element_type=F32)
        hu = jnp.dot(x, wub[...], preferred_element_type=F32)
        hid = (hg * jax.nn.sigmoid(hg) * hu).astype(BF16)
        _to_row_tiles(ybuf, jnp.dot(hid, wdb[...], preferred_element_type=F32), (slot,))
        rows(nv_i, lambda r: scatter_copy(dst_ref[0, 0, r], r, slot).start())

    wait_rows(nv_prev, lambda k: scatter_block(k, 1 - slot))

    @pl.when(i == n - 1)
    def _():
        wait_rows(nv_i, lambda k: scatter_block(k, slot))


def _moe(h2, route, w_gate, w_up, w_down):
    T = h2.shape[0]
    E, D, F = w_gate.shape
    tm = MOE_TM
    n_tiles = 2 * T // tm + E
    i32 = jnp.int32
    ex = jnp.concatenate([route[:, 0], route[:, 1]]).astype(i32)
    order = jnp.argsort(ex, stable=True).astype(i32)
    counts = jnp.sum(ex[:, None] == jnp.arange(E, dtype=i32)[None, :], axis=0).astype(i32)
    padded = ((counts + tm - 1) // tm) * tm
    pend = jnp.cumsum(padded)
    pstart = pend - padded
    cstart = jnp.cumsum(counts) - counts
    tile_start = jnp.arange(n_tiles, dtype=i32) * tm
    te_raw = jnp.sum(tile_start[:, None] >= pend[None, :], axis=1).astype(i32)
    active = te_raw < E
    te_c = jnp.minimum(te_raw, E - 1)
    off = tile_start - pstart[te_c]
    nv = jnp.where(active, jnp.clip(counts[te_c] - off, 0, tm), 0).astype(i32)
    n_active = jnp.sum(active.astype(i32))
    te = jnp.where(active, te_c, te_c[jnp.maximum(n_active - 1, 0)])
    rank = off[:, None] + jnp.arange(tm, dtype=i32)[None, :]
    a = order[jnp.clip(cstart[te_c][:, None] + rank, 0, 2 * T - 1)]
    valid = jnp.arange(tm, dtype=i32)[None, :] < nv[:, None]
    src = jnp.where(valid, a % T, 0).reshape(n_tiles, 1, tm)
    dst = jnp.where(valid, a, 0).reshape(n_tiles, 1, tm)

    idx_spec = lambda f: pl.BlockSpec((1, 1, tm), lambda i, te, nv: (f(i), 0, 0), memory_space=pltpu.SMEM)
    wspec = lambda r, c: pl.BlockSpec((1, r, c), lambda i, te, nv: (te[i], 0, 0))
    grid_spec = pltpu.PrefetchScalarGridSpec(
        num_scalar_prefetch=2,
        grid=(n_tiles,),
        in_specs=[idx_spec(lambda i: i), idx_spec(lambda i: jnp.minimum(i + 1, n_tiles - 1)),
                  idx_spec(lambda i: i), pl.BlockSpec(memory_space=pl.ANY),
                  wspec(D, F), wspec(D, F), wspec(F, D)],
        out_specs=pl.BlockSpec(memory_space=pl.ANY),
        scratch_shapes=[pltpu.VMEM((2, tm, SUBLANES, D // SUBLANES), F32),
                        pltpu.VMEM((2, tm, SUBLANES, D // SUBLANES), F32),
                        pltpu.VMEM((D, F), BF16), pltpu.VMEM((D, F), BF16), pltpu.VMEM((F, D), BF16),
                        pltpu.SemaphoreType.DMA((2,)), pltpu.SemaphoreType.DMA((2,))],
    )
    return pl.pallas_call(
        _moe_kernel,
        grid_spec=grid_spec,
        out_shape=jax.ShapeDtypeStruct((2 * T, SUBLANES, D // SUBLANES), F32),
        compiler_params=_cparams(("arbitrary",)),
        name="moe",
    )(te, nv, src, src, dst, h2, w_gate, w_up, w_down)


FIN_TM = 256


def _final_kernel(x1_ref, y1_ref, y2_ref, rt_ref, g_ref, o_ref):
    rt = rt_ref[...]
    x = x1_ref[...] + rt[:, 2:3] * _from_row_tiles(y1_ref) + rt[:, 3:4] * _from_row_tiles(y2_ref)
    o_ref[...] = x * lax.rsqrt(jnp.mean(x * x, axis=-1, keepdims=True) + RMS_EPS) * g_ref[...]


def _final(x1, y, route, final_norm):
    T, D = x1.shape
    nb = T // FIN_TM
    ytile = (FIN_TM, SUBLANES, D // SUBLANES)
    return pl.pallas_call(
        _final_kernel,
        grid=(nb,),
        in_specs=[pl.BlockSpec((FIN_TM, D), lambda i: (i, 0)),
                  pl.BlockSpec(ytile, lambda i: (i, 0, 0)),
                  pl.BlockSpec(ytile, lambda i: (i + nb, 0, 0)),
                  pl.BlockSpec((FIN_TM, LANES), lambda i: (i, 0)),
                  pl.BlockSpec((1, D), lambda i: (0, 0))],
        out_specs=pl.BlockSpec((FIN_TM, D), lambda i: (i, 0)),
        out_shape=jax.ShapeDtypeStruct((T, D), F32),
        compiler_params=_cparams(("parallel",)),
        name="final",
    )(x1, y, y, route, final_norm.reshape(1, D))


def kernel(x, attn_norm, w_in, w_cmp_k, w_cmp_v, cmp_pos, hg_lb_logits, hg_norm, w_br_nsa, w_br_hg, w_out,
           ffn_norm, w_grp, b_grp, w_rtr, b_rtr, w_gate, w_up, w_down, final_norm):
    B, S, D = x.shape
    assert attn_norm.shape[0] == 1, "single-layer block"
    x2 = x.reshape(B * S, D)
    q, kcin, vcin, kv, gate, hq, hf, hi, hg, mg = _in_proj(x2, attn_norm[0], w_in[0])
    qt, ksa, vst, kw, vwt, gtt = _nsa_prep(q, kv, gate, B, S)
    kc, vct = _compress(kcin, vcin, w_cmp_k[0], w_cmp_v[0], cmp_pos[0], B, S)
    o_nsa = _nsa_attn(qt, kc, vct, ksa, vst, kw, vwt, gtt, B, S).reshape(B * S, -1)
    o_hg = _hgrn(hq, hf, hi, hg, hg_lb_logits, hg_norm[0], B, S).reshape(B * S, -1)
    x1, h2, route = _merge(o_nsa, o_hg, mg, x2, w_br_nsa[0], w_br_hg[0], w_out[0], ffn_norm[0],
                           w_grp[0], b_grp[0], w_rtr[0], b_rtr[0])
    y = _moe(h2, route, w_gate[0], w_up[0], w_down[0])
    out = _final(x1, y, route, final_norm)
    return out.reshape(B, S, D)
```

```python
import functools

import numpy as np
import jax
import jax.numpy as jnp
from jax import lax
from jax.experimental import pallas as pl
from jax.experimental.pallas import tpu as pltpu

F32 = jnp.float32
BF16 = jnp.bfloat16

NSA_HEADS = 8
NSA_KV_GROUPS = 2
NSA_HEAD_DIM = 64
NSA_R = NSA_HEADS // NSA_KV_GROUPS
CMP_LEN = 32
CMP_STRIDE = 16
SLC_LEN = 64
SLC_SHIFT = 6
SLC_TOPN = 16
WINDOW = 512
Q_BLOCK = 128
ROPE_THETA = 500000.0
ROPE_DIM = NSA_HEAD_DIM // 4
HG_HEADS = 4
HG_DIM = 128
HG_CHUNK = 64
N_GROUPS = 4
EXPERTS_PER_GROUP = 8
N_EXPERTS = N_GROUPS * EXPERTS_PER_GROUP
RMS_EPS = 1e-6
NEG = -1e30
FORCE_SCORE = 1e4

LANES = 128
VMEM_LIMIT = 56 * 1024 * 1024

NT_DIMS = (((1,), (1,)), ((), ()))
TN_DIMS = (((0,), (0,)), ((), ()))


def _cparams(sem):
    return pltpu.CompilerParams(dimension_semantics=sem, vmem_limit_bytes=VMEM_LIMIT)


IN_TM = 256
IN_SEGS = (("q", 512, 512), ("kc", 128, 128), ("vc", 128, 128), ("kv", 512, 512), ("gate", 24, 128),
           ("hq", 512, 512), ("hf", 512, 512), ("hi", 512, 512), ("hg", 512, 512), ("merge", 2048, 2048))
IN_F32 = ("kc", "vc", "gate")


def _in_proj_kernel(x_ref, g_ref, w_ref, *out_refs):
    x = x_ref[...]
    h = x * lax.rsqrt(jnp.mean(x * x, axis=-1, keepdims=True) + RMS_EPS) * g_ref[...]
    h = h.astype(BF16)
    off = 0
    for (_, _, wpad), o_ref in zip(IN_SEGS, out_refs):
        y = jnp.dot(h, w_ref[:, off:off + wpad], preferred_element_type=F32)
        o_ref[...] = y.astype(o_ref.dtype)
        off += wpad


def _in_proj(x2, attn_norm, w_in):
    T, D = x2.shape
    pieces, off = [], 0
    for _, w, wpad in IN_SEGS:
        p = w_in[:, off:off + w]
        if wpad != w:
            p = jnp.pad(p, ((0, 0), (0, wpad - w)))
        pieces.append(p)
        off += w
    wcat = jnp.concatenate(pieces, axis=1).astype(BF16)
    NP = wcat.shape[1]
    out_shapes = [jax.ShapeDtypeStruct((T, wpad), F32 if name in IN_F32 else BF16)
                  for name, _, wpad in IN_SEGS]
    out_specs = [pl.BlockSpec((IN_TM, wpad), lambda i: (i, 0)) for _, _, wpad in IN_SEGS]
    return pl.pallas_call(
        _in_proj_kernel,
        grid=(T // IN_TM,),
        in_specs=[pl.BlockSpec((IN_TM, D), lambda i: (i, 0)),
                  pl.BlockSpec((1, D), lambda i: (0, 0)),
                  pl.BlockSpec((D, NP), lambda i: (0, 0))],
        out_specs=out_specs,
        out_shape=out_shapes,
        compiler_params=_cparams(("parallel",)),
        name="in_proj",
    )(x2, attn_norm.reshape(1, D), wcat)


def _rope_tables(pos):
    half = ROPE_DIM // 2
    d = np.arange(LANES) % NSA_HEAD_DIM
    inv_freq = ROPE_THETA ** (-jnp.arange(half, dtype=F32) / half)
    lane_freq = jnp.where(d < ROPE_DIM, inv_freq[d % half], 0.0)
    ang = pos.astype(F32)[:, None] * lane_freq[None, :]
    cos, sin = jnp.cos(ang), jnp.sin(ang)
    s1 = jnp.where(d < half, -sin, 0.0)
    s2 = jnp.where((d >= half) & (d < ROPE_DIM), sin, 0.0)
    return cos, s1, s2


def _rope(x, c, s1, s2):
    half = ROPE_DIM // 2
    return x * c + pltpu.roll(x, LANES - half, 1) * s1 + pltpu.roll(x, half, 1) * s2


PREP_TS = 512


GATE_ROWS = 16
LOG2E = 1.4426950408889634


def _prep_kernel(q_ref, kv_ref, gate_ref, c_ref, s1_ref, s2_ref, nege_ref,
                 qt_ref, ksa_ref, vst_ref, kw_ref, vwt_ref, gtt_ref):
    c, s1, s2 = c_ref[...], s1_ref[...], s2_ref[...]
    lane = lax.broadcasted_iota(jnp.int32, c.shape, 1)
    lo = lane < NSA_HEAD_DIM
    dk, QB = NSA_HEAD_DIM, Q_BLOCK

    def split(x):
        return (jnp.where(lo, x, 0.0), jnp.where(lo, pltpu.roll(x, dk, 1), 0.0))

    scale = (dk ** -0.5) * LOG2E
    for blk in range(NSA_HEADS // 2):
        x = q_ref[:, blk * LANES:(blk + 1) * LANES].astype(F32)
        for hh, part in enumerate(split(_rope(x, c, s1, s2) * scale)):
            h = 2 * blk + hh
            g, r = h // NSA_R, h % NSA_R
            pt = part.T.astype(BF16)
            for i in range(PREP_TS // QB):
                qt_ref[0, g, i, :, r * QB:(r + 1) * QB] = pt[:, i * QB:(i + 1) * QB]
    nege = nege_ref[...]
    for g, part in enumerate(split(_rope(kv_ref[:, 0:LANES].astype(F32), c, s1, s2))):
        ksa_ref[0, g] = jnp.concatenate([part.astype(BF16), nege], axis=1)
    for g, part in enumerate(split(_rope(kv_ref[:, 2 * LANES:3 * LANES].astype(F32), c, s1, s2))):
        kw_ref[0, g] = part.astype(BF16)
    for blk, ref in ((1, vst_ref), (3, vwt_ref)):
        for g, part in enumerate(split(kv_ref[:, blk * LANES:(blk + 1) * LANES].astype(F32))):
            ref[0, g] = part.T[0:dk, :].astype(BF16)
    sg = jax.nn.sigmoid(gate_ref[...])
    gtt_ref[0, 0] = sg.T[0:GATE_ROWS, :]
    gtt_ref[0, 1] = pltpu.roll(sg, LANES - 3 * NSA_R, 1).T[0:GATE_ROWS, :]


def _nsa_prep(q, kv, gate, B, S):
    G, dk, QB = NSA_KV_GROUPS, NSA_HEAD_DIM, Q_BLOCK
    c, s1, s2 = _rope_tables(jnp.arange(S))
    key = np.arange(S)
    nege = jnp.asarray(np.where(key[:, None] // SLC_LEN == np.arange(LANES)[None, :], NEG, 0.0), BF16)
    nj = S // PREP_TS
    tok = lambda w: pl.BlockSpec((PREP_TS, w), lambda b, j: (b * nj + j, 0))
    tab = pl.BlockSpec((PREP_TS, LANES), lambda b, j: (j, 0))
    rows_out = lambda w, dt: (jax.ShapeDtypeStruct((B, G, S, w), dt),
                              pl.BlockSpec((1, G, PREP_TS, w), lambda b, j: (b, 0, j, 0)))
    cols_out = lambda n, dt: (jax.ShapeDtypeStruct((B, G, n, S), dt),
                              pl.BlockSpec((1, G, n, PREP_TS), lambda b, j: (b, 0, 0, j)))
    qt_out = (jax.ShapeDtypeStruct((B, G, S // QB, LANES, NSA_R * QB), BF16),
              pl.BlockSpec((1, G, PREP_TS // QB, LANES, NSA_R * QB), lambda b, j: (b, 0, j, 0, 0)))
    outs = [qt_out, rows_out(2 * LANES, BF16), cols_out(dk, BF16), rows_out(LANES, BF16),
            cols_out(dk, BF16), cols_out(GATE_ROWS, F32)]
    return pl.pallas_call(
        _prep_kernel,
        grid=(B, nj),
        in_specs=[tok(512), tok(512), tok(LANES), tab, tab, tab, tab],
        out_specs=[o[1] for o in outs],
        out_shape=[o[0] for o in outs],
        compiler_params=_cparams(("parallel", "parallel")),
        name="nsa_prep",
    )(q, kv, gate, c, s1, s2, nege)


def _compress_kernel(tk_ref, tv_ref, wk_ref, wv_ref, pos_ref, c_ref, s1_ref, s2_ref, kc_ref, vct_ref):
    nc = c_ref.shape[0]
    dk = NSA_HEAD_DIM
    lane = lax.broadcasted_iota(jnp.int32, (nc, LANES), 1)
    lo = lane < dk

    def comp(t_ref, w_ref):
        a = jnp.zeros((nc, LANES), F32)
        b = jnp.zeros((nc, LANES), F32)
        c0 = jnp.zeros((SUBLANES, LANES), F32)
        for j in range(CMP_STRIDE):
            u = t_ref[pl.ds(j, nc, stride=CMP_STRIDE), :].astype(BF16)
            a = a + jnp.dot(u, w_ref[j], preferred_element_type=F32)
            b = b + jnp.dot(u, w_ref[CMP_STRIDE + j], preferred_element_type=F32)
        for l in range(CMP_LEN):
            c0 = c0 + jnp.dot(pos_ref[l], w_ref[l], preferred_element_type=F32)
        return a + pltpu.roll(b, nc - 1, 0) + c0[0:1, :]

    def split(x):
        return (jnp.where(lo, x, 0.0), jnp.where(lo, pltpu.roll(x, dk, 1), 0.0))

    for g, part in enumerate(split(_rope(comp(tk_ref, wk_ref), c_ref[...], s1_ref[...], s2_ref[...]))):
        kc_ref[0, g] = part.astype(BF16)
    for g, part in enumerate(split(comp(tv_ref, wv_ref))):
        vct_ref[0, g] = part.T[0:dk, :].astype(BF16)


def _compress(kcin, vcin, w_cmp_k, w_cmp_v, cmp_pos, B, S):
    G, dk = NSA_KV_GROUPS, NSA_HEAD_DIM
    nc = S // CMP_STRIDE

    def block_diag(w):
        w3 = w.reshape(CMP_LEN, dk, dk)
        z = jnp.zeros_like(w3)
        return jnp.concatenate([jnp.concatenate([w3, z], axis=2),
                                jnp.concatenate([z, w3], axis=2)], axis=1).astype(BF16)

    pos = jnp.concatenate([cmp_pos, cmp_pos], axis=1)
    pos = jnp.broadcast_to(pos[:, None, :], (CMP_LEN, SUBLANES, G * dk)).astype(BF16)
    c, s1, s2 = _rope_tables(jnp.arange(nc) * CMP_STRIDE)
    tspec = pl.BlockSpec((S, LANES), lambda b: (b, 0))
    wspec = pl.BlockSpec((CMP_LEN, LANES, LANES), lambda b: (0, 0, 0))
    tab = pl.BlockSpec((nc, LANES), lambda b: (0, 0))
    return pl.pallas_call(
        _compress_kernel,
        grid=(B,),
        in_specs=[tspec, tspec, wspec, wspec,
                  pl.BlockSpec((CMP_LEN, SUBLANES, LANES), lambda b: (0, 0, 0)), tab, tab, tab],
        out_specs=[pl.BlockSpec((1, G, nc, LANES), lambda b: (b, 0, 0, 0)),
                   pl.BlockSpec((1, G, dk, nc), lambda b: (b, 0, 0, 0))],
        out_shape=[jax.ShapeDtypeStruct((B, G, nc, LANES), BF16),
                   jax.ShapeDtypeStruct((B, G, dk, nc), BF16)],
        compiler_params=_cparams(("parallel",)),
        name="compress",
    )(kcin, vcin, block_diag(w_cmp_k), block_diag(w_cmp_v), pos, c, s1, s2)


CMP_STRIDE_SHIFT = 4
CMP_ROWS = 128
CMP_ROWS_SHIFT = 7
SEL_KC = 256
SEL_KC_SHIFT = 8
SEL_UC = 2
SEL_UC_SHIFT = 1
WIN_KEYS = WINDOW + Q_BLOCK


def _col_reduce(x, op, fin):
    n = x.shape[0]
    while (n // 2) % 8 == 0 and n > 8:
        n //= 2
        x = op(x[:n], x[n:])
    return fin(x, axis=0, keepdims=True)


def _col_max(x):
    return _col_reduce(x, jnp.maximum, jnp.max)


def _col_min(x):
    return _col_reduce(x, jnp.minimum, jnp.min)


def _col_sum(x):
    return _col_reduce(x, jnp.add, jnp.sum)


def _softmax_cols(s):
    e = jnp.exp2(s - _col_max(s))
    return e, _col_sum(e)


def _nsa_kernel(qt_ref, kc_ref, vct_ref, ksa_ref, vst_ref, kw_ref, vwt_ref, gtt_ref, ovt_ref,
                o_ref, rhs_sc, m0_sc, m1_sc, l0_sc, l1_sc, a0_sc, a1_sc, sa_sc, sb_sc, ocmp_sc, imp_sc,
                *, n_sel):
    m_scs, l_scs, acc_scs = (m0_sc, m1_sc), (l0_sc, l1_sc), (a0_sc, a1_sc)
    s_bufs = (sa_sc, sb_sc)
    R, QB = NSA_R, Q_BLOCK
    cols = R * QB
    i = pl.program_id(2)
    t0 = i * QB
    qt = qt_ref[0, 0, 0]

    def per_head(x):
        return jnp.concatenate([x] * (cols // x.shape[1]), axis=1)

    def rc(n):
        return (lax.broadcasted_iota(jnp.int32, (n, QB), 0),
                t0 + lax.broadcasted_iota(jnp.int32, (n, QB), 1))

    nc = kc_ref.shape[2]

    def cmp_branch(nr):
        sc = jnp.dot(kc_ref[0, 0, 0:nr, :], qt, preferred_element_type=F32)
        n_idx, tq_c = rc(nr)
        cvis = (n_idx * CMP_STRIDE + (CMP_LEN - 1)) <= tq_c
        ec, lc = _softmax_cols(sc + per_head(jnp.where(cvis, 0.0, NEG)))
        pc = ec * per_head(jnp.where(cvis, 1.0, 0.0)) * (1.0 / lc)
        ocmp_sc[...] = jnp.dot(vct_ref[0, 0, :, 0:nr], pc.astype(BF16), preferred_element_type=F32)
        psum = pc[:, 0:QB] + pc[:, QB:2 * QB] + pc[:, 2 * QB:3 * QB] + pc[:, 3 * QB:4 * QB]
        p_hi = psum.astype(BF16)
        p_lo = (psum - p_hi.astype(F32)).astype(BF16)
        ovt = ovt_ref[:, 0:nr]
        imp_sc[...] = (jnp.dot(ovt, p_hi, preferred_element_type=F32) +
                       jnp.dot(ovt, p_lo, preferred_element_type=F32))

    n_var = nc // CMP_ROWS
    last_vis = jnp.right_shift(t0 + QB - CMP_LEN, CMP_STRIDE_SHIFT)
    var = jnp.minimum(jnp.right_shift(jnp.maximum(last_vis, 0), CMP_ROWS_SHIFT), n_var - 1)
    for k in range(n_var):
        pl.when(var == k)(functools.partial(cmp_branch, (k + 1) * CMP_ROWS))
    o_cmp = ocmp_sc[...]
    imp = imp_sc[...]

    start = pl.multiple_of(jnp.maximum(t0 - WINDOW, 0), QB)
    sw = jnp.dot(kw_ref[0, 0, pl.ds(start, WIN_KEYS), :], qt, preferred_element_type=F32)

    j = lax.broadcasted_iota(jnp.int32, (LANES, QB), 0)
    tq = t0 + lax.broadcasted_iota(jnp.int32, (LANES, QB), 1)
    force = (j == jnp.right_shift(tq, SLC_SHIFT)) | (j == 0)
    valid = (j * SLC_LEN) <= tq
    score = jnp.where(force, -jnp.inf, jnp.where(valid, imp, -1.0))
    work = jnp.where(j < n_sel, score, -jnp.inf)
    jf = j.astype(F32)
    for _ in range(min(SLC_TOPN, n_sel) - 2):
        mx = _col_max(work)
        idx = _col_min(jnp.where(work == mx, jf, float(LANES)))
        work = jnp.where(jf == idx, -jnp.inf, work)
    notsel = jnp.where(work == -jnp.inf, 0.0, 1.0)

    k_idx, tq_w = rc(WIN_KEYS)
    rel = tq_w - (start + k_idx)
    pw, lw = _softmax_cols(sw + per_head(jnp.where((rel >= 0) & (rel < WINDOW), 0.0, NEG)))
    o_win = jnp.dot(vwt_ref[0, 0, :, pl.ds(start, WIN_KEYS)], pw.astype(BF16),
                    preferred_element_type=F32) * (1.0 / lw)

    rhs_sc[0:LANES, :] = qt
    rhs_sc[LANES:2 * LANES, :] = jnp.concatenate([notsel.astype(BF16)] * R, axis=1)
    halves = tuple(zip(m_scs, l_scs, acc_scs))
    hw = cols // len(halves)
    for m_sc, l_sc, acc_sc in halves:
        m_sc[...] = jnp.full(m_sc.shape, NEG, F32)
        l_sc[...] = jnp.zeros(l_sc.shape, F32)
        acc_sc[...] = jnp.zeros(acc_sc.shape, F32)

    def sel_scores(chunk_ids):
        kas = [ksa_ref[0, 0, pl.ds(pl.multiple_of(c * SEL_KC, SEL_KC), SEL_KC), :] for c in chunk_ids]
        return [[jnp.dot(ka, rhs_sc[:, hp * hw:(hp + 1) * hw], preferred_element_type=F32)
                 for ka in kas] for hp in range(len(halves))]

    def sel_update(chunk_ids, scores, diagonal):
        k0s = [pl.multiple_of(c * SEL_KC, SEL_KC) for c in chunk_ids]
        vts = [vst_ref[0, 0, :, pl.ds(k0, SEL_KC)] for k0 in k0s]
        for hp, (m_sc, l_sc, acc_sc) in enumerate(halves):
            ss = scores[hp]
            if diagonal:
                k_idx, tq_s = rc(SEL_KC)
                ss = [s + jnp.concatenate([jnp.where(k0 + k_idx <= tq_s, 0.0, NEG)] * (hw // QB), axis=1)
                      for k0, s in zip(k0s, ss)]
            m_old = m_sc[...]
            m_new = functools.reduce(jnp.maximum, [_col_max(s) for s in ss], m_old)
            alpha = jnp.exp2(m_old - m_new)
            ps = [jnp.exp2(s - m_new) for s in ss]
            l_sc[...] = alpha * l_sc[...] + functools.reduce(jnp.add, [_col_sum(p) for p in ps])
            pv = [jnp.dot(vt, p.astype(BF16), preferred_element_type=F32) for vt, p in zip(vts, ps)]
            acc_sc[...] = alpha * acc_sc[...] + functools.reduce(jnp.add, pv)
            m_sc[...] = m_new

    n_units = jnp.right_shift(t0, SEL_KC_SHIFT + SEL_UC_SHIFT)
    nh = len(halves)

    def unit(u):
        return [SEL_UC * u + c for c in range(SEL_UC)]

    def put_scores(buf, scores):
        for hp in range(nh):
            for c in range(SEL_UC):
                buf[hp * SEL_UC + c] = scores[hp][c]

    def get_scores(buf):
        return [[buf[hp * SEL_UC + c] for c in range(SEL_UC)] for hp in range(nh)]

    put_scores(s_bufs[0], sel_scores(unit(0)))

    def sel_step(u, carry):
        for parity in range(2):
            @pl.when((u & 1) == parity)
            def _():
                put_scores(s_bufs[1 - parity], sel_scores(unit(u + 1)))
                sel_update(unit(u), get_scores(s_bufs[parity]), False)
        return carry

    lax.fori_loop(0, n_units, sel_step, 0)
    for parity in range(2):
        @pl.when((n_units & 1) == parity)
        def _():
            sel_update(unit(n_units), get_scores(s_bufs[parity]), True)
    o_slc = jnp.concatenate([a[...] * (1.0 / l[...]) for _, l, a in halves], axis=1)

    gtt = gtt_ref[0, 0]
    outs = []
    for r in range(R):
        cs = slice(r * QB, (r + 1) * QB)
        outs.append(gtt[3 * r:3 * r + 1, :] * o_cmp[:, cs] + gtt[3 * r + 1:3 * r + 2, :] * o_slc[:, cs] +
                    gtt[3 * r + 2:3 * r + 3, :] * o_win[:, cs])
    o_ref[0] = jnp.concatenate(outs, axis=0).T.astype(o_ref.dtype)


def _nsa_attn(qt, kc, vct, ksa, vst, kw, vwt, gtt, B, S):
    G, R, QB, dk = NSA_KV_GROUPS, NSA_R, Q_BLOCK, NSA_HEAD_DIM
    nc = S // CMP_STRIDE
    n_sel = S // SLC_LEN
    assert n_sel <= LANES and n_sel >= SLC_TOPN and S % (SEL_UC * SEL_KC) == 0 and S >= WIN_KEYS
    assert SEL_KC == 1 << SEL_KC_SHIFT and SEL_UC == 1 << SEL_UC_SHIFT and SEL_KC % QB == 0
    assert nc % CMP_ROWS == 0 and CMP_ROWS == 1 << CMP_ROWS_SHIFT and CMP_STRIDE == 1 << CMP_STRIDE_SHIFT
    cs = np.arange(nc)[None, :] * CMP_STRIDE
    ss = np.arange(LANES)[:, None] * SLC_LEN
    ovm = (cs < ss + SLC_LEN) & (cs + CMP_LEN - 1 >= ss) & (np.arange(LANES)[:, None] < n_sel) \
        & (np.arange(nc)[None, :] < nc - 1)
    ovt = jnp.asarray(ovm, BF16)
    rows_in = lambda n, w: pl.BlockSpec((1, 1, n, w), lambda b, g, i: (b, g, 0, 0))
    cols = R * QB
    return pl.pallas_call(
        functools.partial(_nsa_kernel, n_sel=n_sel),
        grid=(B, G, S // QB),
        in_specs=[pl.BlockSpec((1, 1, 1, LANES, cols), lambda b, g, i: (b, g, i, 0, 0)),
                  rows_in(nc, LANES), rows_in(dk, nc), rows_in(S, 2 * LANES), rows_in(dk, S),
                  rows_in(S, LANES), rows_in(dk, S),
                  pl.BlockSpec((1, 1, GATE_ROWS, QB), lambda b, g, i: (b, g, 0, i)),
                  pl.BlockSpec((LANES, nc), lambda b, g, i: (0, 0))],
        out_specs=pl.BlockSpec((1, QB, R * dk), lambda b, g, i: (b, i, g)),
        out_shape=jax.ShapeDtypeStruct((B, S, NSA_HEADS * dk), BF16),
        scratch_shapes=[pltpu.VMEM((2 * LANES, cols), BF16)] +
                       [pltpu.VMEM((1, cols // 2), F32)] * 4 + [pltpu.VMEM((dk, cols // 2), F32)] * 2 +
                       [pltpu.VMEM((2 * SEL_UC, SEL_KC, cols // 2), F32)] * 2 +
                       [pltpu.VMEM((dk, cols), F32), pltpu.VMEM((LANES, QB), F32)],
        compiler_params=_cparams(("parallel", "parallel", "arbitrary")),
        name="nsa_attn",
    )(qt, kc, vct, ksa, vst, kw, vwt, gtt, ovt)


HG_TB = 256
HG_LEVELS = 6


def _hgrn_consts():
    C = HG_CHUNK
    M = np.zeros((HG_LEVELS + 2, C, C), np.float32)
    RM = np.zeros((HG_LEVELS, C, 1), np.float32)
    BM = np.zeros((HG_LEVELS + 1, C, C), np.float32)
    for lv in range(HG_LEVELS):
        m = 1 << lv
        for t in range(C):
            bs = (t // (2 * m)) * (2 * m)
            mid = bs + m - 1
            if t - bs >= m:
                RM[lv, t, 0] = 1.0
                M[lv, t, mid + 1:t + 1] = 1.0
            else:
                M[lv, t, t + 1:mid + 1] = 1.0
            BM[lv, t, bs:bs + 2 * m] = 1.0
    BM[HG_LEVELS] = np.eye(C, dtype=np.float32)
    M[HG_LEVELS] = np.tril(np.ones((C, C), np.float32))
    M[HG_LEVELS + 1] = np.triu(np.ones((C, C), np.float32), 1)
    mm = M.reshape(-1, C)
    mm3 = np.concatenate([mm, mm, mm], axis=1)
    rm = np.broadcast_to(RM, (HG_LEVELS, C, HG_HEADS * HG_DIM)).copy()
    return jnp.asarray(mm3, BF16), jnp.asarray(rm, F32), jnp.asarray(BM, F32)


def _hgrn_kernel(q_ref, f_ref, i_ref, g_ref, lbl_ref, gn_ref, mm_ref, rm_ref, bm_ref, o_ref, st_sc):
    C, D = HG_CHUNK, HG_DIM

    @pl.when(pl.program_id(1) == 0)
    def _():
        st_sc[...] = jnp.zeros(st_sc.shape, F32)

    lbl = lbl_ref[...]
    mxl = jnp.max(lbl, axis=0, keepdims=True)
    el = jnp.exp(lbl - mxl)
    lb_all = el[0:1, :] / jnp.sum(el, axis=0, keepdims=True)
    mm = mm_ref[...]
    gn = gn_ref[...]
    heads = [slice(h * D, (h + 1) * D) for h in range(HG_HEADS)]
    nt = lambda a, b: lax.dot_general(a, b, NT_DIMS, preferred_element_type=F32)
    for ch in range(HG_TB // C):
        rs = slice(ch * C, (ch + 1) * C)
        q = q_ref[0, rs, :].astype(F32)
        v = i_ref[0, rs, :]
        gg = g_ref[0, rs, :].astype(F32)
        f = lb_all + (1.0 - lb_all) * jax.nn.sigmoid(f_ref[0, rs, :].astype(F32))
        lg = jnp.log(f)
        kk = 1.0 - f
        g1 = lg.astype(BF16)
        r1 = lg - g1.astype(F32)
        g2 = r1.astype(BF16)
        g3 = (r1 - g2.astype(F32)).astype(BF16)
        ez = jnp.exp(jnp.dot(mm, jnp.concatenate([g1, g2, g3], axis=0), preferred_element_type=F32))
        qb, kb = q.astype(BF16), kk.astype(BF16)
        attn = [bm_ref[HG_LEVELS] * nt(qb[:, hs], kb[:, hs]) for hs in heads]
        for lv in range(HG_LEVELS):
            zl = ez[lv * C:(lv + 1) * C]
            rm = rm_ref[lv]
            ql = (q * zl * rm).astype(BF16)
            kl = (kk * zl * (1.0 - rm)).astype(BF16)
            attn = [a + bm_ref[lv] * nt(ql[:, hs], kl[:, hs]) for a, hs in zip(attn, heads)]
        zc = ez[HG_LEVELS * C:(HG_LEVELS + 1) * C]
        zs = ez[(HG_LEVELS + 1) * C:(HG_LEVELS + 2) * C]
        qz = (q * zc).astype(BF16)
        kz = (kk * zs).astype(BF16)
        sts = [st_sc[h] for h in range(HG_HEADS)]
        o_intra = [jnp.dot(a.astype(BF16), v[:, hs], preferred_element_type=F32) for a, hs in zip(attn, heads)]
        o_inter = [nt(qz[:, hs], st.astype(BF16)) for st, hs in zip(sts, heads)]
        upd = [lax.dot_general(v[:, hs], kz[:, hs], TN_DIMS, preferred_element_type=F32) for hs in heads]
        for h, hs in enumerate(heads):
            st_sc[h] = zc[C - 1:C, hs] * sts[h] + upd[h]
            o = o_intra[h] + o_inter[h]
            y = o * lax.rsqrt(jnp.mean(o * o, axis=-1, keepdims=True) + RMS_EPS) * gn
            g = gg[:, hs]
            o_ref[0, rs, hs] = (y * (g * jax.nn.sigmoid(g))).astype(o_ref.dtype)


def _hgrn(hq, hf, hi, hg, lb_logits, hg_norm, B, S):
    W = HG_HEADS * HG_DIM
    mm3, rm, bm = _hgrn_consts()
    r3 = lambda a: a.reshape(B, S, W)
    tok = pl.BlockSpec((1, HG_TB, W), lambda b, j: (b, j, 0))
    const = lambda a: pl.BlockSpec(a.shape, lambda b, j: (0,) * a.ndim)
    gn = hg_norm.reshape(1, HG_DIM)
    return pl.pallas_call(
        _hgrn_kernel,
        grid=(B, S // HG_TB),
        in_specs=[tok, tok, tok, tok, const(lb_logits), const(gn), const(mm3), const(rm), const(bm)],
        out_specs=tok,
        out_shape=jax.ShapeDtypeStruct((B, S, W), BF16),
        scratch_shapes=[pltpu.VMEM((HG_HEADS, HG_DIM, HG_DIM), F32)],
        compiler_params=_cparams(("parallel", "arbitrary")),
        name="hgrn2",
    )(r3(hq), r3(hf), r3(hi), r3(hg), lb_logits, gn, mm3, rm, bm)


MG_TM = 256


SUBLANES = 8


def _to_row_tiles(ref, x, lead=()):
    for s in range(SUBLANES):
        ref[lead + (slice(None), s, slice(None))] = x[:, s * LANES:(s + 1) * LANES]


def _from_row_tiles(ref, lead=()):
    return jnp.concatenate([ref[lead + (slice(None), s, slice(None))] for s in range(SUBLANES)], axis=1)


def _first_lane(cond, lane_f):
    return jnp.min(jnp.where(cond, lane_f, float(LANES)), axis=1, keepdims=True)


def _merge_kernel(on_ref, oh_ref, mg_ref, x_ref, wn_ref, wh_ref, wo_ref, fn_ref, wr1_ref, wr2_ref, br_ref,
                  x1_ref, h2_ref, rt_ref):
    D = x_ref.shape[1]
    a = jnp.dot(on_ref[...], wn_ref[...], preferred_element_type=F32)
    b = jnp.dot(oh_ref[...], wh_ref[...], preferred_element_type=F32)
    mixed = (jax.nn.sigmoid(mg_ref[:, 0:D].astype(F32)) * a +
             jax.nn.sigmoid(mg_ref[:, D:2 * D].astype(F32)) * b)
    x1 = x_ref[...] + jnp.dot(mixed.astype(BF16), wo_ref[...], preferred_element_type=F32)
    x1_ref[...] = x1
    h2 = x1 * lax.rsqrt(jnp.mean(x1 * x1, axis=-1, keepdims=True) + RMS_EPS) * fn_ref[...]
    _to_row_tiles(h2_ref, h2)
    h_hi = h2.astype(BF16)
    h_lo = (h2 - h_hi.astype(F32)).astype(BF16)
    logits = (jnp.dot(h_lo, wr1_ref[...], preferred_element_type=F32) +
              jnp.dot(h_hi, wr2_ref[...], preferred_element_type=F32) +
              jnp.dot(h_hi, wr1_ref[...], preferred_element_type=F32)) + br_ref[...]
    lane = lax.broadcasted_iota(jnp.int32, logits.shape, 1)
    ninf = -jnp.inf
    gl = jnp.where(lane < N_GROUPS, logits, ninf)
    gmax = jnp.max(gl, axis=1, keepdims=True)
    lane_f = lane.astype(F32)
    gsel = _first_lane(gl == gmax, lane_f)
    gprob = 1.0 / jnp.sum(jnp.exp(gl - gmax), axis=1, keepdims=True)
    lo = N_GROUPS + gsel * EXPERTS_PER_GROUP
    el = jnp.where((lane_f >= lo) & (lane_f < lo + EXPERTS_PER_GROUP), logits, ninf)
    v1 = jnp.max(el, axis=1, keepdims=True)
    i1 = _first_lane(el == v1, lane_f)
    el2 = jnp.where(lane_f == i1, ninf, el)
    v2 = jnp.max(el2, axis=1, keepdims=True)
    i2 = _first_lane(el2 == v2, lane_f)
    e2 = jnp.exp(v2 - v1)
    w1 = gprob / (1.0 + e2)
    w2 = gprob * e2 / (1.0 + e2)
    rt = jnp.where(lane == 0, i1 - N_GROUPS,
                   jnp.where(lane == 1, i2 - N_GROUPS,
                             jnp.where(lane == 2, w1, jnp.where(lane == 3, w2, 0.0))))
    rt_ref[...] = rt


def _merge(o_nsa, o_hg, mg, x2, w_br_nsa, w_br_hg, w_out, ffn_norm, w_grp, b_grp, w_rtr, b_rtr):
    T, D = x2.shape
    nr = N_GROUPS + N_EXPERTS
    wr = jnp.pad(jnp.concatenate([w_grp, w_rtr], axis=1), ((0, 0), (0, LANES - nr)))
    wr1 = wr.astype(BF16)
    wr2 = (wr - wr1.astype(F32)).astype(BF16)
    br = jnp.pad(jnp.concatenate([b_grp, b_rtr]), (0, LANES - nr)).reshape(1, LANES)
    tok = lambda w: pl.BlockSpec((MG_TM, w), lambda i: (i, 0))
    const = lambda a: pl.BlockSpec(a.shape, lambda i: (0,) * a.ndim)
    wn, wh, wo = w_br_nsa.astype(BF16), w_br_hg.astype(BF16), w_out.astype(BF16)
    fn = ffn_norm.reshape(1, D)
    return pl.pallas_call(
        _merge_kernel,
        grid=(T // MG_TM,),
        in_specs=[tok(o_nsa.shape[1]), tok(o_hg.shape[1]), tok(2 * D), tok(D),
                  const(wn), const(wh), const(wo), const(fn), const(wr1), const(wr2), const(br)],
        out_specs=[tok(D), pl.BlockSpec((MG_TM, SUBLANES, D // SUBLANES), lambda i: (i, 0, 0)), tok(LANES)],
        out_shape=[jax.ShapeDtypeStruct((T, D), F32), jax.ShapeDtypeStruct((T, SUBLANES, D // SUBLANES), F32),
                   jax.ShapeDtypeStruct((T, LANES), F32)],
        compiler_params=_cparams(("parallel",)),
        name="merge",
    )(o_nsa, o_hg, mg, x2, wn, wh, wo, fn, wr1, wr2, br)


MOE_TM = 256


def _moe_kernel(te_ref, nv_ref, src_ref, srcn_ref, dst_ref, h_hbm, wg_ref, wu_ref, wd_ref, y_hbm,
                xbuf, ybuf, wgb, wub, wdb, gsem, ssem):
    i = pl.program_id(0)
    n = pl.num_programs(0)
    tm = xbuf.shape[1]
    slot = i & 1
    prv = jnp.maximum(i - 1, 0)
    nv_i = nv_ref[i]
    nv_prev = jnp.where(i > 0, nv_ref[prv], 0)

    def gather_copy(idx_ref, r, sl):
        return pltpu.make_async_copy(h_hbm.at[idx_ref[0, 0, r]], xbuf.at[sl, r], gsem.at[sl])

    def scatter_copy(row, r, sl):
        return pltpu.make_async_copy(ybuf.at[sl, r], y_hbm.at[row], ssem.at[sl])

    def rows(count, fn):
        groups = jnp.right_shift(count, 3)

        def body8(g, c):
            for k in range(8):
                fn(g * 8 + k)
            return c

        def body1(r, c):
            fn(r)
            return c

        lax.fori_loop(0, groups, body8, 0)
        lax.fori_loop(groups * 8, count, body1, 0)

    def wait_rows(count, copy_of):
        k = tm
        while k >= 1:
            pl.when((count & k) != 0)(lambda k=k: copy_of(k).wait())
            k //= 2

    def gather_block(k, sl):
        return pltpu.make_async_copy(h_hbm.at[pl.ds(0, k)], xbuf.at[sl, pl.ds(0, k)], gsem.at[sl])

    def scatter_block(k, sl):
        return pltpu.make_async_copy(ybuf.at[sl, pl.ds(0, k)], y_hbm.at[pl.ds(0, k)], ssem.at[sl])

    @pl.when((i == 0) & (nv_i > 0))
    def _():
        rows(tm, lambda r: gather_copy(src_ref, r, 0).start())

    @pl.when((i == 0) | (te_ref[i] != te_ref[prv]))
    def _():
        wgb[...] = wg_ref[0].astype(BF16)
        wub[...] = wu_ref[0].astype(BF16)
        wdb[...] = wd_ref[0].astype(BF16)

    @pl.when(nv_i > 0)
    def _():
        gather_block(tm, slot).wait()
        for r in range(tm):
            gather_copy(srcn_ref, r, 1 - slot).start()
        x = _from_row_tiles(xbuf, (slot,)).astype(BF16)
        hg = jnp.dot(x, wgb[...], preferred_element_type=F32)
        hu = jnp.dot(x, wub[...], preferred_element_type=F32)
        hid = (hg * jax.nn.sigmoid(hg) * hu).astype(BF16)
        _to_row_tiles(ybuf, jnp.dot(hid, wdb[...], preferred_element_type=F32), (slot,))
        rows(nv_i, lambda r: scatter_copy(dst_ref[0, 0, r], r, slot).start())

    @pl.when((nv_i == 0) & (nv_prev > 0))
    def _():
        gather_block(tm, slot).wait()

    wait_rows(nv_prev, lambda k: scatter_block(k, 1 - slot))

    @pl.when(i == n - 1)
    def _():
        wait_rows(nv_i, lambda k: scatter_block(k, slot))
        pl.when(nv_i > 0)(lambda: gather_block(tm, 1 - slot).wait())


def _moe(h2, route, w_gate, w_up, w_down):
    T = h2.shape[0]
    E, D, F = w_gate.shape
    tm = MOE_TM
    n_tiles = 2 * T // tm + E
    i32 = jnp.int32
    ex = jnp.concatenate([route[:, 0], route[:, 1]]).astype(i32)
    order = jnp.argsort(ex, stable=True).astype(i32)
    counts = jnp.sum(ex[:, None] == jnp.arange(E, dtype=i32)[None, :], axis=0).astype(i32)
    padded = ((counts + tm - 1) // tm) * tm
    pend = jnp.cumsum(padded)
    pstart = pend - padded
    cstart = jnp.cumsum(counts) - counts
    tile_start = jnp.arange(n_tiles, dtype=i32) * tm
    te_raw = jnp.sum(tile_start[:, None] >= pend[None, :], axis=1).astype(i32)
    active = te_raw < E
    te_c = jnp.minimum(te_raw, E - 1)
    off = tile_start - pstart[te_c]
    nv = jnp.where(active, jnp.clip(counts[te_c] - off, 0, tm), 0).astype(i32)
    n_active = jnp.sum(active.astype(i32))
    te = jnp.where(active, te_c, te_c[jnp.maximum(n_active - 1, 0)])
    rank = off[:, None] + jnp.arange(tm, dtype=i32)[None, :]
    a = order[jnp.clip(cstart[te_c][:, None] + rank, 0, 2 * T - 1)]
    valid = jnp.arange(tm, dtype=i32)[None, :] < nv[:, None]
    src = jnp.where(valid, a % T, 0).reshape(n_tiles, 1, tm)
    dst = jnp.where(valid, a, 0).reshape(n_tiles, 1, tm)

    idx_spec = lambda f: pl.BlockSpec((1, 1, tm), lambda i, te, nv: (f(i), 0, 0), memory_space=pltpu.SMEM)
    wspec = lambda r, c: pl.BlockSpec((1, r, c), lambda i, te, nv: (te[i], 0, 0))
    grid_spec = pltpu.PrefetchScalarGridSpec(
        num_scalar_prefetch=2,
        grid=(n_tiles,),
        in_specs=[idx_spec(lambda i: i), idx_spec(lambda i: jnp.minimum(i + 1, n_tiles - 1)),
                  idx_spec(lambda i: i), pl.BlockSpec(memory_space=pl.ANY),
                  wspec(D, F), wspec(D, F), wspec(F, D)],
        out_specs=pl.BlockSpec(memory_space=pl.ANY),
        scratch_shapes=[pltpu.VMEM((2, tm, SUBLANES, D // SUBLANES), F32),
                        pltpu.VMEM((2, tm, SUBLANES, D // SUBLANES), F32),
                        pltpu.VMEM((D, F), BF16), pltpu.VMEM((D, F), BF16), pltpu.VMEM((F, D), BF16),
                        pltpu.SemaphoreType.DMA((2,)), pltpu.SemaphoreType.DMA((2,))],
    )
    return pl.pallas_call(
        _moe_kernel,
        grid_spec=grid_spec,
        out_shape=jax.ShapeDtypeStruct((2 * T, SUBLANES, D // SUBLANES), F32),
        compiler_params=_cparams(("arbitrary",)),
        name="moe",
    )(te, nv, src, src, dst, h2, w_gate, w_up, w_down)


FIN_TM = 256


def _final_kernel(x1_ref, y1_ref, y2_ref, rt_ref, g_ref, o_ref):
    rt = rt_ref[...]
    x = x1_ref[...] + rt[:, 2:3] * _from_row_tiles(y1_ref) + rt[:, 3:4] * _from_row_tiles(y2_ref)
    o_ref[...] = x * lax.rsqrt(jnp.mean(x * x, axis=-1, keepdims=True) + RMS_EPS) * g_ref[...]


def _final(x1, y, route, final_norm):
    T, D = x1.shape
    nb = T // FIN_TM
    ytile = (FIN_TM, SUBLANES, D // SUBLANES)
    return pl.pallas_call(
        _final_kernel,
        grid=(nb,),
        in_specs=[pl.BlockSpec((FIN_TM, D), lambda i: (i, 0)),
                  pl.BlockSpec(ytile, lambda i: (i, 0, 0)),
                  pl.BlockSpec(ytile, lambda i: (i + nb, 0, 0)),
                  pl.BlockSpec((FIN_TM, LANES), lambda i: (i, 0)),
                  pl.BlockSpec((1, D), lambda i: (0, 0))],
        out_specs=pl.BlockSpec((FIN_TM, D), lambda i: (i, 0)),
        out_shape=jax.ShapeDtypeStruct((T, D), F32),
        compiler_params=_cparams(("parallel",)),
        name="final",
    )(x1, y, y, route, final_norm.reshape(1, D))


def kernel(x, attn_norm, w_in, w_cmp_k, w_cmp_v, cmp_pos, hg_lb_logits, hg_norm, w_br_nsa, w_br_hg, w_out,
           ffn_norm, w_grp, b_grp, w_rtr, b_rtr, w_gate, w_up, w_down, final_norm):
    B, S, D = x.shape
    assert attn_norm.shape[0] == 1, "single-layer block"
    x2 = x.reshape(B * S, D)
    q, kcin, vcin, kv, gate, hq, hf, hi, hg, mg = _in_proj(x2, attn_norm[0], w_in[0])
    qt, ksa, vst, kw, vwt, gtt = _nsa_prep(q, kv, gate, B, S)
    kc, vct = _compress(kcin, vcin, w_cmp_k[0], w_cmp_v[0], cmp_pos[0], B, S)
    o_nsa = _nsa_attn(qt, kc, vct, ksa, vst, kw, vwt, gtt, B, S).reshape(B * S, -1)
    o_hg = _hgrn(hq, hf, hi, hg, hg_lb_logits, hg_norm[0], B, S).reshape(B * S, -1)
    x1, h2, route = _merge(o_nsa, o_hg, mg, x2, w_br_nsa[0], w_br_hg[0], w_out[0], ffn_norm[0],
                           w_grp[0], b_grp[0], w_rtr[0], b_rtr[0])
    y = _moe(h2, route, w_gate[0], w_up[0], w_down[0])
    out = _final(x1, y, route, final_norm)
    return out.reshape(B, S, D)
```

```python
import functools

import numpy as np
import jax
import jax.numpy as jnp
from jax import lax
from jax.experimental import pallas as pl
from jax.experimental.pallas import tpu as pltpu

F32 = jnp.float32
BF16 = jnp.bfloat16

NSA_HEADS = 8
NSA_KV_GROUPS = 2
NSA_HEAD_DIM = 64
NSA_R = NSA_HEADS // NSA_KV_GROUPS
CMP_LEN = 32
CMP_STRIDE = 16
SLC_LEN = 64
SLC_SHIFT = 6
SLC_TOPN = 16
WINDOW = 512
Q_BLOCK = 128
ROPE_THETA = 500000.0
ROPE_DIM = NSA_HEAD_DIM // 4
HG_HEADS = 4
HG_DIM = 128
HG_CHUNK = 64
N_GROUPS = 4
EXPERTS_PER_GROUP = 8
N_EXPERTS = N_GROUPS * EXPERTS_PER_GROUP
RMS_EPS = 1e-6
NEG = -1e30
FORCE_SCORE = 1e4

LANES = 128
VMEM_LIMIT = 56 * 1024 * 1024

NT_DIMS = (((1,), (1,)), ((), ()))
TN_DIMS = (((0,), (0,)), ((), ()))


def _cparams(sem):
    return pltpu.CompilerParams(dimension_semantics=sem, vmem_limit_bytes=VMEM_LIMIT)


IN_TM = 256
IN_SEGS = (("q", 512, 512), ("kc", 128, 128), ("vc", 128, 128), ("kv", 512, 512), ("gate", 24, 128),
           ("hq", 512, 512), ("hf", 512, 512), ("hi", 512, 512), ("hg", 512, 512), ("merge", 2048, 2048))
IN_F32 = ("kc", "vc", "gate")


def _in_proj_kernel(x_ref, g_ref, w_ref, *out_refs):
    x = x_ref[...]
    h = x * lax.rsqrt(jnp.mean(x * x, axis=-1, keepdims=True) + RMS_EPS) * g_ref[...]
    h = h.astype(BF16)
    off = 0
    for (_, _, wpad), o_ref in zip(IN_SEGS, out_refs):
        y = jnp.dot(h, w_ref[:, off:off + wpad], preferred_element_type=F32)
        o_ref[...] = y.astype(o_ref.dtype)
        off += wpad


def _in_proj(x2, attn_norm, w_in):
    T, D = x2.shape
    pieces, off = [], 0
    for _, w, wpad in IN_SEGS:
        p = w_in[:, off:off + w]
        if wpad != w:
            p = jnp.pad(p, ((0, 0), (0, wpad - w)))
        pieces.append(p)
        off += w
    wcat = jnp.concatenate(pieces, axis=1).astype(BF16)
    NP = wcat.shape[1]
    out_shapes = [jax.ShapeDtypeStruct((T, wpad), F32 if name in IN_F32 else BF16)
                  for name, _, wpad in IN_SEGS]
    out_specs = [pl.BlockSpec((IN_TM, wpad), lambda i: (i, 0)) for _, _, wpad in IN_SEGS]
    return pl.pallas_call(
        _in_proj_kernel,
        grid=(T // IN_TM,),
        in_specs=[pl.BlockSpec((IN_TM, D), lambda i: (i, 0)),
                  pl.BlockSpec((1, D), lambda i: (0, 0)),
                  pl.BlockSpec((D, NP), lambda i: (0, 0))],
        out_specs=out_specs,
        out_shape=out_shapes,
        compiler_params=_cparams(("parallel",)),
        name="in_proj",
    )(x2, attn_norm.reshape(1, D), wcat)


def _rope_tables(pos):
    half = ROPE_DIM // 2
    d = np.arange(LANES) % NSA_HEAD_DIM
    inv_freq = ROPE_THETA ** (-jnp.arange(half, dtype=F32) / half)
    lane_freq = jnp.where(d < ROPE_DIM, inv_freq[d % half], 0.0)
    ang = pos.astype(F32)[:, None] * lane_freq[None, :]
    cos, sin = jnp.cos(ang), jnp.sin(ang)
    s1 = jnp.where(d < half, -sin, 0.0)
    s2 = jnp.where((d >= half) & (d < ROPE_DIM), sin, 0.0)
    return cos, s1, s2


def _rope(x, c, s1, s2):
    half = ROPE_DIM // 2
    return x * c + pltpu.roll(x, LANES - half, 1) * s1 + pltpu.roll(x, half, 1) * s2


PREP_TS = 512


GATE_ROWS = 16
LOG2E = 1.4426950408889634


def _prep_kernel(q_ref, kv_ref, gate_ref, c_ref, s1_ref, s2_ref, nege_ref,
                 qt_ref, ksa_ref, vst_ref, kw_ref, vwt_ref, gtt_ref):
    c, s1, s2 = c_ref[...], s1_ref[...], s2_ref[...]
    lane = lax.broadcasted_iota(jnp.int32, c.shape, 1)
    lo = lane < NSA_HEAD_DIM
    dk, QB = NSA_HEAD_DIM, Q_BLOCK

    def split(x):
        return (jnp.where(lo, x, 0.0), jnp.where(lo, pltpu.roll(x, dk, 1), 0.0))

    scale = (dk ** -0.5) * LOG2E
    for blk in range(NSA_HEADS // 2):
        x = q_ref[:, blk * LANES:(blk + 1) * LANES].astype(F32)
        for hh, part in enumerate(split(_rope(x, c, s1, s2) * scale)):
            h = 2 * blk + hh
            g, r = h // NSA_R, h % NSA_R
            pt = part.T.astype(BF16)
            for i in range(PREP_TS // QB):
                qt_ref[0, g, i, :, r * QB:(r + 1) * QB] = pt[:, i * QB:(i + 1) * QB]
    nege = nege_ref[...]
    for g, part in enumerate(split(_rope(kv_ref[:, 0:LANES].astype(F32), c, s1, s2))):
        ksa_ref[0, g] = jnp.concatenate([part.astype(BF16), nege], axis=1)
    for g, part in enumerate(split(_rope(kv_ref[:, 2 * LANES:3 * LANES].astype(F32), c, s1, s2))):
        kw_ref[0, g] = part.astype(BF16)
    for blk, ref in ((1, vst_ref), (3, vwt_ref)):
        for g, part in enumerate(split(kv_ref[:, blk * LANES:(blk + 1) * LANES].astype(F32))):
            ref[0, g] = part.T[0:dk, :].astype(BF16)
    sg = jax.nn.sigmoid(gate_ref[...])
    gtt_ref[0, 0] = sg.T[0:GATE_ROWS, :]
    gtt_ref[0, 1] = pltpu.roll(sg, LANES - 3 * NSA_R, 1).T[0:GATE_ROWS, :]


def _nsa_prep(q, kv, gate, B, S):
    G, dk, QB = NSA_KV_GROUPS, NSA_HEAD_DIM, Q_BLOCK
    c, s1, s2 = _rope_tables(jnp.arange(S))
    key = np.arange(S)
    nege = jnp.asarray(np.where(key[:, None] // SLC_LEN == np.arange(LANES)[None, :], NEG, 0.0), BF16)
    nj = S // PREP_TS
    tok = lambda w: pl.BlockSpec((PREP_TS, w), lambda b, j: (b * nj + j, 0))
    tab = pl.BlockSpec((PREP_TS, LANES), lambda b, j: (j, 0))
    rows_out = lambda w, dt: (jax.ShapeDtypeStruct((B, G, S, w), dt),
                              pl.BlockSpec((1, G, PREP_TS, w), lambda b, j: (b, 0, j, 0)))
    cols_out = lambda n, dt: (jax.ShapeDtypeStruct((B, G, n, S), dt),
                              pl.BlockSpec((1, G, n, PREP_TS), lambda b, j: (b, 0, 0, j)))
    qt_out = (jax.ShapeDtypeStruct((B, G, S // QB, LANES, NSA_R * QB), BF16),
              pl.BlockSpec((1, G, PREP_TS // QB, LANES, NSA_R * QB), lambda b, j: (b, 0, j, 0, 0)))
    outs = [qt_out, rows_out(2 * LANES, BF16), cols_out(dk, BF16), rows_out(LANES, BF16),
            cols_out(dk, BF16), cols_out(GATE_ROWS, F32)]
    return pl.pallas_call(
        _prep_kernel,
        grid=(B, nj),
        in_specs=[tok(512), tok(512), tok(LANES), tab, tab, tab, tab],
        out_specs=[o[1] for o in outs],
        out_shape=[o[0] for o in outs],
        compiler_params=_cparams(("parallel", "parallel")),
        name="nsa_prep",
    )(q, kv, gate, c, s1, s2, nege)


def _compress_kernel(tk_ref, tv_ref, wk_ref, wv_ref, pos_ref, c_ref, s1_ref, s2_ref, kc_ref, vct_ref):
    nc = c_ref.shape[0]
    dk = NSA_HEAD_DIM
    lane = lax.broadcasted_iota(jnp.int32, (nc, LANES), 1)
    lo = lane < dk

    def comp(t_ref, w_ref):
        a = jnp.zeros((nc, LANES), F32)
        b = jnp.zeros((nc, LANES), F32)
        c0 = jnp.zeros((SUBLANES, LANES), F32)
        for j in range(CMP_STRIDE):
            u = t_ref[pl.ds(j, nc, stride=CMP_STRIDE), :].astype(BF16)
            a = a + jnp.dot(u, w_ref[j], preferred_element_type=F32)
            b = b + jnp.dot(u, w_ref[CMP_STRIDE + j], preferred_element_type=F32)
        for l in range(CMP_LEN):
            c0 = c0 + jnp.dot(pos_ref[l], w_ref[l], preferred_element_type=F32)
        return a + pltpu.roll(b, nc - 1, 0) + c0[0:1, :]

    def split(x):
        return (jnp.where(lo, x, 0.0), jnp.where(lo, pltpu.roll(x, dk, 1), 0.0))

    for g, part in enumerate(split(_rope(comp(tk_ref, wk_ref), c_ref[...], s1_ref[...], s2_ref[...]))):
        kc_ref[0, g] = part.astype(BF16)
    for g, part in enumerate(split(comp(tv_ref, wv_ref))):
        vct_ref[0, g] = part.T[0:dk, :].astype(BF16)


def _compress(kcin, vcin, w_cmp_k, w_cmp_v, cmp_pos, B, S):
    G, dk = NSA_KV_GROUPS, NSA_HEAD_DIM
    nc = S // CMP_STRIDE

    def block_diag(w):
        w3 = w.reshape(CMP_LEN, dk, dk)
        z = jnp.zeros_like(w3)
        return jnp.concatenate([jnp.concatenate([w3, z], axis=2),
                                jnp.concatenate([z, w3], axis=2)], axis=1).astype(BF16)

    pos = jnp.concatenate([cmp_pos, cmp_pos], axis=1)
    pos = jnp.broadcast_to(pos[:, None, :], (CMP_LEN, SUBLANES, G * dk)).astype(BF16)
    c, s1, s2 = _rope_tables(jnp.arange(nc) * CMP_STRIDE)
    tspec = pl.BlockSpec((S, LANES), lambda b: (b, 0))
    wspec = pl.BlockSpec((CMP_LEN, LANES, LANES), lambda b: (0, 0, 0))
    tab = pl.BlockSpec((nc, LANES), lambda b: (0, 0))
    return pl.pallas_call(
        _compress_kernel,
        grid=(B,),
        in_specs=[tspec, tspec, wspec, wspec,
                  pl.BlockSpec((CMP_LEN, SUBLANES, LANES), lambda b: (0, 0, 0)), tab, tab, tab],
        out_specs=[pl.BlockSpec((1, G, nc, LANES), lambda b: (b, 0, 0, 0)),
                   pl.BlockSpec((1, G, dk, nc), lambda b: (b, 0, 0, 0))],
        out_shape=[jax.ShapeDtypeStruct((B, G, nc, LANES), BF16),
                   jax.ShapeDtypeStruct((B, G, dk, nc), BF16)],
        compiler_params=_cparams(("parallel",)),
        name="compress",
    )(kcin, vcin, block_diag(w_cmp_k), block_diag(w_cmp_v), pos, c, s1, s2)


CMP_STRIDE_SHIFT = 4
CMP_ROWS = 128
CMP_ROWS_SHIFT = 7
SEL_KC = 256
SEL_KC_SHIFT = 8
SEL_UC = 2
SEL_UC_SHIFT = 1
WIN_KEYS = WINDOW + Q_BLOCK


def _col_reduce(x, op, fin):
    n = x.shape[0]
    while (n // 2) % 8 == 0 and n > 8:
        n //= 2
        x = op(x[:n], x[n:])
    return fin(x, axis=0, keepdims=True)


def _col_max(x):
    return _col_reduce(x, jnp.maximum, jnp.max)


def _col_min(x):
    return _col_reduce(x, jnp.minimum, jnp.min)


def _col_sum(x):
    return _col_reduce(x, jnp.add, jnp.sum)


def _softmax_cols(s):
    e = jnp.exp2(s - _col_max(s))
    return e, _col_sum(e)


def _nsa_kernel(qt_ref, kc_ref, vct_ref, ksa_ref, vst_ref, kw_ref, vwt_ref, gtt_ref, ovt_ref,
                o_ref, rhs_sc, m0_sc, m1_sc, l0_sc, l1_sc, a0_sc, a1_sc, sa_sc, sb_sc, ocmp_sc, imp_sc,
                *, n_sel):
    m_scs, l_scs, acc_scs = (m0_sc, m1_sc), (l0_sc, l1_sc), (a0_sc, a1_sc)
    s_bufs = (sa_sc, sb_sc)
    R, QB = NSA_R, Q_BLOCK
    cols = R * QB
    i = pl.program_id(2)
    t0 = i * QB
    qt = qt_ref[0, 0, 0]

    def per_head(x):
        return jnp.concatenate([x] * (cols // x.shape[1]), axis=1)

    def rc(n):
        return (lax.broadcasted_iota(jnp.int32, (n, QB), 0),
                t0 + lax.broadcasted_iota(jnp.int32, (n, QB), 1))

    nc = kc_ref.shape[2]

    def cmp_branch(nr):
        sc = jnp.dot(kc_ref[0, 0, 0:nr, :], qt, preferred_element_type=F32)
        n_idx, tq_c = rc(nr)
        cvis = (n_idx * CMP_STRIDE + (CMP_LEN - 1)) <= tq_c
        ec, lc = _softmax_cols(sc + per_head(jnp.where(cvis, 0.0, NEG)))
        pc = ec * per_head(jnp.where(cvis, 1.0, 0.0)) * (1.0 / lc)
        ocmp_sc[...] = jnp.dot(vct_ref[0, 0, :, 0:nr], pc.astype(BF16), preferred_element_type=F32)
        psum = pc[:, 0:QB] + pc[:, QB:2 * QB] + pc[:, 2 * QB:3 * QB] + pc[:, 3 * QB:4 * QB]
        p_hi = psum.astype(BF16)
        p_lo = (psum - p_hi.astype(F32)).astype(BF16)
        ovt = ovt_ref[:, 0:nr]
        imp_sc[...] = (jnp.dot(ovt, p_hi, preferred_element_type=F32) +
                       jnp.dot(ovt, p_lo, preferred_element_type=F32))

    n_var = nc // CMP_ROWS
    last_vis = jnp.right_shift(t0 + QB - CMP_LEN, CMP_STRIDE_SHIFT)
    var = jnp.minimum(jnp.right_shift(jnp.maximum(last_vis, 0), CMP_ROWS_SHIFT), n_var - 1)
    for k in range(n_var):
        pl.when(var == k)(functools.partial(cmp_branch, (k + 1) * CMP_ROWS))
    o_cmp = ocmp_sc[...]
    imp = imp_sc[...]

    start = pl.multiple_of(jnp.maximum(t0 - WINDOW, 0), QB)
    sw = jnp.dot(kw_ref[0, 0, pl.ds(start, WIN_KEYS), :], qt, preferred_element_type=F32)

    j = lax.broadcasted_iota(jnp.int32, (LANES, QB), 0)
    tq = t0 + lax.broadcasted_iota(jnp.int32, (LANES, QB), 1)
    force = (j == jnp.right_shift(tq, SLC_SHIFT)) | (j == 0)
    valid = (j * SLC_LEN) <= tq
    score = jnp.where(force, -jnp.inf, jnp.where(valid, imp, -1.0))
    work = jnp.where(j < n_sel, score, -jnp.inf)
    jf = j.astype(F32)
    for _ in range(min(SLC_TOPN, n_sel) - 2):
        mx = _col_max(work)
        idx = _col_min(jnp.where(work == mx, jf, float(LANES)))
        work = jnp.where(jf == idx, -jnp.inf, work)
    notsel = jnp.where(work == -jnp.inf, 0.0, 1.0)

    k_idx, tq_w = rc(WIN_KEYS)
    rel = tq_w - (start + k_idx)
    pw, lw = _softmax_cols(sw + per_head(jnp.where((rel >= 0) & (rel < WINDOW), 0.0, NEG)))
    o_win = jnp.dot(vwt_ref[0, 0, :, pl.ds(start, WIN_KEYS)], pw.astype(BF16),
                    preferred_element_type=F32) * (1.0 / lw)

    rhs_sc[0:LANES, :] = qt
    rhs_sc[LANES:2 * LANES, :] = jnp.concatenate([notsel.astype(BF16)] * R, axis=1)
    halves = tuple(zip(m_scs, l_scs, acc_scs))
    hw = cols // len(halves)
    for m_sc, l_sc, acc_sc in halves:
        m_sc[...] = jnp.full(m_sc.shape, NEG, F32)
        l_sc[...] = jnp.zeros(l_sc.shape, F32)
        acc_sc[...] = jnp.zeros(acc_sc.shape, F32)

    def sel_scores(chunk_ids):
        kas = [ksa_ref[0, 0, pl.ds(pl.multiple_of(c * SEL_KC, SEL_KC), SEL_KC), :] for c in chunk_ids]
        return [[jnp.dot(ka, rhs_sc[:, hp * hw:(hp + 1) * hw], preferred_element_type=F32)
                 for ka in kas] for hp in range(len(halves))]

    def sel_update(chunk_ids, scores, diagonal):
        k0s = [pl.multiple_of(c * SEL_KC, SEL_KC) for c in chunk_ids]
        vts = [vst_ref[0, 0, :, pl.ds(k0, SEL_KC)] for k0 in k0s]
        for hp, (m_sc, l_sc, acc_sc) in enumerate(halves):
            ss = scores[hp]
            if diagonal:
                k_idx, tq_s = rc(SEL_KC)
                ss = [s + jnp.concatenate([jnp.where(k0 + k_idx <= tq_s, 0.0, NEG)] * (hw // QB), axis=1)
                      for k0, s in zip(k0s, ss)]
            m_old = m_sc[...]
            m_new = functools.reduce(jnp.maximum, [_col_max(s) for s in ss], m_old)
            alpha = jnp.exp2(m_old - m_new)
            ps = [jnp.exp2(s - m_new) for s in ss]
            l_sc[...] = alpha * l_sc[...] + functools.reduce(jnp.add, [_col_sum(p) for p in ps])
            pv = [jnp.dot(vt, p.astype(BF16), preferred_element_type=F32) for vt, p in zip(vts, ps)]
            acc_sc[...] = alpha * acc_sc[...] + functools.reduce(jnp.add, pv)
            m_sc[...] = m_new

    n_units = jnp.right_shift(t0, SEL_KC_SHIFT + SEL_UC_SHIFT)
    nh = len(halves)

    def unit(u):
        return [SEL_UC * u + c for c in range(SEL_UC)]

    def put_scores(buf, scores):
        for hp in range(nh):
            for c in range(SEL_UC):
                buf[hp * SEL_UC + c] = scores[hp][c]

    def get_scores(buf):
        return [[buf[hp * SEL_UC + c] for c in range(SEL_UC)] for hp in range(nh)]

    put_scores(s_bufs[0], sel_scores(unit(0)))

    def sel_step(u, carry):
        for parity in range(2):
            @pl.when((u & 1) == parity)
            def _():
                put_scores(s_bufs[1 - parity], sel_scores(unit(u + 1)))
                sel_update(unit(u), get_scores(s_bufs[parity]), False)
        return carry

    lax.fori_loop(0, n_units, sel_step, 0)
    for parity in range(2):
        @pl.when((n_units & 1) == parity)
        def _():
            sel_update(unit(n_units), get_scores(s_bufs[parity]), True)
    o_slc = jnp.concatenate([a[...] * (1.0 / l[...]) for _, l, a in halves], axis=1)

    gtt = gtt_ref[0, 0]
    outs = []
    for r in range(R):
        cs = slice(r * QB, (r + 1) * QB)
        outs.append(gtt[3 * r:3 * r + 1, :] * o_cmp[:, cs] + gtt[3 * r + 1:3 * r + 2, :] * o_slc[:, cs] +
                    gtt[3 * r + 2:3 * r + 3, :] * o_win[:, cs])
    o_ref[0] = jnp.concatenate(outs, axis=0).T.astype(o_ref.dtype)


def _nsa_attn(qt, kc, vct, ksa, vst, kw, vwt, gtt, B, S):
    G, R, QB, dk = NSA_KV_GROUPS, NSA_R, Q_BLOCK, NSA_HEAD_DIM
    nc = S // CMP_STRIDE
    n_sel = S // SLC_LEN
    assert n_sel <= LANES and n_sel >= SLC_TOPN and S % (SEL_UC * SEL_KC) == 0 and S >= WIN_KEYS
    assert SEL_KC == 1 << SEL_KC_SHIFT and SEL_UC == 1 << SEL_UC_SHIFT and SEL_KC % QB == 0
    assert nc % CMP_ROWS == 0 and CMP_ROWS == 1 << CMP_ROWS_SHIFT and CMP_STRIDE == 1 << CMP_STRIDE_SHIFT
    cs = np.arange(nc)[None, :] * CMP_STRIDE
    ss = np.arange(LANES)[:, None] * SLC_LEN
    ovm = (cs < ss + SLC_LEN) & (cs + CMP_LEN - 1 >= ss) & (np.arange(LANES)[:, None] < n_sel) \
        & (np.arange(nc)[None, :] < nc - 1)
    ovt = jnp.asarray(ovm, BF16)
    rows_in = lambda n, w: pl.BlockSpec((1, 1, n, w), lambda b, g, i: (b, g, 0, 0))
    cols = R * QB
    return pl.pallas_call(
        functools.partial(_nsa_kernel, n_sel=n_sel),
        grid=(B, G, S // QB),
        in_specs=[pl.BlockSpec((1, 1, 1, LANES, cols), lambda b, g, i: (b, g, i, 0, 0)),
                  rows_in(nc, LANES), rows_in(dk, nc), rows_in(S, 2 * LANES), rows_in(dk, S),
                  rows_in(S, LANES), rows_in(dk, S),
                  pl.BlockSpec((1, 1, GATE_ROWS, QB), lambda b, g, i: (b, g, 0, i)),
                  pl.BlockSpec((LANES, nc), lambda b, g, i: (0, 0))],
        out_specs=pl.BlockSpec((1, QB, R * dk), lambda b, g, i: (b, i, g)),
        out_shape=jax.ShapeDtypeStruct((B, S, NSA_HEADS * dk), BF16),
        scratch_shapes=[pltpu.VMEM((2 * LANES, cols), BF16)] +
                       [pltpu.VMEM((1, cols // 2), F32)] * 4 + [pltpu.VMEM((dk, cols // 2), F32)] * 2 +
                       [pltpu.VMEM((2 * SEL_UC, SEL_KC, cols // 2), F32)] * 2 +
                       [pltpu.VMEM((dk, cols), F32), pltpu.VMEM((LANES, QB), F32)],
        compiler_params=_cparams(("parallel", "parallel", "arbitrary")),
        name="nsa_attn",
    )(qt, kc, vct, ksa, vst, kw, vwt, gtt, ovt)


HG_TB = 256
HG_LEVELS = 6


def _hgrn_consts():
    C = HG_CHUNK
    M = np.zeros((HG_LEVELS + 2, C, C), np.float32)
    RM = np.zeros((HG_LEVELS, C, 1), np.float32)
    BM = np.zeros((HG_LEVELS + 1, C, C), np.float32)
    for lv in range(HG_LEVELS):
        m = 1 << lv
        for t in range(C):
            bs = (t // (2 * m)) * (2 * m)
            mid = bs + m - 1
            if t - bs >= m:
                RM[lv, t, 0] = 1.0
                M[lv, t, mid + 1:t + 1] = 1.0
            else:
                M[lv, t, t + 1:mid + 1] = 1.0
            BM[lv, t, bs:bs + 2 * m] = 1.0
    BM[HG_LEVELS] = np.eye(C, dtype=np.float32)
    M[HG_LEVELS] = np.tril(np.ones((C, C), np.float32))
    M[HG_LEVELS + 1] = np.triu(np.ones((C, C), np.float32), 1)
    mm = M.reshape(-1, C)
    mm3 = np.concatenate([mm, mm, mm], axis=1)
    rm = np.broadcast_to(RM, (HG_LEVELS, C, HG_HEADS * HG_DIM)).copy()
    return jnp.asarray(mm3, BF16), jnp.asarray(rm, F32), jnp.asarray(BM, F32)


def _hgrn_kernel(q_ref, f_ref, i_ref, g_ref, lbl_ref, gn_ref, mm_ref, rm_ref, bm_ref, o_ref, st_sc):
    C, D = HG_CHUNK, HG_DIM

    @pl.when(pl.program_id(1) == 0)
    def _():
        st_sc[...] = jnp.zeros(st_sc.shape, F32)

    lbl = lbl_ref[...]
    mxl = jnp.max(lbl, axis=0, keepdims=True)
    el = jnp.exp(lbl - mxl)
    lb_all = el[0:1, :] / jnp.sum(el, axis=0, keepdims=True)
    mm = mm_ref[...]
    gn = gn_ref[...]
    heads = [slice(h * D, (h + 1) * D) for h in range(HG_HEADS)]
    nt = lambda a, b: lax.dot_general(a, b, NT_DIMS, preferred_element_type=F32)
    for ch in range(HG_TB // C):
        rs = slice(ch * C, (ch + 1) * C)
        q = q_ref[0, rs, :].astype(F32)
        v = i_ref[0, rs, :]
        gg = g_ref[0, rs, :].astype(F32)
        f = lb_all + (1.0 - lb_all) * jax.nn.sigmoid(f_ref[0, rs, :].astype(F32))
        lg = jnp.log(f)
        kk = 1.0 - f
        g1 = lg.astype(BF16)
        r1 = lg - g1.astype(F32)
        g2 = r1.astype(BF16)
        g3 = (r1 - g2.astype(F32)).astype(BF16)
        ez = jnp.exp(jnp.dot(mm, jnp.concatenate([g1, g2, g3], axis=0), preferred_element_type=F32))
        qb, kb = q.astype(BF16), kk.astype(BF16)
        attn = [bm_ref[HG_LEVELS] * nt(qb[:, hs], kb[:, hs]) for hs in heads]
        for lv in range(HG_LEVELS):
            zl = ez[lv * C:(lv + 1) * C]
            rm = rm_ref[lv]
            ql = (q * zl * rm).astype(BF16)
            kl = (kk * zl * (1.0 - rm)).astype(BF16)
            attn = [a + bm_ref[lv] * nt(ql[:, hs], kl[:, hs]) for a, hs in zip(attn, heads)]
        zc = ez[HG_LEVELS * C:(HG_LEVELS + 1) * C]
        zs = ez[(HG_LEVELS + 1) * C:(HG_LEVELS + 2) * C]
        qz = (q * zc).astype(BF16)
        kz = (kk * zs).astype(BF16)
        sts = [st_sc[h] for h in range(HG_HEADS)]
        o_intra = [jnp.dot(a.astype(BF16), v[:, hs], preferred_element_type=F32) for a, hs in zip(attn, heads)]
        o_inter = [nt(qz[:, hs], st.astype(BF16)) for st, hs in zip(sts, heads)]
        upd = [lax.dot_general(v[:, hs], kz[:, hs], TN_DIMS, preferred_element_type=F32) for hs in heads]
        for h, hs in enumerate(heads):
            st_sc[h] = zc[C - 1:C, hs] * sts[h] + upd[h]
            o = o_intra[h] + o_inter[h]
            y = o * lax.rsqrt(jnp.mean(o * o, axis=-1, keepdims=True) + RMS_EPS) * gn
            g = gg[:, hs]
            o_ref[0, rs, hs] = (y * (g * jax.nn.sigmoid(g))).astype(o_ref.dtype)


def _hgrn(hq, hf, hi, hg, lb_logits, hg_norm, B, S):
    W = HG_HEADS * HG_DIM
    mm3, rm, bm = _hgrn_consts()
    r3 = lambda a: a.reshape(B, S, W)
    tok = pl.BlockSpec((1, HG_TB, W), lambda b, j: (b, j, 0))
    const = lambda a: pl.BlockSpec(a.shape, lambda b, j: (0,) * a.ndim)
    gn = hg_norm.reshape(1, HG_DIM)
    return pl.pallas_call(
        _hgrn_kernel,
        grid=(B, S // HG_TB),
        in_specs=[tok, tok, tok, tok, const(lb_logits), const(gn), const(mm3), const(rm), const(bm)],
        out_specs=tok,
        out_shape=jax.ShapeDtypeStruct((B, S, W), BF16),
        scratch_shapes=[pltpu.VMEM((HG_HEADS, HG_DIM, HG_DIM), F32)],
        compiler_params=_cparams(("parallel", "arbitrary")),
        name="hgrn2",
    )(r3(hq), r3(hf), r3(hi), r3(hg), lb_logits, gn, mm3, rm, bm)


MG_TM = 256


SUBLANES = 8


def _to_row_tiles(ref, x, lead=()):
    for s in range(SUBLANES):
        ref[lead + (slice(None), s, slice(None))] = x[:, s * LANES:(s + 1) * LANES]


def _from_row_tiles(ref, lead=()):
    return jnp.concatenate([ref[lead + (slice(None), s, slice(None))] for s in range(SUBLANES)], axis=1)


def _first_lane(cond, lane_f):
    return jnp.min(jnp.where(cond, lane_f, float(LANES)), axis=1, keepdims=True)


def _merge_kernel(on_ref, oh_ref, mg_ref, x_ref, wn_ref, wh_ref, wo_ref, fn_ref, wr1_ref, wr2_ref, br_ref,
                  x1_ref, h2_ref, rt_ref):
    D = x_ref.shape[1]
    a = jnp.dot(on_ref[...], wn_ref[...], preferred_element_type=F32)
    b = jnp.dot(oh_ref[...], wh_ref[...], preferred_element_type=F32)
    mixed = (jax.nn.sigmoid(mg_ref[:, 0:D].astype(F32)) * a +
             jax.nn.sigmoid(mg_ref[:, D:2 * D].astype(F32)) * b)
    x1 = x_ref[...] + jnp.dot(mixed.astype(BF16), wo_ref[...], preferred_element_type=F32)
    x1_ref[...] = x1
    h2 = x1 * lax.rsqrt(jnp.mean(x1 * x1, axis=-1, keepdims=True) + RMS_EPS) * fn_ref[...]
    _to_row_tiles(h2_ref, h2)
    h_hi = h2.astype(BF16)
    h_lo = (h2 - h_hi.astype(F32)).astype(BF16)
    logits = (jnp.dot(h_lo, wr1_ref[...], preferred_element_type=F32) +
              jnp.dot(h_hi, wr2_ref[...], preferred_element_type=F32) +
              jnp.dot(h_hi, wr1_ref[...], preferred_element_type=F32)) + br_ref[...]
    lane = lax.broadcasted_iota(jnp.int32, logits.shape, 1)
    ninf = -jnp.inf
    gl = jnp.where(lane < N_GROUPS, logits, ninf)
    gmax = jnp.max(gl, axis=1, keepdims=True)
    lane_f = lane.astype(F32)
    gsel = _first_lane(gl == gmax, lane_f)
    gprob = 1.0 / jnp.sum(jnp.exp(gl - gmax), axis=1, keepdims=True)
    lo = N_GROUPS + gsel * EXPERTS_PER_GROUP
    el = jnp.where((lane_f >= lo) & (lane_f < lo + EXPERTS_PER_GROUP), logits, ninf)
    v1 = jnp.max(el, axis=1, keepdims=True)
    i1 = _first_lane(el == v1, lane_f)
    el2 = jnp.where(lane_f == i1, ninf, el)
    v2 = jnp.max(el2, axis=1, keepdims=True)
    i2 = _first_lane(el2 == v2, lane_f)
    e2 = jnp.exp(v2 - v1)
    w1 = gprob / (1.0 + e2)
    w2 = gprob * e2 / (1.0 + e2)
    rt = jnp.where(lane == 0, i1 - N_GROUPS,
                   jnp.where(lane == 1, i2 - N_GROUPS,
                             jnp.where(lane == 2, w1, jnp.where(lane == 3, w2, 0.0))))
    rt_ref[...] = rt


def _merge(o_nsa, o_hg, mg, x2, w_br_nsa, w_br_hg, w_out, ffn_norm, w_grp, b_grp, w_rtr, b_rtr):
    T, D = x2.shape
    nr = N_GROUPS + N_EXPERTS
    wr = jnp.pad(jnp.concatenate([w_grp, w_rtr], axis=1), ((0, 0), (0, LANES - nr)))
    wr1 = wr.astype(BF16)
    wr2 = (wr - wr1.astype(F32)).astype(BF16)
    br = jnp.pad(jnp.concatenate([b_grp, b_rtr]), (0, LANES - nr)).reshape(1, LANES)
    tok = lambda w: pl.BlockSpec((MG_TM, w), lambda i: (i, 0))
    const = lambda a: pl.BlockSpec(a.shape, lambda i: (0,) * a.ndim)
    wn, wh, wo = w_br_nsa.astype(BF16), w_br_hg.astype(BF16), w_out.astype(BF16)
    fn = ffn_norm.reshape(1, D)
    return pl.pallas_call(
        _merge_kernel,
        grid=(T // MG_TM,),
        in_specs=[tok(o_nsa.shape[1]), tok(o_hg.shape[1]), tok(2 * D), tok(D),
                  const(wn), const(wh), const(wo), const(fn), const(wr1), const(wr2), const(br)],
        out_specs=[tok(D), pl.BlockSpec((MG_TM, SUBLANES, D // SUBLANES), lambda i: (i, 0, 0)), tok(LANES)],
        out_shape=[jax.ShapeDtypeStruct((T, D), F32), jax.ShapeDtypeStruct((T, SUBLANES, D // SUBLANES), F32),
                   jax.ShapeDtypeStruct((T, LANES), F32)],
        compiler_params=_cparams(("parallel",)),
        name="merge",
    )(o_nsa, o_hg, mg, x2, wn, wh, wo, fn, wr1, wr2, br)


MOE_TM = 256


def _moe_kernel(te_ref, nv_ref, src_ref, srcn_ref, dst_ref, h_hbm, wg_ref, wu_ref, wd_ref, y_hbm,
                xbuf, ybuf, wgb, wub, wdb, gsem, ssem):
    i = pl.program_id(0)
    n = pl.num_programs(0)
    tm = xbuf.shape[1]
    slot = i & 1
    prv = jnp.maximum(i - 1, 0)
    nv_i = nv_ref[i]
    nv_prev = jnp.where(i > 0, nv_ref[prv], 0)
    nv_next = jnp.where(i + 1 < n, nv_ref[jnp.minimum(i + 1, n - 1)], 0)

    def gather_copy(idx_ref, r, sl):
        return pltpu.make_async_copy(h_hbm.at[idx_ref[0, 0, r]], xbuf.at[sl, r], gsem.at[sl])

    def scatter_copy(row, r, sl):
        return pltpu.make_async_copy(ybuf.at[sl, r], y_hbm.at[row], ssem.at[sl])

    def rows(count, fn):
        groups = jnp.right_shift(count, 3)

        def body8(g, c):
            for k in range(8):
                fn(g * 8 + k)
            return c

        def body1(r, c):
            fn(r)
            return c

        lax.fori_loop(0, groups, body8, 0)
        lax.fori_loop(groups * 8, count, body1, 0)

    def wait_rows(count, copy_of):
        k = tm
        while k >= 1:
            pl.when((count & k) != 0)(lambda k=k: copy_of(k).wait())
            k //= 2

    def gather_block(k, sl):
        return pltpu.make_async_copy(h_hbm.at[pl.ds(0, k)], xbuf.at[sl, pl.ds(0, k)], gsem.at[sl])

    def scatter_block(k, sl):
        return pltpu.make_async_copy(ybuf.at[sl, pl.ds(0, k)], y_hbm.at[pl.ds(0, k)], ssem.at[sl])

    @pl.when(i == 0)
    def _():
        xbuf[...] = jnp.zeros(xbuf.shape, F32)
        rows(nv_i, lambda r: gather_copy(src_ref, r, 0).start())

    rows(nv_next, lambda r: gather_copy(srcn_ref, r, 1 - slot).start())

    @pl.when((i == 0) | (te_ref[i] != te_ref[prv]))
    def _():
        wgb[...] = wg_ref[0].astype(BF16)
        wub[...] = wu_ref[0].astype(BF16)
        wdb[...] = wd_ref[0].astype(BF16)

    @pl.when(nv_i > 0)
    def _():
        wait_rows(nv_i, lambda k: gather_block(k, slot))
        x = _from_row_tiles(xbuf, (slot,)).astype(BF16)
        hg = jnp.dot(x, wgb[...], preferred_element_type=F32)
        hu = jnp.dot(x, wub[...], preferred_element_type=F32)
        hid = (hg * jax.nn.sigmoid(hg) * hu).astype(BF16)
        _to_row_tiles(ybuf, jnp.dot(hid, wdb[...], preferred_element_type=F32), (slot,))
        rows(nv_i, lambda r: scatter_copy(dst_ref[0, 0, r], r, slot).start())

    wait_rows(nv_prev, lambda k: scatter_block(k, 1 - slot))

    @pl.when(i == n - 1)
    def _():
        wait_rows(nv_i, lambda k: scatter_block(k, slot))


def _moe(h2, route, w_gate, w_up, w_down):
    T = h2.shape[0]
    E, D, F = w_gate.shape
    tm = MOE_TM
    n_tiles = 2 * T // tm + E
    i32 = jnp.int32
    ex = jnp.concatenate([route[:, 0], route[:, 1]]).astype(i32)
    order = jnp.argsort(ex, stable=True).astype(i32)
    counts = jnp.sum(ex[:, None] == jnp.arange(E, dtype=i32)[None, :], axis=0).astype(i32)
    padded = ((counts + tm - 1) // tm) * tm
    pend = jnp.cumsum(padded)
    pstart = pend - padded
    cstart = jnp.cumsum(counts) - counts
    tile_start = jnp.arange(n_tiles, dtype=i32) * tm
    te_raw = jnp.sum(tile_start[:, None] >= pend[None, :], axis=1).astype(i32)
    active = te_raw < E
    te_c = jnp.minimum(te_raw, E - 1)
    off = tile_start - pstart[te_c]
    nv = jnp.where(active, jnp.clip(counts[te_c] - off, 0, tm), 0).astype(i32)
    n_active = jnp.sum(active.astype(i32))
    te = jnp.where(active, te_c, te_c[jnp.maximum(n_active - 1, 0)])
    rank = off[:, None] + jnp.arange(tm, dtype=i32)[None, :]
    a = order[jnp.clip(cstart[te_c][:, None] + rank, 0, 2 * T - 1)]
    valid = jnp.arange(tm, dtype=i32)[None, :] < nv[:, None]
    src = jnp.where(valid, a % T, 0).reshape(n_tiles, 1, tm)
    dst = jnp.where(valid, a, 0).reshape(n_tiles, 1, tm)

    idx_spec = lambda f: pl.BlockSpec((1, 1, tm), lambda i, te, nv: (f(i), 0, 0), memory_space=pltpu.SMEM)
    wspec = lambda r, c: pl.BlockSpec((1, r, c), lambda i, te, nv: (te[i], 0, 0))
    grid_spec = pltpu.PrefetchScalarGridSpec(
        num_scalar_prefetch=2,
        grid=(n_tiles,),
        in_specs=[idx_spec(lambda i: i), idx_spec(lambda i: jnp.minimum(i + 1, n_tiles - 1)),
                  idx_spec(lambda i: i), pl.BlockSpec(memory_space=pl.ANY),
                  wspec(D, F), wspec(D, F), wspec(F, D)],
        out_specs=pl.BlockSpec(memory_space=pl.ANY),
        scratch_shapes=[pltpu.VMEM((2, tm, SUBLANES, D // SUBLANES), F32),
                        pltpu.VMEM((2, tm, SUBLANES, D // SUBLANES), F32),
                        pltpu.VMEM((D, F), BF16), pltpu.VMEM((D, F), BF16), pltpu.VMEM((F, D), BF16),
                        pltpu.SemaphoreType.DMA((2,)), pltpu.SemaphoreType.DMA((2,))],
    )
    return pl.pallas_call(
        _moe_kernel,
        grid_spec=grid_spec,
        out_shape=jax.ShapeDtypeStruct((2 * T, SUBLANES, D // SUBLANES), F32),
        compiler_params=_cparams(("arbitrary",)),
        name="moe",
    )(te, nv, src, src, dst, h2, w_gate, w_up, w_down)


FIN_TM = 256


def _final_kernel(x1_ref, y1_ref, y2_ref, rt_ref, g_ref, o_ref):
    rt = rt_ref[...]
    x = x1_ref[...] + rt[:, 2:3] * _from_row_tiles(y1_ref) + rt[:, 3:4] * _from_row_tiles(y2_ref)
    o_ref[...] = x * lax.rsqrt(jnp.mean(x * x, axis=-1, keepdims=True) + RMS_EPS) * g_ref[...]


def _final(x1, y, route, final_norm):
    T, D = x1.shape
    nb = T // FIN_TM
    ytile = (FIN_TM, SUBLANES, D // SUBLANES)
    return pl.pallas_call(
        _final_kernel,
        grid=(nb,),
        in_specs=[pl.BlockSpec((FIN_TM, D), lambda i: (i, 0)),
                  pl.BlockSpec(ytile, lambda i: (i, 0, 0)),
                  pl.BlockSpec(ytile, lambda i: (i + nb, 0, 0)),
                  pl.BlockSpec((FIN_TM, LANES), lambda i: (i, 0)),
                  pl.BlockSpec((1, D), lambda i: (0, 0))],
        out_specs=pl.BlockSpec((FIN_TM, D), lambda i: (i, 0)),
        out_shape=jax.ShapeDtypeStruct((T, D), F32),
        compiler_params=_cparams(("parallel",)),
        name="final",
    )(x1, y, y, route, final_norm.reshape(1, D))


def kernel(x, attn_norm, w_in, w_cmp_k, w_cmp_v, cmp_pos, hg_lb_logits, hg_norm, w_br_nsa, w_br_hg, w_out,
           ffn_norm, w_grp, b_grp, w_rtr, b_rtr, w_gate, w_up, w_down, final_norm):
    B, S, D = x.shape
    assert attn_norm.shape[0] == 1, "single-layer block"
    x2 = x.reshape(B * S, D)
    q, kcin, vcin, kv, gate, hq, hf, hi, hg, mg = _in_proj(x2, attn_norm[0], w_in[0])
    qt, ksa, vst, kw, vwt, gtt = _nsa_prep(q, kv, gate, B, S)
    kc, vct = _compress(kcin, vcin, w_cmp_k[0], w_cmp_v[0], cmp_pos[0], B, S)
    o_nsa = _nsa_attn(qt, kc, vct, ksa, vst, kw, vwt, gtt, B, S).reshape(B * S, -1)
    o_hg = _hgrn(hq, hf, hi, hg, hg_lb_logits, hg_norm[0], B, S).reshape(B * S, -1)
    x1, h2, route = _merge(o_nsa, o_hg, mg, x2, w_br_nsa[0], w_br_hg[0], w_out[0], ffn_norm[0],
                           w_grp[0], b_grp[0], w_rtr[0], b_rtr[0])
    y = _moe(h2, route, w_gate[0], w_up[0], w_down[0])
    out = _final(x1, y, route, final_norm)
    return out.reshape(B, S, D)
```

```python
import functools

import numpy as np
import jax
import jax.numpy as jnp
from jax import lax
from jax.experimental import pallas as pl
from jax.experimental.pallas import tpu as pltpu

F32 = jnp.float32
BF16 = jnp.bfloat16

NSA_HEADS = 8
NSA_KV_GROUPS = 2
NSA_HEAD_DIM = 64
NSA_R = NSA_HEADS // NSA_KV_GROUPS
CMP_LEN = 32
CMP_STRIDE = 16
SLC_LEN = 64
SLC_SHIFT = 6
SLC_TOPN = 16
WINDOW = 512
Q_BLOCK = 128
ROPE_THETA = 500000.0
ROPE_DIM = NSA_HEAD_DIM // 4
HG_HEADS = 4
HG_DIM = 128
HG_CHUNK = 64
N_GROUPS = 4
EXPERTS_PER_GROUP = 8
N_EXPERTS = N_GROUPS * EXPERTS_PER_GROUP
RMS_EPS = 1e-6
NEG = -1e30
FORCE_SCORE = 1e4

LANES = 128
VMEM_LIMIT = 56 * 1024 * 1024

NT_DIMS = (((1,), (1,)), ((), ()))
TN_DIMS = (((0,), (0,)), ((), ()))


def _cparams(sem):
    return pltpu.CompilerParams(dimension_semantics=sem, vmem_limit_bytes=VMEM_LIMIT)


IN_TM = 256
IN_SEGS = (("q", 512, 512), ("kc", 128, 128), ("vc", 128, 128), ("kv", 512, 512), ("gate", 24, 128),
           ("hq", 512, 512), ("hf", 512, 512), ("hi", 512, 512), ("hg", 512, 512), ("merge", 2048, 2048))
IN_F32 = ("kc", "vc")
IN_PREP = ("q", "kv", "gate")


def _in_proj_kernel(x_ref, g_ref, w_ref, c_ref, s1_ref, s2_ref, nege_ref, *out_refs):
    n_plain = len(IN_SEGS) - len(IN_PREP)
    plain_refs, prep_refs = iter(out_refs[:n_plain]), out_refs[n_plain:]
    x = x_ref[...]
    h = x * lax.rsqrt(jnp.mean(x * x, axis=-1, keepdims=True) + RMS_EPS) * g_ref[...]
    h = h.astype(BF16)
    off, kept = 0, {}
    for name, _, wpad in IN_SEGS:
        y = jnp.dot(h, w_ref[:, off:off + wpad], preferred_element_type=F32)
        off += wpad
        if name in IN_PREP:
            kept[name] = y
            if len(kept) == len(IN_PREP):
                _prep_body(kept["q"], kept["kv"], kept["gate"], c_ref[...], s1_ref[...], s2_ref[...],
                           nege_ref[...], *prep_refs)
        else:
            o_ref = next(plain_refs)
            o_ref[...] = y.astype(o_ref.dtype)


def _in_proj(x2, attn_norm, w_in, B, S):
    T, D = x2.shape
    G, dk, QB = NSA_KV_GROUPS, NSA_HEAD_DIM, Q_BLOCK
    pieces, off = [], 0
    for _, w, wpad in IN_SEGS:
        p = w_in[:, off:off + w]
        if wpad != w:
            p = jnp.pad(p, ((0, 0), (0, wpad - w)))
        pieces.append(p)
        off += w
    wcat = jnp.concatenate(pieces, axis=1).astype(BF16)
    NP = wcat.shape[1]
    plain = [(jax.ShapeDtypeStruct((T, wpad), F32 if name in IN_F32 else BF16),
              pl.BlockSpec((IN_TM, wpad), lambda i: (i, 0)))
             for name, _, wpad in IN_SEGS if name not in IN_PREP]
    nj = S // IN_TM
    c, s1, s2 = _rope_tables(jnp.arange(S))
    key = np.arange(S)
    nege = jnp.asarray(np.where(key[:, None] // SLC_LEN == np.arange(LANES)[None, :], NEG, 0.0), BF16)
    tab = pl.BlockSpec((IN_TM, LANES), lambda i: (i % nj, 0))
    rows_out = lambda w, dt: (jax.ShapeDtypeStruct((B, G, S, w), dt),
                              pl.BlockSpec((1, G, IN_TM, w), lambda i: (i // nj, 0, i % nj, 0)))
    cols_out = lambda n, dt: (jax.ShapeDtypeStruct((B, G, n, S), dt),
                              pl.BlockSpec((1, G, n, IN_TM), lambda i: (i // nj, 0, 0, i % nj)))
    qt_out = (jax.ShapeDtypeStruct((B, G, S // QB, LANES, NSA_R * QB), BF16),
              pl.BlockSpec((1, G, IN_TM // QB, LANES, NSA_R * QB), lambda i: (i // nj, 0, i % nj, 0, 0)))
    prep = [qt_out, rows_out(2 * LANES, BF16), cols_out(dk, BF16), rows_out(LANES, BF16),
            cols_out(dk, BF16), cols_out(GATE_ROWS, F32)]
    outs = pl.pallas_call(
        _in_proj_kernel,
        grid=(T // IN_TM,),
        in_specs=[pl.BlockSpec((IN_TM, D), lambda i: (i, 0)),
                  pl.BlockSpec((1, D), lambda i: (0, 0)),
                  pl.BlockSpec((D, NP), lambda i: (0, 0)), tab, tab, tab, tab],
        out_specs=[o[1] for o in plain + prep],
        out_shape=[o[0] for o in plain + prep],
        compiler_params=_cparams(("parallel",)),
        name="in_proj",
    )(x2, attn_norm.reshape(1, D), wcat, c, s1, s2, nege)
    return outs[:len(plain)], outs[len(plain):]


def _rope_tables(pos):
    half = ROPE_DIM // 2
    d = np.arange(LANES) % NSA_HEAD_DIM
    inv_freq = ROPE_THETA ** (-jnp.arange(half, dtype=F32) / half)
    lane_freq = jnp.where(d < ROPE_DIM, inv_freq[d % half], 0.0)
    ang = pos.astype(F32)[:, None] * lane_freq[None, :]
    cos, sin = jnp.cos(ang), jnp.sin(ang)
    s1 = jnp.where(d < half, -sin, 0.0)
    s2 = jnp.where((d >= half) & (d < ROPE_DIM), sin, 0.0)
    return cos, s1, s2


def _rope(x, c, s1, s2):
    half = ROPE_DIM // 2
    return x * c + pltpu.roll(x, LANES - half, 1) * s1 + pltpu.roll(x, half, 1) * s2


GATE_ROWS = 16
LOG2E = 1.4426950408889634


def _prep_body(q, kv, gate, c, s1, s2, nege, qt_ref, ksa_ref, vst_ref, kw_ref, vwt_ref, gtt_ref):
    lane = lax.broadcasted_iota(jnp.int32, c.shape, 1)
    lo = lane < NSA_HEAD_DIM
    dk, QB = NSA_HEAD_DIM, Q_BLOCK

    def split(x):
        return (jnp.where(lo, x, 0.0), jnp.where(lo, pltpu.roll(x, dk, 1), 0.0))

    scale = (dk ** -0.5) * LOG2E
    for blk in range(NSA_HEADS // 2):
        x = q[:, blk * LANES:(blk + 1) * LANES]
        for hh, part in enumerate(split(_rope(x, c, s1, s2) * scale)):
            h = 2 * blk + hh
            g, r = h // NSA_R, h % NSA_R
            pt = part.T.astype(BF16)
            for i in range(q.shape[0] // QB):
                qt_ref[0, g, i, :, r * QB:(r + 1) * QB] = pt[:, i * QB:(i + 1) * QB]
    for g, part in enumerate(split(_rope(kv[:, 0:LANES], c, s1, s2))):
        ksa_ref[0, g] = jnp.concatenate([part.astype(BF16), nege], axis=1)
    for g, part in enumerate(split(_rope(kv[:, 2 * LANES:3 * LANES], c, s1, s2))):
        kw_ref[0, g] = part.astype(BF16)
    for blk, ref in ((1, vst_ref), (3, vwt_ref)):
        for g, part in enumerate(split(kv[:, blk * LANES:(blk + 1) * LANES])):
            ref[0, g] = part.T[0:dk, :].astype(BF16)
    sg = jax.nn.sigmoid(gate)
    gtt_ref[0, 0] = sg.T[0:GATE_ROWS, :]
    gtt_ref[0, 1] = pltpu.roll(sg, LANES - 3 * NSA_R, 1).T[0:GATE_ROWS, :]


def _compress_kernel(tk_ref, tv_ref, wk_ref, wv_ref, pos_ref, c_ref, s1_ref, s2_ref, kc_ref, vct_ref):
    nc = c_ref.shape[0]
    dk = NSA_HEAD_DIM
    lane = lax.broadcasted_iota(jnp.int32, (nc, LANES), 1)
    lo = lane < dk

    def comp(t_ref, w_ref):
        a = jnp.zeros((nc, LANES), F32)
        b = jnp.zeros((nc, LANES), F32)
        c0 = jnp.zeros((SUBLANES, LANES), F32)
        for j in range(CMP_STRIDE):
            u = t_ref[pl.ds(j, nc, stride=CMP_STRIDE), :].astype(BF16)
            a = a + jnp.dot(u, w_ref[j], preferred_element_type=F32)
            b = b + jnp.dot(u, w_ref[CMP_STRIDE + j], preferred_element_type=F32)
        for l in range(CMP_LEN):
            c0 = c0 + jnp.dot(pos_ref[l], w_ref[l], preferred_element_type=F32)
        return a + pltpu.roll(b, nc - 1, 0) + c0[0:1, :]

    def split(x):
        return (jnp.where(lo, x, 0.0), jnp.where(lo, pltpu.roll(x, dk, 1), 0.0))

    for g, part in enumerate(split(_rope(comp(tk_ref, wk_ref), c_ref[...], s1_ref[...], s2_ref[...]))):
        kc_ref[0, g] = part.astype(BF16)
    for g, part in enumerate(split(comp(tv_ref, wv_ref))):
        vct_ref[0, g] = part.T[0:dk, :].astype(BF16)


def _compress(kcin, vcin, w_cmp_k, w_cmp_v, cmp_pos, B, S):
    G, dk = NSA_KV_GROUPS, NSA_HEAD_DIM
    nc = S // CMP_STRIDE

    def block_diag(w):
        w3 = w.reshape(CMP_LEN, dk, dk)
        z = jnp.zeros_like(w3)
        return jnp.concatenate([jnp.concatenate([w3, z], axis=2),
                                jnp.concatenate([z, w3], axis=2)], axis=1).astype(BF16)

    pos = jnp.concatenate([cmp_pos, cmp_pos], axis=1)
    pos = jnp.broadcast_to(pos[:, None, :], (CMP_LEN, SUBLANES, G * dk)).astype(BF16)
    c, s1, s2 = _rope_tables(jnp.arange(nc) * CMP_STRIDE)
    tspec = pl.BlockSpec((S, LANES), lambda b: (b, 0))
    wspec = pl.BlockSpec((CMP_LEN, LANES, LANES), lambda b: (0, 0, 0))
    tab = pl.BlockSpec((nc, LANES), lambda b: (0, 0))
    return pl.pallas_call(
        _compress_kernel,
        grid=(B,),
        in_specs=[tspec, tspec, wspec, wspec,
                  pl.BlockSpec((CMP_LEN, SUBLANES, LANES), lambda b: (0, 0, 0)), tab, tab, tab],
        out_specs=[pl.BlockSpec((1, G, nc, LANES), lambda b: (b, 0, 0, 0)),
                   pl.BlockSpec((1, G, dk, nc), lambda b: (b, 0, 0, 0))],
        out_shape=[jax.ShapeDtypeStruct((B, G, nc, LANES), BF16),
                   jax.ShapeDtypeStruct((B, G, dk, nc), BF16)],
        compiler_params=_cparams(("parallel",)),
        name="compress",
    )(kcin, vcin, block_diag(w_cmp_k), block_diag(w_cmp_v), pos, c, s1, s2)


CMP_STRIDE_SHIFT = 4
CMP_ROWS = 128
CMP_ROWS_SHIFT = 7
SEL_KC = 256
SEL_KC_SHIFT = 8
SEL_UC = 2
SEL_UC_SHIFT = 1
WIN_KEYS = WINDOW + Q_BLOCK


def _col_reduce(x, op, fin):
    n = x.shape[0]
    while (n // 2) % 8 == 0 and n > 8:
        n //= 2
        x = op(x[:n], x[n:])
    return fin(x, axis=0, keepdims=True)


def _col_max(x):
    return _col_reduce(x, jnp.maximum, jnp.max)


def _col_min(x):
    return _col_reduce(x, jnp.minimum, jnp.min)


def _col_sum(x):
    return _col_reduce(x, jnp.add, jnp.sum)


def _softmax_cols(s):
    e = jnp.exp2(s - _col_max(s))
    return e, _col_sum(e)


def _nsa_kernel(qt_ref, kc_ref, vct_ref, ksa_ref, vst_ref, kw_ref, vwt_ref, gtt_ref, ovt_ref,
                o_ref, rhs_sc, m0_sc, m1_sc, l0_sc, l1_sc, a0_sc, a1_sc, sa_sc, sb_sc, ocmp_sc, imp_sc,
                *, n_sel):
    m_scs, l_scs, acc_scs = (m0_sc, m1_sc), (l0_sc, l1_sc), (a0_sc, a1_sc)
    s_bufs = (sa_sc, sb_sc)
    R, QB = NSA_R, Q_BLOCK
    cols = R * QB
    i = pl.program_id(2)
    t0 = i * QB
    qt = qt_ref[0, 0, 0]

    def per_head(x):
        return jnp.concatenate([x] * (cols // x.shape[1]), axis=1)

    def rc(n):
        return (lax.broadcasted_iota(jnp.int32, (n, QB), 0),
                t0 + lax.broadcasted_iota(jnp.int32, (n, QB), 1))

    nc = kc_ref.shape[2]

    def cmp_branch(nr):
        sc = jnp.dot(kc_ref[0, 0, 0:nr, :], qt, preferred_element_type=F32)
        n_idx, tq_c = rc(nr)
        cvis = (n_idx * CMP_STRIDE + (CMP_LEN - 1)) <= tq_c
        ec, lc = _softmax_cols(sc + per_head(jnp.where(cvis, 0.0, NEG)))
        pc = ec * per_head(jnp.where(cvis, 1.0, 0.0)) * (1.0 / lc)
        ocmp_sc[...] = jnp.dot(vct_ref[0, 0, :, 0:nr], pc.astype(BF16), preferred_element_type=F32)
        psum = pc[:, 0:QB] + pc[:, QB:2 * QB] + pc[:, 2 * QB:3 * QB] + pc[:, 3 * QB:4 * QB]
        p_hi = psum.astype(BF16)
        p_lo = (psum - p_hi.astype(F32)).astype(BF16)
        ovt = ovt_ref[:, 0:nr]
        imp_sc[...] = (jnp.dot(ovt, p_hi, preferred_element_type=F32) +
                       jnp.dot(ovt, p_lo, preferred_element_type=F32))

    n_var = nc // CMP_ROWS
    last_vis = jnp.right_shift(t0 + QB - CMP_LEN, CMP_STRIDE_SHIFT)
    var = jnp.minimum(jnp.right_shift(jnp.maximum(last_vis, 0), CMP_ROWS_SHIFT), n_var - 1)
    for k in range(n_var):
        pl.when(var == k)(functools.partial(cmp_branch, (k + 1) * CMP_ROWS))
    o_cmp = ocmp_sc[...]
    imp = imp_sc[...]

    start = pl.multiple_of(jnp.maximum(t0 - WINDOW, 0), QB)
    sw = jnp.dot(kw_ref[0, 0, pl.ds(start, WIN_KEYS), :], qt, preferred_element_type=F32)

    j = lax.broadcasted_iota(jnp.int32, (LANES, QB), 0)
    tq = t0 + lax.broadcasted_iota(jnp.int32, (LANES, QB), 1)
    force = (j == jnp.right_shift(tq, SLC_SHIFT)) | (j == 0)
    valid = (j * SLC_LEN) <= tq
    score = jnp.where(force, -jnp.inf, jnp.where(valid, imp, -1.0))
    work = jnp.where(j < n_sel, score, -jnp.inf)
    jf = j.astype(F32)
    for _ in range(min(SLC_TOPN, n_sel) - 2):
        mx = _col_max(work)
        idx = _col_min(jnp.where(work == mx, jf, float(LANES)))
        work = jnp.where(jf == idx, -jnp.inf, work)
    notsel = jnp.where(work == -jnp.inf, 0.0, 1.0)

    k_idx, tq_w = rc(WIN_KEYS)
    rel = tq_w - (start + k_idx)
    pw, lw = _softmax_cols(sw + per_head(jnp.where((rel >= 0) & (rel < WINDOW), 0.0, NEG)))
    o_win = jnp.dot(vwt_ref[0, 0, :, pl.ds(start, WIN_KEYS)], pw.astype(BF16),
                    preferred_element_type=F32) * (1.0 / lw)

    rhs_sc[0:LANES, :] = qt
    rhs_sc[LANES:2 * LANES, :] = jnp.concatenate([notsel.astype(BF16)] * R, axis=1)
    halves = tuple(zip(m_scs, l_scs, acc_scs))
    hw = cols // len(halves)
    for m_sc, l_sc, acc_sc in halves:
        m_sc[...] = jnp.full(m_sc.shape, NEG, F32)
        l_sc[...] = jnp.zeros(l_sc.shape, F32)
        acc_sc[...] = jnp.zeros(acc_sc.shape, F32)

    def sel_scores(chunk_ids):
        kas = [ksa_ref[0, 0, pl.ds(pl.multiple_of(c * SEL_KC, SEL_KC), SEL_KC), :] for c in chunk_ids]
        return [[jnp.dot(ka, rhs_sc[:, hp * hw:(hp + 1) * hw], preferred_element_type=F32)
                 for ka in kas] for hp in range(len(halves))]

    def sel_update(chunk_ids, scores, diagonal):
        k0s = [pl.multiple_of(c * SEL_KC, SEL_KC) for c in chunk_ids]
        vts = [vst_ref[0, 0, :, pl.ds(k0, SEL_KC)] for k0 in k0s]
        for hp, (m_sc, l_sc, acc_sc) in enumerate(halves):
            ss = scores[hp]
            if diagonal:
                k_idx, tq_s = rc(SEL_KC)
                ss = [s + jnp.concatenate([jnp.where(k0 + k_idx <= tq_s, 0.0, NEG)] * (hw // QB), axis=1)
                      for k0, s in zip(k0s, ss)]
            m_old = m_sc[...]
            m_new = functools.reduce(jnp.maximum, [_col_max(s) for s in ss], m_old)
            alpha = jnp.exp2(m_old - m_new)
            ps = [jnp.exp2(s - m_new) for s in ss]
            l_sc[...] = alpha * l_sc[...] + functools.reduce(jnp.add, [_col_sum(p) for p in ps])
            pv = [jnp.dot(vt, p.astype(BF16), preferred_element_type=F32) for vt, p in zip(vts, ps)]
            acc_sc[...] = alpha * acc_sc[...] + functools.reduce(jnp.add, pv)
            m_sc[...] = m_new

    n_units = jnp.right_shift(t0, SEL_KC_SHIFT + SEL_UC_SHIFT)
    nh = len(halves)

    def unit(u):
        return [SEL_UC * u + c for c in range(SEL_UC)]

    def put_scores(buf, scores):
        for hp in range(nh):
            for c in range(SEL_UC):
                buf[hp * SEL_UC + c] = scores[hp][c]

    def get_scores(buf):
        return [[buf[hp * SEL_UC + c] for c in range(SEL_UC)] for hp in range(nh)]

    put_scores(s_bufs[0], sel_scores(unit(0)))

    def sel_step(u, carry):
        for parity in range(2):
            @pl.when((u & 1) == parity)
            def _():
                put_scores(s_bufs[1 - parity], sel_scores(unit(u + 1)))
                sel_update(unit(u), get_scores(s_bufs[parity]), False)
        return carry

    lax.fori_loop(0, n_units, sel_step, 0)
    for parity in range(2):
        @pl.when((n_units & 1) == parity)
        def _():
            sel_update(unit(n_units), get_scores(s_bufs[parity]), True)
    o_slc = jnp.concatenate([a[...] * (1.0 / l[...]) for _, l, a in halves], axis=1)

    gtt = gtt_ref[0, 0]
    outs = []
    for r in range(R):
        cs = slice(r * QB, (r + 1) * QB)
        outs.append(gtt[3 * r:3 * r + 1, :] * o_cmp[:, cs] + gtt[3 * r + 1:3 * r + 2, :] * o_slc[:, cs] +
                    gtt[3 * r + 2:3 * r + 3, :] * o_win[:, cs])
    o_ref[0] = jnp.concatenate(outs, axis=0).T.astype(o_ref.dtype)


def _nsa_attn(qt, kc, vct, ksa, vst, kw, vwt, gtt, B, S):
    G, R, QB, dk = NSA_KV_GROUPS, NSA_R, Q_BLOCK, NSA_HEAD_DIM
    nc = S // CMP_STRIDE
    n_sel = S // SLC_LEN
    assert n_sel <= LANES and n_sel >= SLC_TOPN and S % (SEL_UC * SEL_KC) == 0 and S >= WIN_KEYS
    assert SEL_KC == 1 << SEL_KC_SHIFT and SEL_UC == 1 << SEL_UC_SHIFT and SEL_KC % QB == 0
    assert nc % CMP_ROWS == 0 and CMP_ROWS == 1 << CMP_ROWS_SHIFT and CMP_STRIDE == 1 << CMP_STRIDE_SHIFT
    cs = np.arange(nc)[None, :] * CMP_STRIDE
    ss = np.arange(LANES)[:, None] * SLC_LEN
    ovm = (cs < ss + SLC_LEN) & (cs + CMP_LEN - 1 >= ss) & (np.arange(LANES)[:, None] < n_sel) \
        & (np.arange(nc)[None, :] < nc - 1)
    ovt = jnp.asarray(ovm, BF16)
    rows_in = lambda n, w: pl.BlockSpec((1, 1, n, w), lambda b, g, i: (b, g, 0, 0))
    cols = R * QB
    return pl.pallas_call(
        functools.partial(_nsa_kernel, n_sel=n_sel),
        grid=(B, G, S // QB),
        in_specs=[pl.BlockSpec((1, 1, 1, LANES, cols), lambda b, g, i: (b, g, i, 0, 0)),
                  rows_in(nc, LANES), rows_in(dk, nc), rows_in(S, 2 * LANES), rows_in(dk, S),
                  rows_in(S, LANES), rows_in(dk, S),
                  pl.BlockSpec((1, 1, GATE_ROWS, QB), lambda b, g, i: (b, g, 0, i)),
                  pl.BlockSpec((LANES, nc), lambda b, g, i: (0, 0))],
        out_specs=pl.BlockSpec((1, QB, R * dk), lambda b, g, i: (b, i, g)),
        out_shape=jax.ShapeDtypeStruct((B, S, NSA_HEADS * dk), BF16),
        scratch_shapes=[pltpu.VMEM((2 * LANES, cols), BF16)] +
                       [pltpu.VMEM((1, cols // 2), F32)] * 4 + [pltpu.VMEM((dk, cols // 2), F32)] * 2 +
                       [pltpu.VMEM((2 * SEL_UC, SEL_KC, cols // 2), F32)] * 2 +
                       [pltpu.VMEM((dk, cols), F32), pltpu.VMEM((LANES, QB), F32)],
        compiler_params=_cparams(("parallel", "parallel", "arbitrary")),
        name="nsa_attn",
    )(qt, kc, vct, ksa, vst, kw, vwt, gtt, ovt)


HG_TB = 256
HG_LEVELS = 6


def _hgrn_consts():
    C = HG_CHUNK
    M = np.zeros((HG_LEVELS + 2, C, C), np.float32)
    RM = np.zeros((HG_LEVELS, C, 1), np.float32)
    BM = np.zeros((HG_LEVELS + 1, C, C), np.float32)
    for lv in range(HG_LEVELS):
        m = 1 << lv
        for t in range(C):
            bs = (t // (2 * m)) * (2 * m)
            mid = bs + m - 1
            if t - bs >= m:
                RM[lv, t, 0] = 1.0
                M[lv, t, mid + 1:t + 1] = 1.0
            else:
                M[lv, t, t + 1:mid + 1] = 1.0
            BM[lv, t, bs:bs + 2 * m] = 1.0
    BM[HG_LEVELS] = np.eye(C, dtype=np.float32)
    M[HG_LEVELS] = np.tril(np.ones((C, C), np.float32))
    M[HG_LEVELS + 1] = np.triu(np.ones((C, C), np.float32), 1)
    mm = M.reshape(-1, C)
    mm3 = np.concatenate([mm, mm, mm], axis=1)
    rm = np.broadcast_to(RM, (HG_LEVELS, C, HG_HEADS * HG_DIM)).copy()
    return jnp.asarray(mm3, BF16), jnp.asarray(rm, F32), jnp.asarray(BM, F32)


def _hgrn_kernel(q_ref, f_ref, i_ref, g_ref, lbl_ref, gn_ref, mm_ref, rm_ref, bm_ref, o_ref, st_sc):
    C, D = HG_CHUNK, HG_DIM

    @pl.when(pl.program_id(1) == 0)
    def _():
        st_sc[...] = jnp.zeros(st_sc.shape, F32)

    lbl = lbl_ref[...]
    mxl = jnp.max(lbl, axis=0, keepdims=True)
    el = jnp.exp(lbl - mxl)
    lb_all = el[0:1, :] / jnp.sum(el, axis=0, keepdims=True)
    mm = mm_ref[...]
    gn = gn_ref[...]
    heads = [slice(h * D, (h + 1) * D) for h in range(HG_HEADS)]
    nt = lambda a, b: lax.dot_general(a, b, NT_DIMS, preferred_element_type=F32)
    for ch in range(HG_TB // C):
        rs = slice(ch * C, (ch + 1) * C)
        q = q_ref[0, rs, :].astype(F32)
        v = i_ref[0, rs, :]
        gg = g_ref[0, rs, :].astype(F32)
        f = lb_all + (1.0 - lb_all) * jax.nn.sigmoid(f_ref[0, rs, :].astype(F32))
        lg = jnp.log(f)
        kk = 1.0 - f
        g1 = lg.astype(BF16)
        r1 = lg - g1.astype(F32)
        g2 = r1.astype(BF16)
        g3 = (r1 - g2.astype(F32)).astype(BF16)
        ez = jnp.exp(jnp.dot(mm, jnp.concatenate([g1, g2, g3], axis=0), preferred_element_type=F32))
        qb, kb = q.astype(BF16), kk.astype(BF16)
        attn = [bm_ref[HG_LEVELS] * nt(qb[:, hs], kb[:, hs]) for hs in heads]
        for lv in range(HG_LEVELS):
            zl = ez[lv * C:(lv + 1) * C]
            rm = rm_ref[lv]
            ql = (q * zl * rm).astype(BF16)
            kl = (kk * zl * (1.0 - rm)).astype(BF16)
            attn = [a + bm_ref[lv] * nt(ql[:, hs], kl[:, hs]) for a, hs in zip(attn, heads)]
        zc = ez[HG_LEVELS * C:(HG_LEVELS + 1) * C]
        zs = ez[(HG_LEVELS + 1) * C:(HG_LEVELS + 2) * C]
        qz = (q * zc).astype(BF16)
        kz = (kk * zs).astype(BF16)
        sts = [st_sc[h] for h in range(HG_HEADS)]
        o_intra = [jnp.dot(a.astype(BF16), v[:, hs], preferred_element_type=F32) for a, hs in zip(attn, heads)]
        o_inter = [nt(qz[:, hs], st.astype(BF16)) for st, hs in zip(sts, heads)]
        upd = [lax.dot_general(v[:, hs], kz[:, hs], TN_DIMS, preferred_element_type=F32) for hs in heads]
        for h, hs in enumerate(heads):
            st_sc[h] = zc[C - 1:C, hs] * sts[h] + upd[h]
            o = o_intra[h] + o_inter[h]
            y = o * lax.rsqrt(jnp.mean(o * o, axis=-1, keepdims=True) + RMS_EPS) * gn
            g = gg[:, hs]
            o_ref[0, rs, hs] = (y * (g * jax.nn.sigmoid(g))).astype(o_ref.dtype)


def _hgrn(hq, hf, hi, hg, lb_logits, hg_norm, B, S):
    W = HG_HEADS * HG_DIM
    mm3, rm, bm = _hgrn_consts()
    r3 = lambda a: a.reshape(B, S, W)
    tok = pl.BlockSpec((1, HG_TB, W), lambda b, j: (b, j, 0))
    const = lambda a: pl.BlockSpec(a.shape, lambda b, j: (0,) * a.ndim)
    gn = hg_norm.reshape(1, HG_DIM)
    return pl.pallas_call(
        _hgrn_kernel,
        grid=(B, S // HG_TB),
        in_specs=[tok, tok, tok, tok, const(lb_logits), const(gn), const(mm3), const(rm), const(bm)],
        out_specs=tok,
        out_shape=jax.ShapeDtypeStruct((B, S, W), BF16),
        scratch_shapes=[pltpu.VMEM((HG_HEADS, HG_DIM, HG_DIM), F32)],
        compiler_params=_cparams(("parallel", "arbitrary")),
        name="hgrn2",
    )(r3(hq), r3(hf), r3(hi), r3(hg), lb_logits, gn, mm3, rm, bm)


MG_TM = 256


SUBLANES = 8


def _to_row_tiles(ref, x, lead=()):
    for s in range(SUBLANES):
        ref[lead + (slice(None), s, slice(None))] = x[:, s * LANES:(s + 1) * LANES]


def _from_row_tiles(ref, lead=()):
    return jnp.concatenate([ref[lead + (slice(None), s, slice(None))] for s in range(SUBLANES)], axis=1)


def _first_lane(cond, lane_f):
    return jnp.min(jnp.where(cond, lane_f, float(LANES)), axis=1, keepdims=True)


def _merge_kernel(on_ref, oh_ref, mg_ref, x_ref, wn_ref, wh_ref, wo_ref, fn_ref, wr1_ref, wr2_ref, br_ref,
                  x1_ref, h2_ref, rt_ref):
    D = x_ref.shape[1]
    a = jnp.dot(on_ref[...], wn_ref[...], preferred_element_type=F32)
    b = jnp.dot(oh_ref[...], wh_ref[...], preferred_element_type=F32)
    mixed = (jax.nn.sigmoid(mg_ref[:, 0:D].astype(F32)) * a +
             jax.nn.sigmoid(mg_ref[:, D:2 * D].astype(F32)) * b)
    x1 = x_ref[...] + jnp.dot(mixed.astype(BF16), wo_ref[...], preferred_element_type=F32)
    x1_ref[...] = x1
    h2 = x1 * lax.rsqrt(jnp.mean(x1 * x1, axis=-1, keepdims=True) + RMS_EPS) * fn_ref[...]
    _to_row_tiles(h2_ref, h2)
    h_hi = h2.astype(BF16)
    h_lo = (h2 - h_hi.astype(F32)).astype(BF16)
    logits = (jnp.dot(h_lo, wr1_ref[...], preferred_element_type=F32) +
              jnp.dot(h_hi, wr2_ref[...], preferred_element_type=F32) +
              jnp.dot(h_hi, wr1_ref[...], preferred_element_type=F32)) + br_ref[...]
    lane = lax.broadcasted_iota(jnp.int32, logits.shape, 1)
    ninf = -jnp.inf
    gl = jnp.where(lane < N_GROUPS, logits, ninf)
    gmax = jnp.max(gl, axis=1, keepdims=True)
    lane_f = lane.astype(F32)
    gsel = _first_lane(gl == gmax, lane_f)
    gprob = 1.0 / jnp.sum(jnp.exp(gl - gmax), axis=1, keepdims=True)
    lo = N_GROUPS + gsel * EXPERTS_PER_GROUP
    el = jnp.where((lane_f >= lo) & (lane_f < lo + EXPERTS_PER_GROUP), logits, ninf)
    v1 = jnp.max(el, axis=1, keepdims=True)
    i1 = _first_lane(el == v1, lane_f)
    el2 = jnp.where(lane_f == i1, ninf, el)
    v2 = jnp.max(el2, axis=1, keepdims=True)
    i2 = _first_lane(el2 == v2, lane_f)
    e2 = jnp.exp(v2 - v1)
    w1 = gprob / (1.0 + e2)
    w2 = gprob * e2 / (1.0 + e2)
    rt = jnp.where(lane == 0, i1 - N_GROUPS,
                   jnp.where(lane == 1, i2 - N_GROUPS,
                             jnp.where(lane == 2, w1, jnp.where(lane == 3, w2, 0.0))))
    rt_ref[...] = rt


def _merge(o_nsa, o_hg, mg, x2, w_br_nsa, w_br_hg, w_out, ffn_norm, w_grp, b_grp, w_rtr, b_rtr):
    T, D = x2.shape
    nr = N_GROUPS + N_EXPERTS
    wr = jnp.pad(jnp.concatenate([w_grp, w_rtr], axis=1), ((0, 0), (0, LANES - nr)))
    wr1 = wr.astype(BF16)
    wr2 = (wr - wr1.astype(F32)).astype(BF16)
    br = jnp.pad(jnp.concatenate([b_grp, b_rtr]), (0, LANES - nr)).reshape(1, LANES)
    tok = lambda w: pl.BlockSpec((MG_TM, w), lambda i: (i, 0))
    const = lambda a: pl.BlockSpec(a.shape, lambda i: (0,) * a.ndim)
    wn, wh, wo = w_br_nsa.astype(BF16), w_br_hg.astype(BF16), w_out.astype(BF16)
    fn = ffn_norm.reshape(1, D)
    return pl.pallas_call(
        _merge_kernel,
        grid=(T // MG_TM,),
        in_specs=[tok(o_nsa.shape[1]), tok(o_hg.shape[1]), tok(2 * D), tok(D),
                  const(wn), const(wh), const(wo), const(fn), const(wr1), const(wr2), const(br)],
        out_specs=[tok(D), pl.BlockSpec((MG_TM, SUBLANES, D // SUBLANES), lambda i: (i, 0, 0)), tok(LANES)],
        out_shape=[jax.ShapeDtypeStruct((T, D), F32), jax.ShapeDtypeStruct((T, SUBLANES, D // SUBLANES), F32),
                   jax.ShapeDtypeStruct((T, LANES), F32)],
        compiler_params=_cparams(("parallel",)),
        name="merge",
    )(o_nsa, o_hg, mg, x2, wn, wh, wo, fn, wr1, wr2, br)


MOE_TM = 256


def _moe_kernel(te_ref, nv_ref, src_ref, srcn_ref, dst_ref, h_hbm, wg_ref, wu_ref, wd_ref, y_hbm,
                xbuf, ybuf, wgb, wub, wdb, gsem, ssem):
    i = pl.program_id(0)
    n = pl.num_programs(0)
    tm = xbuf.shape[1]
    slot = i & 1
    prv = jnp.maximum(i - 1, 0)
    nv_i = nv_ref[i]
    nv_prev = jnp.where(i > 0, nv_ref[prv], 0)
    nv_next = jnp.where(i + 1 < n, nv_ref[jnp.minimum(i + 1, n - 1)], 0)

    def gather_copy(idx_ref, r, sl):
        return pltpu.make_async_copy(h_hbm.at[idx_ref[0, 0, r]], xbuf.at[sl, r], gsem.at[sl])

    def scatter_copy(row, r, sl):
        return pltpu.make_async_copy(ybuf.at[sl, r], y_hbm.at[row], ssem.at[sl])

    def rows(count, fn):
        groups = jnp.right_shift(count, 3)

        def body8(g, c):
            for k in range(8):
                fn(g * 8 + k)
            return c

        def body1(r, c):
            fn(r)
            return c

        lax.fori_loop(0, groups, body8, 0)
        lax.fori_loop(groups * 8, count, body1, 0)

    def wait_rows(count, copy_of):
        k = tm
        while k >= 1:
            pl.when((count & k) != 0)(lambda k=k: copy_of(k).wait())
            k //= 2

    def gather_block(k, sl):
        return pltpu.make_async_copy(h_hbm.at[pl.ds(0, k)], xbuf.at[sl, pl.ds(0, k)], gsem.at[sl])

    def scatter_block(k, sl):
        return pltpu.make_async_copy(ybuf.at[sl, pl.ds(0, k)], y_hbm.at[pl.ds(0, k)], ssem.at[sl])

    @pl.when(i == 0)
    def _():
        xbuf[...] = jnp.zeros(xbuf.shape, F32)
        rows(nv_i, lambda r: gather_copy(src_ref, r, 0).start())

    rows(nv_next, lambda r: gather_copy(srcn_ref, r, 1 - slot).start())

    @pl.when((i == 0) | (te_ref[i] != te_ref[prv]))
    def _():
        wgb[...] = wg_ref[0].astype(BF16)
        wub[...] = wu_ref[0].astype(BF16)
        wdb[...] = wd_ref[0].astype(BF16)

    @pl.when(nv_i > 0)
    def _():
        wait_rows(nv_i, lambda k: gather_block(k, slot))
        x = _from_row_tiles(xbuf, (slot,)).astype(BF16)
        hg = jnp.dot(x, wgb[...], preferred_element_type=F32)
        hu = jnp.dot(x, wub[...], preferred_element_type=F32)
        hid = (hg * jax.nn.sigmoid(hg) * hu).astype(BF16)
        _to_row_tiles(ybuf, jnp.dot(hid, wdb[...], preferred_element_type=F32), (slot,))
        rows(nv_i, lambda r: scatter_copy(dst_ref[0, 0, r], r, slot).start())

    wait_rows(nv_prev, lambda k: scatter_block(k, 1 - slot))

    @pl.when(i == n - 1)
    def _():
        wait_rows(nv_i, lambda k: scatter_block(k, slot))


def _moe(h2, route, w_gate, w_up, w_down):
    T = h2.shape[0]
    E, D, F = w_gate.shape
    tm = MOE_TM
    n_tiles = 2 * T // tm + E
    i32 = jnp.int32
    ex = jnp.concatenate([route[:, 0], route[:, 1]]).astype(i32)
    order = jnp.argsort(ex, stable=True).astype(i32)
    counts = jnp.sum(ex[:, None] == jnp.arange(E, dtype=i32)[None, :], axis=0).astype(i32)
    padded = ((counts + tm - 1) // tm) * tm
    pend = jnp.cumsum(padded)
    pstart = pend - padded
    cstart = jnp.cumsum(counts) - counts
    tile_start = jnp.arange(n_tiles, dtype=i32) * tm
    te_raw = jnp.sum(tile_start[:, None] >= pend[None, :], axis=1).astype(i32)
    active = te_raw < E
    te_c = jnp.minimum(te_raw, E - 1)
    off = tile_start - pstart[te_c]
    nv = jnp.where(active, jnp.clip(counts[te_c] - off, 0, tm), 0).astype(i32)
    n_active = jnp.sum(active.astype(i32))
    te = jnp.where(active, te_c, te_c[jnp.maximum(n_active - 1, 0)])
    rank = off[:, None] + jnp.arange(tm, dtype=i32)[None, :]
    a = order[jnp.clip(cstart[te_c][:, None] + rank, 0, 2 * T - 1)]
    valid = jnp.arange(tm, dtype=i32)[None, :] < nv[:, None]
    src = jnp.where(valid, a % T, 0).reshape(n_tiles, 1, tm)
    dst = jnp.where(valid, a, 0).reshape(n_tiles, 1, tm)

    idx_spec = lambda f: pl.BlockSpec((1, 1, tm), lambda i, te, nv: (f(i), 0, 0), memory_space=pltpu.SMEM)
    wspec = lambda r, c: pl.BlockSpec((1, r, c), lambda i, te, nv: (te[i], 0, 0))
    grid_spec = pltpu.PrefetchScalarGridSpec(
        num_scalar_prefetch=2,
        grid=(n_tiles,),
        in_specs=[idx_spec(lambda i: i), idx_spec(lambda i: jnp.minimum(i + 1, n_tiles - 1)),
                  idx_spec(lambda i: i), pl.BlockSpec(memory_space=pl.ANY),
                  wspec(D, F), wspec(D, F), wspec(F, D)],
        out_specs=pl.BlockSpec(memory_space=pl.ANY),
        scratch_shapes=[pltpu.VMEM((2, tm, SUBLANES, D // SUBLANES), F32),
                        pltpu.VMEM((2, tm, SUBLANES, D // SUBLANES), F32),
                        pltpu.VMEM((D, F), BF16), pltpu.VMEM((D, F), BF16), pltpu.VMEM((F, D), BF16),
                        pltpu.SemaphoreType.DMA((2,)), pltpu.SemaphoreType.DMA((2,))],
    )
    return pl.pallas_call(
        _moe_kernel,
        grid_spec=grid_spec,
        out_shape=jax.ShapeDtypeStruct((2 * T, SUBLANES, D // SUBLANES), F32),
        compiler_params=_cparams(("arbitrary",)),
        name="moe",
    )(te, nv, src, src, dst, h2, w_gate, w_up, w_down)


FIN_TM = 256


def _final_kernel(x1_ref, y1_ref, y2_ref, rt_ref, g_ref, o_ref):
    rt = rt_ref[...]
    x = x1_ref[...] + rt[:, 2:3] * _from_row_tiles(y1_ref) + rt[:, 3:4] * _from_row_tiles(y2_ref)
    o_ref[...] = x * lax.rsqrt(jnp.mean(x * x, axis=-1, keepdims=True) + RMS_EPS) * g_ref[...]


def _final(x1, y, route, final_norm):
    T, D = x1.shape
    nb = T // FIN_TM
    ytile = (FIN_TM, SUBLANES, D // SUBLANES)
    return pl.pallas_call(
        _final_kernel,
        grid=(nb,),
        in_specs=[pl.BlockSpec((FIN_TM, D), lambda i: (i, 0)),
                  pl.BlockSpec(ytile, lambda i: (i, 0, 0)),
                  pl.BlockSpec(ytile, lambda i: (i + nb, 0, 0)),
                  pl.BlockSpec((FIN_TM, LANES), lambda i: (i, 0)),
                  pl.BlockSpec((1, D), lambda i: (0, 0))],
        out_specs=pl.BlockSpec((FIN_TM, D), lambda i: (i, 0)),
        out_shape=jax.ShapeDtypeStruct((T, D), F32),
        compiler_params=_cparams(("parallel",)),
        name="final",
    )(x1, y, y, route, final_norm.reshape(1, D))


def kernel(x, attn_norm, w_in, w_cmp_k, w_cmp_v, cmp_pos, hg_lb_logits, hg_norm, w_br_nsa, w_br_hg, w_out,
           ffn_norm, w_grp, b_grp, w_rtr, b_rtr, w_gate, w_up, w_down, final_norm):
    B, S, D = x.shape
    assert attn_norm.shape[0] == 1, "single-layer block"
    x2 = x.reshape(B * S, D)
    (kcin, vcin, hq, hf, hi, hg, mg), (qt, ksa, vst, kw, vwt, gtt) = _in_proj(x2, attn_norm[0], w_in[0], B, S)
    kc, vct = _compress(kcin, vcin, w_cmp_k[0], w_cmp_v[0], cmp_pos[0], B, S)
    o_nsa = _nsa_attn(qt, kc, vct, ksa, vst, kw, vwt, gtt, B, S).reshape(B * S, -1)
    o_hg = _hgrn(hq, hf, hi, hg, hg_lb_logits, hg_norm[0], B, S).reshape(B * S, -1)
    x1, h2, route = _merge(o_nsa, o_hg, mg, x2, w_br_nsa[0], w_br_hg[0], w_out[0], ffn_norm[0],
                           w_grp[0], b_grp[0], w_rtr[0], b_rtr[0])
    y = _moe(h2, route, w_gate[0], w_up[0], w_down[0])
    out = _final(x1, y, route, final_norm)
    return out.reshape(B, S, D)
```

```python
import functools

import numpy as np
import jax
import jax.numpy as jnp
from jax import lax
from jax.experimental import pallas as pl
from jax.experimental.pallas import tpu as pltpu

F32 = jnp.float32
BF16 = jnp.bfloat16

NSA_HEADS = 8
NSA_KV_GROUPS = 2
NSA_HEAD_DIM = 64
NSA_R = NSA_HEADS // NSA_KV_GROUPS
CMP_LEN = 32
CMP_STRIDE = 16
SLC_LEN = 64
SLC_SHIFT = 6
SLC_TOPN = 16
WINDOW = 512
Q_BLOCK = 128
ROPE_THETA = 500000.0
ROPE_DIM = NSA_HEAD_DIM // 4
HG_HEADS = 4
HG_DIM = 128
HG_CHUNK = 64
N_GROUPS = 4
EXPERTS_PER_GROUP = 8
N_EXPERTS = N_GROUPS * EXPERTS_PER_GROUP
RMS_EPS = 1e-6
NEG = -1e30
FORCE_SCORE = 1e4

LANES = 128
VMEM_LIMIT = 56 * 1024 * 1024

NT_DIMS = (((1,), (1,)), ((), ()))
TN_DIMS = (((0,), (0,)), ((), ()))


def _cparams(sem):
    return pltpu.CompilerParams(dimension_semantics=sem, vmem_limit_bytes=VMEM_LIMIT)


IN_TM = 256
IN_SEGS = (("q", 512, 512), ("kc", 128, 128), ("vc", 128, 128), ("kv", 512, 512), ("gate", 24, 128),
           ("hq", 512, 512), ("hf", 512, 512), ("hi", 512, 512), ("hg", 512, 512), ("merge", 2048, 2048))
IN_F32 = ("kc", "vc")
IN_PREP = ("q", "kv", "gate")


def _in_proj_kernel(x_ref, g_ref, w_ref, c_ref, s1_ref, s2_ref, nege_ref, *out_refs):
    n_plain = len(IN_SEGS) - len(IN_PREP)
    plain_refs, prep_refs = iter(out_refs[:n_plain]), out_refs[n_plain:]
    x = x_ref[...]
    h = x * lax.rsqrt(jnp.mean(x * x, axis=-1, keepdims=True) + RMS_EPS) * g_ref[...]
    h = h.astype(BF16)
    off, kept = 0, {}
    for name, _, wpad in IN_SEGS:
        y = jnp.dot(h, w_ref[:, off:off + wpad], preferred_element_type=F32)
        off += wpad
        if name in IN_PREP:
            kept[name] = y
            if len(kept) == len(IN_PREP):
                _prep_body(kept["q"], kept["kv"], kept["gate"], c_ref[...], s1_ref[...], s2_ref[...],
                           nege_ref[...], *prep_refs)
        else:
            o_ref = next(plain_refs)
            o_ref[...] = y.astype(o_ref.dtype)


def _in_proj(x2, attn_norm, w_in, B, S):
    T, D = x2.shape
    G, dk, QB = NSA_KV_GROUPS, NSA_HEAD_DIM, Q_BLOCK
    pieces, off = [], 0
    for _, w, wpad in IN_SEGS:
        p = w_in[:, off:off + w]
        if wpad != w:
            p = jnp.pad(p, ((0, 0), (0, wpad - w)))
        pieces.append(p)
        off += w
    wcat = jnp.concatenate(pieces, axis=1).astype(BF16)
    NP = wcat.shape[1]
    plain = [(jax.ShapeDtypeStruct((T, wpad), F32 if name in IN_F32 else BF16),
              pl.BlockSpec((IN_TM, wpad), lambda i: (i, 0)))
             for name, _, wpad in IN_SEGS if name not in IN_PREP]
    nj = S // IN_TM
    c, s1, s2 = _rope_tables(jnp.arange(S))
    key = np.arange(S)
    nege = jnp.asarray(np.where(key[:, None] // SLC_LEN == np.arange(LANES)[None, :], NEG, 0.0), BF16)
    tab = pl.BlockSpec((IN_TM, LANES), lambda i: (i % nj, 0))
    rows_out = lambda w, dt: (jax.ShapeDtypeStruct((B, G, S, w), dt),
                              pl.BlockSpec((1, G, IN_TM, w), lambda i: (i // nj, 0, i % nj, 0)))
    cols_out = lambda n, dt: (jax.ShapeDtypeStruct((B, G, n, S), dt),
                              pl.BlockSpec((1, G, n, IN_TM), lambda i: (i // nj, 0, 0, i % nj)))
    qt_out = (jax.ShapeDtypeStruct((B, G, S // QB, LANES, NSA_R * QB), BF16),
              pl.BlockSpec((1, G, IN_TM // QB, LANES, NSA_R * QB), lambda i: (i // nj, 0, i % nj, 0, 0)))
    prep = [qt_out, rows_out(2 * LANES, BF16), cols_out(dk, BF16), rows_out(LANES, BF16),
            cols_out(dk, BF16), cols_out(GATE_ROWS, F32)]
    outs = pl.pallas_call(
        _in_proj_kernel,
        grid=(T // IN_TM,),
        in_specs=[pl.BlockSpec((IN_TM, D), lambda i: (i, 0)),
                  pl.BlockSpec((1, D), lambda i: (0, 0)),
                  pl.BlockSpec((D, NP), lambda i: (0, 0)), tab, tab, tab, tab],
        out_specs=[o[1] for o in plain + prep],
        out_shape=[o[0] for o in plain + prep],
        compiler_params=_cparams(("parallel",)),
        name="in_proj",
    )(x2, attn_norm.reshape(1, D), wcat, c, s1, s2, nege)
    return outs[:len(plain)], outs[len(plain):]


def _rope_tables(pos):
    half = ROPE_DIM // 2
    d = np.arange(LANES) % NSA_HEAD_DIM
    inv_freq = ROPE_THETA ** (-jnp.arange(half, dtype=F32) / half)
    lane_freq = jnp.where(d < ROPE_DIM, inv_freq[d % half], 0.0)
    ang = pos.astype(F32)[:, None] * lane_freq[None, :]
    cos, sin = jnp.cos(ang), jnp.sin(ang)
    s1 = jnp.where(d < half, -sin, 0.0)
    s2 = jnp.where((d >= half) & (d < ROPE_DIM), sin, 0.0)
    return cos, s1, s2


def _rope(x, c, s1, s2):
    half = ROPE_DIM // 2
    return x * c + pltpu.roll(x, LANES - half, 1) * s1 + pltpu.roll(x, half, 1) * s2


GATE_ROWS = 16
LOG2E = 1.4426950408889634


def _prep_body(q, kv, gate, c, s1, s2, nege, qt_ref, ksa_ref, vst_ref, kw_ref, vwt_ref, gtt_ref):
    lane = lax.broadcasted_iota(jnp.int32, c.shape, 1)
    lo = lane < NSA_HEAD_DIM
    dk, QB = NSA_HEAD_DIM, Q_BLOCK

    def split(x):
        return (jnp.where(lo, x, 0.0), jnp.where(lo, pltpu.roll(x, dk, 1), 0.0))

    scale = (dk ** -0.5) * LOG2E
    for blk in range(NSA_HEADS // 2):
        x = q[:, blk * LANES:(blk + 1) * LANES]
        for hh, part in enumerate(split(_rope(x, c, s1, s2) * scale)):
            h = 2 * blk + hh
            g, r = h // NSA_R, h % NSA_R
            pt = part.T.astype(BF16)
            for i in range(q.shape[0] // QB):
                qt_ref[0, g, i, :, r * QB:(r + 1) * QB] = pt[:, i * QB:(i + 1) * QB]
    for g, part in enumerate(split(_rope(kv[:, 0:LANES], c, s1, s2))):
        ksa_ref[0, g] = jnp.concatenate([part.astype(BF16), nege], axis=1)
    for g, part in enumerate(split(_rope(kv[:, 2 * LANES:3 * LANES], c, s1, s2))):
        kw_ref[0, g] = part.astype(BF16)
    for blk, ref in ((1, vst_ref), (3, vwt_ref)):
        for g, part in enumerate(split(kv[:, blk * LANES:(blk + 1) * LANES])):
            ref[0, g] = part.T[0:dk, :].astype(BF16)
    sg = jax.nn.sigmoid(gate)
    gtt_ref[0, 0] = sg.T[0:GATE_ROWS, :]
    gtt_ref[0, 1] = pltpu.roll(sg, LANES - 3 * NSA_R, 1).T[0:GATE_ROWS, :]


def _compress_kernel(tk_ref, tv_ref, wk_ref, wv_ref, pos_ref, c_ref, s1_ref, s2_ref, kc_ref, vct_ref):
    nc = c_ref.shape[0]
    dk = NSA_HEAD_DIM
    lane = lax.broadcasted_iota(jnp.int32, (nc, LANES), 1)
    lo = lane < dk

    def comp(t_ref, w_ref):
        a = jnp.zeros((nc, LANES), F32)
        b = jnp.zeros((nc, LANES), F32)
        c0 = jnp.zeros((SUBLANES, LANES), F32)
        for j in range(CMP_STRIDE):
            u = t_ref[pl.ds(j, nc, stride=CMP_STRIDE), :].astype(BF16)
            a = a + jnp.dot(u, w_ref[j], preferred_element_type=F32)
            b = b + jnp.dot(u, w_ref[CMP_STRIDE + j], preferred_element_type=F32)
        for l in range(CMP_LEN):
            c0 = c0 + jnp.dot(pos_ref[l], w_ref[l], preferred_element_type=F32)
        return a + pltpu.roll(b, nc - 1, 0) + c0[0:1, :]

    def split(x):
        return (jnp.where(lo, x, 0.0), jnp.where(lo, pltpu.roll(x, dk, 1), 0.0))

    for g, part in enumerate(split(_rope(comp(tk_ref, wk_ref), c_ref[...], s1_ref[...], s2_ref[...]))):
        kc_ref[0, g] = part.astype(BF16)
    for g, part in enumerate(split(comp(tv_ref, wv_ref))):
        vct_ref[0, g] = part.T[0:dk, :].astype(BF16)


def _compress(kcin, vcin, w_cmp_k, w_cmp_v, cmp_pos, B, S):
    G, dk = NSA_KV_GROUPS, NSA_HEAD_DIM
    nc = S // CMP_STRIDE

    def block_diag(w):
        w3 = w.reshape(CMP_LEN, dk, dk)
        z = jnp.zeros_like(w3)
        return jnp.concatenate([jnp.concatenate([w3, z], axis=2),
                                jnp.concatenate([z, w3], axis=2)], axis=1).astype(BF16)

    pos = jnp.concatenate([cmp_pos, cmp_pos], axis=1)
    pos = jnp.broadcast_to(pos[:, None, :], (CMP_LEN, SUBLANES, G * dk)).astype(BF16)
    c, s1, s2 = _rope_tables(jnp.arange(nc) * CMP_STRIDE)
    tspec = pl.BlockSpec((S, LANES), lambda b: (b, 0))
    wspec = pl.BlockSpec((CMP_LEN, LANES, LANES), lambda b: (0, 0, 0))
    tab = pl.BlockSpec((nc, LANES), lambda b: (0, 0))
    return pl.pallas_call(
        _compress_kernel,
        grid=(B,),
        in_specs=[tspec, tspec, wspec, wspec,
                  pl.BlockSpec((CMP_LEN, SUBLANES, LANES), lambda b: (0, 0, 0)), tab, tab, tab],
        out_specs=[pl.BlockSpec((1, G, nc, LANES), lambda b: (b, 0, 0, 0)),
                   pl.BlockSpec((1, G, dk, nc), lambda b: (b, 0, 0, 0))],
        out_shape=[jax.ShapeDtypeStruct((B, G, nc, LANES), BF16),
                   jax.ShapeDtypeStruct((B, G, dk, nc), BF16)],
        compiler_params=_cparams(("parallel",)),
        name="compress",
    )(kcin, vcin, block_diag(w_cmp_k), block_diag(w_cmp_v), pos, c, s1, s2)


CMP_STRIDE_SHIFT = 4
CMP_ROWS = 128
CMP_ROWS_SHIFT = 7
SEL_KC = 256
SEL_KC_SHIFT = 8
SEL_UC = 2
SEL_UC_SHIFT = 1
WIN_KEYS = WINDOW + Q_BLOCK


def _col_reduce(x, op, fin):
    n = x.shape[0]
    while (n // 2) % 8 == 0 and n > 8:
        n //= 2
        x = op(x[:n], x[n:])
    return fin(x, axis=0, keepdims=True)


def _col_max(x):
    return _col_reduce(x, jnp.maximum, jnp.max)


def _col_min(x):
    return _col_reduce(x, jnp.minimum, jnp.min)


def _col_sum(x):
    return _col_reduce(x, jnp.add, jnp.sum)


def _softmax_cols(s):
    e = jnp.exp2(s - _col_max(s))
    return e, _col_sum(e)


def _nsa_kernel(qt_ref, kc_ref, vct_ref, ksa_ref, vst_ref, kw_ref, vwt_ref, gtt_ref, ovt_ref,
                o_ref, rhs_sc, m0_sc, m1_sc, l0_sc, l1_sc, a0_sc, a1_sc, sa_sc, sb_sc, ocmp_sc, imp_sc,
                *, n_sel):
    m_scs, l_scs, acc_scs = (m0_sc, m1_sc), (l0_sc, l1_sc), (a0_sc, a1_sc)
    s_bufs = (sa_sc, sb_sc)
    R, QB = NSA_R, Q_BLOCK
    cols = R * QB
    i = pl.program_id(2)
    t0 = i * QB
    qt = qt_ref[0, 0, 0]

    def per_head(x):
        return jnp.concatenate([x] * (cols // x.shape[1]), axis=1)

    def rc(n):
        return (lax.broadcasted_iota(jnp.int32, (n, QB), 0),
                t0 + lax.broadcasted_iota(jnp.int32, (n, QB), 1))

    nc = kc_ref.shape[2]

    def cmp_branch(nr):
        sc = jnp.dot(kc_ref[0, 0, 0:nr, :], qt, preferred_element_type=F32)
        n_idx, tq_c = rc(nr)
        cvis = (n_idx * CMP_STRIDE + (CMP_LEN - 1)) <= tq_c
        ec, lc = _softmax_cols(sc + per_head(jnp.where(cvis, 0.0, NEG)))
        pc = ec * per_head(jnp.where(cvis, 1.0, 0.0)) * (1.0 / lc)
        ocmp_sc[...] = jnp.dot(vct_ref[0, 0, :, 0:nr], pc.astype(BF16), preferred_element_type=F32)
        psum = pc[:, 0:QB] + pc[:, QB:2 * QB] + pc[:, 2 * QB:3 * QB] + pc[:, 3 * QB:4 * QB]
        p_hi = psum.astype(BF16)
        p_lo = (psum - p_hi.astype(F32)).astype(BF16)
        ovt = ovt_ref[:, 0:nr]
        imp_sc[...] = (jnp.dot(ovt, p_hi, preferred_element_type=F32) +
                       jnp.dot(ovt, p_lo, preferred_element_type=F32))

    n_var = nc // CMP_ROWS
    last_vis = jnp.right_shift(t0 + QB - CMP_LEN, CMP_STRIDE_SHIFT)
    var = jnp.minimum(jnp.right_shift(jnp.maximum(last_vis, 0), CMP_ROWS_SHIFT), n_var - 1)
    for k in range(n_var):
        pl.when(var == k)(functools.partial(cmp_branch, (k + 1) * CMP_ROWS))
    o_cmp = ocmp_sc[...]
    imp = imp_sc[...]

    start = pl.multiple_of(jnp.maximum(t0 - WINDOW, 0), QB)
    sw = jnp.dot(kw_ref[0, 0, pl.ds(start, WIN_KEYS), :], qt, preferred_element_type=F32)

    j = lax.broadcasted_iota(jnp.int32, (LANES, QB), 0)
    tq = t0 + lax.broadcasted_iota(jnp.int32, (LANES, QB), 1)
    force = (j == jnp.right_shift(tq, SLC_SHIFT)) | (j == 0)
    valid = (j * SLC_LEN) <= tq
    score = jnp.where(force, -jnp.inf, jnp.where(valid, imp, -1.0))
    work = jnp.where(j < n_sel, score, -jnp.inf)
    jf = j.astype(F32)
    for _ in range(min(SLC_TOPN, n_sel) - 2):
        mx = _col_max(work)
        idx = _col_min(jnp.where(work == mx, jf, float(LANES)))
        work = jnp.where(jf == idx, -jnp.inf, work)
    notsel = jnp.where(work == -jnp.inf, 0.0, 1.0)

    k_idx, tq_w = rc(WIN_KEYS)
    rel = tq_w - (start + k_idx)
    pw, lw = _softmax_cols(sw + per_head(jnp.where((rel >= 0) & (rel < WINDOW), 0.0, NEG)))
    o_win = jnp.dot(vwt_ref[0, 0, :, pl.ds(start, WIN_KEYS)], pw.astype(BF16),
                    preferred_element_type=F32) * (1.0 / lw)

    rhs_sc[0:LANES, :] = qt
    rhs_sc[LANES:2 * LANES, :] = jnp.concatenate([notsel.astype(BF16)] * R, axis=1)
    halves = tuple(zip(m_scs, l_scs, acc_scs))
    hw = cols // len(halves)
    for m_sc, l_sc, acc_sc in halves:
        m_sc[...] = jnp.full(m_sc.shape, NEG, F32)
        l_sc[...] = jnp.zeros(l_sc.shape, F32)
        acc_sc[...] = jnp.zeros(acc_sc.shape, F32)

    def sel_scores(chunk_ids):
        kas = [ksa_ref[0, 0, pl.ds(pl.multiple_of(c * SEL_KC, SEL_KC), SEL_KC), :] for c in chunk_ids]
        return [[jnp.dot(ka, rhs_sc[:, hp * hw:(hp + 1) * hw], preferred_element_type=F32)
                 for ka in kas] for hp in range(len(halves))]

    def sel_update(chunk_ids, scores, diagonal):
        k0s = [pl.multiple_of(c * SEL_KC, SEL_KC) for c in chunk_ids]
        vts = [vst_ref[0, 0, :, pl.ds(k0, SEL_KC)] for k0 in k0s]
        for hp, (m_sc, l_sc, acc_sc) in enumerate(halves):
            ss = scores[hp]
            if diagonal:
                k_idx, tq_s = rc(SEL_KC)
                ss = [s + jnp.concatenate([jnp.where(k0 + k_idx <= tq_s, 0.0, NEG)] * (hw // QB), axis=1)
                      for k0, s in zip(k0s, ss)]
            m_old = m_sc[...]
            m_new = functools.reduce(jnp.maximum, [_col_max(s) for s in ss], m_old)
            alpha = jnp.exp2(m_old - m_new)
            ps = [jnp.exp2(s - m_new) for s in ss]
            l_sc[...] = alpha * l_sc[...] + functools.reduce(jnp.add, [_col_sum(p) for p in ps])
            pv = [jnp.dot(vt, p.astype(BF16), preferred_element_type=F32) for vt, p in zip(vts, ps)]
            acc_sc[...] = alpha * acc_sc[...] + functools.reduce(jnp.add, pv)
            m_sc[...] = m_new

    n_units = jnp.right_shift(t0, SEL_KC_SHIFT + SEL_UC_SHIFT)
    nh = len(halves)

    def unit(u):
        return [SEL_UC * u + c for c in range(SEL_UC)]

    def put_scores(buf, scores):
        for hp in range(nh):
            for c in range(SEL_UC):
                buf[hp * SEL_UC + c] = scores[hp][c]

    def get_scores(buf):
        return [[buf[hp * SEL_UC + c] for c in range(SEL_UC)] for hp in range(nh)]

    put_scores(s_bufs[0], sel_scores(unit(0)))

    def sel_step(u, carry):
        for parity in range(2):
            @pl.when((u & 1) == parity)
            def _():
                put_scores(s_bufs[1 - parity], sel_scores(unit(u + 1)))
                sel_update(unit(u), get_scores(s_bufs[parity]), False)
        return carry

    lax.fori_loop(0, n_units, sel_step, 0)
    in_unit = jnp.right_shift(t0 & (SEL_UC * SEL_KC - 1), SEL_KC_SHIFT)
    for parity in range(2):
        for last in range(SEL_UC):
            @pl.when(((n_units & 1) == parity) & (in_unit == last))
            def _():
                scores = [s[:last + 1] for s in get_scores(s_bufs[parity])]
                sel_update(unit(n_units)[:last + 1], scores, True)
    o_slc = jnp.concatenate([a[...] * (1.0 / l[...]) for _, l, a in halves], axis=1)

    gtt = gtt_ref[0, 0]
    outs = []
    for r in range(R):
        cs = slice(r * QB, (r + 1) * QB)
        outs.append(gtt[3 * r:3 * r + 1, :] * o_cmp[:, cs] + gtt[3 * r + 1:3 * r + 2, :] * o_slc[:, cs] +
                    gtt[3 * r + 2:3 * r + 3, :] * o_win[:, cs])
    o_ref[0] = jnp.concatenate(outs, axis=0).T.astype(o_ref.dtype)


def _nsa_attn(qt, kc, vct, ksa, vst, kw, vwt, gtt, B, S):
    G, R, QB, dk = NSA_KV_GROUPS, NSA_R, Q_BLOCK, NSA_HEAD_DIM
    nc = S // CMP_STRIDE
    n_sel = S // SLC_LEN
    assert n_sel <= LANES and n_sel >= SLC_TOPN and S % (SEL_UC * SEL_KC) == 0 and S >= WIN_KEYS
    assert SEL_KC == 1 << SEL_KC_SHIFT and SEL_UC == 1 << SEL_UC_SHIFT and SEL_KC % QB == 0
    assert nc % CMP_ROWS == 0 and CMP_ROWS == 1 << CMP_ROWS_SHIFT and CMP_STRIDE == 1 << CMP_STRIDE_SHIFT
    cs = np.arange(nc)[None, :] * CMP_STRIDE
    ss = np.arange(LANES)[:, None] * SLC_LEN
    ovm = (cs < ss + SLC_LEN) & (cs + CMP_LEN - 1 >= ss) & (np.arange(LANES)[:, None] < n_sel) \
        & (np.arange(nc)[None, :] < nc - 1)
    ovt = jnp.asarray(ovm, BF16)
    rows_in = lambda n, w: pl.BlockSpec((1, 1, n, w), lambda b, g, i: (b, g, 0, 0))
    cols = R * QB
    return pl.pallas_call(
        functools.partial(_nsa_kernel, n_sel=n_sel),
        grid=(B, G, S // QB),
        in_specs=[pl.BlockSpec((1, 1, 1, LANES, cols), lambda b, g, i: (b, g, i, 0, 0)),
                  rows_in(nc, LANES), rows_in(dk, nc), rows_in(S, 2 * LANES), rows_in(dk, S),
                  rows_in(S, LANES), rows_in(dk, S),
                  pl.BlockSpec((1, 1, GATE_ROWS, QB), lambda b, g, i: (b, g, 0, i)),
                  pl.BlockSpec((LANES, nc), lambda b, g, i: (0, 0))],
        out_specs=pl.BlockSpec((1, QB, R * dk), lambda b, g, i: (b, i, g)),
        out_shape=jax.ShapeDtypeStruct((B, S, NSA_HEADS * dk), BF16),
        scratch_shapes=[pltpu.VMEM((2 * LANES, cols), BF16)] +
                       [pltpu.VMEM((1, cols // 2), F32)] * 4 + [pltpu.VMEM((dk, cols // 2), F32)] * 2 +
                       [pltpu.VMEM((2 * SEL_UC, SEL_KC, cols // 2), F32)] * 2 +
                       [pltpu.VMEM((dk, cols), F32), pltpu.VMEM((LANES, QB), F32)],
        compiler_params=_cparams(("parallel", "parallel", "arbitrary")),
        name="nsa_attn",
    )(qt, kc, vct, ksa, vst, kw, vwt, gtt, ovt)


HG_TB = 256
HG_LEVELS = 6


def _hgrn_consts():
    C = HG_CHUNK
    M = np.zeros((HG_LEVELS + 2, C, C), np.float32)
    RM = np.zeros((HG_LEVELS, C, 1), np.float32)
    BM = np.zeros((HG_LEVELS + 1, C, C), np.float32)
    for lv in range(HG_LEVELS):
        m = 1 << lv
        for t in range(C):
            bs = (t // (2 * m)) * (2 * m)
            mid = bs + m - 1
            if t - bs >= m:
                RM[lv, t, 0] = 1.0
                M[lv, t, mid + 1:t + 1] = 1.0
            else:
                M[lv, t, t + 1:mid + 1] = 1.0
            BM[lv, t, bs:bs + 2 * m] = 1.0
    BM[HG_LEVELS] = np.eye(C, dtype=np.float32)
    M[HG_LEVELS] = np.tril(np.ones((C, C), np.float32))
    M[HG_LEVELS + 1] = np.triu(np.ones((C, C), np.float32), 1)
    mm = M.reshape(-1, C)
    mm3 = np.concatenate([mm, mm, mm], axis=1)
    rm = np.broadcast_to(RM, (HG_LEVELS, C, HG_HEADS * HG_DIM)).copy()
    return jnp.asarray(mm3, BF16), jnp.asarray(rm, F32), jnp.asarray(BM, F32)


def _hgrn_kernel(q_ref, f_ref, i_ref, g_ref, lbl_ref, gn_ref, mm_ref, rm_ref, bm_ref, o_ref, st_sc):
    C, D = HG_CHUNK, HG_DIM

    @pl.when(pl.program_id(1) == 0)
    def _():
        st_sc[...] = jnp.zeros(st_sc.shape, F32)

    lbl = lbl_ref[...]
    mxl = jnp.max(lbl, axis=0, keepdims=True)
    el = jnp.exp(lbl - mxl)
    lb_all = el[0:1, :] / jnp.sum(el, axis=0, keepdims=True)
    mm = mm_ref[...]
    gn = gn_ref[...]
    heads = [slice(h * D, (h + 1) * D) for h in range(HG_HEADS)]
    nt = lambda a, b: lax.dot_general(a, b, NT_DIMS, preferred_element_type=F32)
    for ch in range(HG_TB // C):
        rs = slice(ch * C, (ch + 1) * C)
        q = q_ref[0, rs, :].astype(F32)
        v = i_ref[0, rs, :]
        gg = g_ref[0, rs, :].astype(F32)
        f = lb_all + (1.0 - lb_all) * jax.nn.sigmoid(f_ref[0, rs, :].astype(F32))
        lg = jnp.log(f)
        kk = 1.0 - f
        g1 = lg.astype(BF16)
        r1 = lg - g1.astype(F32)
        g2 = r1.astype(BF16)
        g3 = (r1 - g2.astype(F32)).astype(BF16)
        ez = jnp.exp(jnp.dot(mm, jnp.concatenate([g1, g2, g3], axis=0), preferred_element_type=F32))
        qb, kb = q.astype(BF16), kk.astype(BF16)
        attn = [bm_ref[HG_LEVELS] * nt(qb[:, hs], kb[:, hs]) for hs in heads]
        for lv in range(HG_LEVELS):
            zl = ez[lv * C:(lv + 1) * C]
            rm = rm_ref[lv]
            ql = (q * zl * rm).astype(BF16)
            kl = (kk * zl * (1.0 - rm)).astype(BF16)
            attn = [a + bm_ref[lv] * nt(ql[:, hs], kl[:, hs]) for a, hs in zip(attn, heads)]
        zc = ez[HG_LEVELS * C:(HG_LEVELS + 1) * C]
        zs = ez[(HG_LEVELS + 1) * C:(HG_LEVELS + 2) * C]
        qz = (q * zc).astype(BF16)
        kz = (kk * zs).astype(BF16)
        sts = [st_sc[h] for h in range(HG_HEADS)]
        o_intra = [jnp.dot(a.astype(BF16), v[:, hs], preferred_element_type=F32) for a, hs in zip(attn, heads)]
        o_inter = [nt(qz[:, hs], st.astype(BF16)) for st, hs in zip(sts, heads)]
        upd = [lax.dot_general(v[:, hs], kz[:, hs], TN_DIMS, preferred_element_type=F32) for hs in heads]
        for h, hs in enumerate(heads):
            st_sc[h] = zc[C - 1:C, hs] * sts[h] + upd[h]
            o = o_intra[h] + o_inter[h]
            y = o * lax.rsqrt(jnp.mean(o * o, axis=-1, keepdims=True) + RMS_EPS) * gn
            g = gg[:, hs]
            o_ref[0, rs, hs] = (y * (g * jax.nn.sigmoid(g))).astype(o_ref.dtype)


def _hgrn(hq, hf, hi, hg, lb_logits, hg_norm, B, S):
    W = HG_HEADS * HG_DIM
    mm3, rm, bm = _hgrn_consts()
    r3 = lambda a: a.reshape(B, S, W)
    tok = pl.BlockSpec((1, HG_TB, W), lambda b, j: (b, j, 0))
    const = lambda a: pl.BlockSpec(a.shape, lambda b, j: (0,) * a.ndim)
    gn = hg_norm.reshape(1, HG_DIM)
    return pl.pallas_call(
        _hgrn_kernel,
        grid=(B, S // HG_TB),
        in_specs=[tok, tok, tok, tok, const(lb_logits), const(gn), const(mm3), const(rm), const(bm)],
        out_specs=tok,
        out_shape=jax.ShapeDtypeStruct((B, S, W), BF16),
        scratch_shapes=[pltpu.VMEM((HG_HEADS, HG_DIM, HG_DIM), F32)],
        compiler_params=_cparams(("parallel", "arbitrary")),
        name="hgrn2",
    )(r3(hq), r3(hf), r3(hi), r3(hg), lb_logits, gn, mm3, rm, bm)


MG_TM = 256


SUBLANES = 8


def _to_row_tiles(ref, x, lead=()):
    for s in range(SUBLANES):
        ref[lead + (slice(None), s, slice(None))] = x[:, s * LANES:(s + 1) * LANES]


def _from_row_tiles(ref, lead=()):
    return jnp.concatenate([ref[lead + (slice(None), s, slice(None))] for s in range(SUBLANES)], axis=1)


def _first_lane(cond, lane_f):
    return jnp.min(jnp.where(cond, lane_f, float(LANES)), axis=1, keepdims=True)


def _merge_kernel(on_ref, oh_ref, mg_ref, x_ref, wn_ref, wh_ref, wo_ref, fn_ref, wr1_ref, wr2_ref, br_ref,
                  x1_ref, h2_ref, rt_ref):
    D = x_ref.shape[1]
    a = jnp.dot(on_ref[...], wn_ref[...], preferred_element_type=F32)
    b = jnp.dot(oh_ref[...], wh_ref[...], preferred_element_type=F32)
    mixed = (jax.nn.sigmoid(mg_ref[:, 0:D].astype(F32)) * a +
             jax.nn.sigmoid(mg_ref[:, D:2 * D].astype(F32)) * b)
    x1 = x_ref[...] + jnp.dot(mixed.astype(BF16), wo_ref[...], preferred_element_type=F32)
    x1_ref[...] = x1
    h2 = x1 * lax.rsqrt(jnp.mean(x1 * x1, axis=-1, keepdims=True) + RMS_EPS) * fn_ref[...]
    _to_row_tiles(h2_ref, h2)
    h_hi = h2.astype(BF16)
    h_lo = (h2 - h_hi.astype(F32)).astype(BF16)
    logits = (jnp.dot(h_lo, wr1_ref[...], preferred_element_type=F32) +
              jnp.dot(h_hi, wr2_ref[...], preferred_element_type=F32) +
              jnp.dot(h_hi, wr1_ref[...], preferred_element_type=F32)) + br_ref[...]
    lane = lax.broadcasted_iota(jnp.int32, logits.shape, 1)
    ninf = -jnp.inf
    gl = jnp.where(lane < N_GROUPS, logits, ninf)
    gmax = jnp.max(gl, axis=1, keepdims=True)
    lane_f = lane.astype(F32)
    gsel = _first_lane(gl == gmax, lane_f)
    gprob = 1.0 / jnp.sum(jnp.exp(gl - gmax), axis=1, keepdims=True)
    lo = N_GROUPS + gsel * EXPERTS_PER_GROUP
    el = jnp.where((lane_f >= lo) & (lane_f < lo + EXPERTS_PER_GROUP), logits, ninf)
    v1 = jnp.max(el, axis=1, keepdims=True)
    i1 = _first_lane(el == v1, lane_f)
    el2 = jnp.where(lane_f == i1, ninf, el)
    v2 = jnp.max(el2, axis=1, keepdims=True)
    i2 = _first_lane(el2 == v2, lane_f)
    e2 = jnp.exp(v2 - v1)
    w1 = gprob / (1.0 + e2)
    w2 = gprob * e2 / (1.0 + e2)
    rt = jnp.where(lane == 0, i1 - N_GROUPS,
                   jnp.where(lane == 1, i2 - N_GROUPS,
                             jnp.where(lane == 2, w1, jnp.where(lane == 3, w2, 0.0))))
    rt_ref[...] = rt


def _merge(o_nsa, o_hg, mg, x2, w_br_nsa, w_br_hg, w_out, ffn_norm, w_grp, b_grp, w_rtr, b_rtr):
    T, D = x2.shape
    nr = N_GROUPS + N_EXPERTS
    wr = jnp.pad(jnp.concatenate([w_grp, w_rtr], axis=1), ((0, 0), (0, LANES - nr)))
    wr1 = wr.astype(BF16)
    wr2 = (wr - wr1.astype(F32)).astype(BF16)
    br = jnp.pad(jnp.concatenate([b_grp, b_rtr]), (0, LANES - nr)).reshape(1, LANES)
    tok = lambda w: pl.BlockSpec((MG_TM, w), lambda i: (i, 0))
    const = lambda a: pl.BlockSpec(a.shape, lambda i: (0,) * a.ndim)
    wn, wh, wo = w_br_nsa.astype(BF16), w_br_hg.astype(BF16), w_out.astype(BF16)
    fn = ffn_norm.reshape(1, D)
    return pl.pallas_call(
        _merge_kernel,
        grid=(T // MG_TM,),
        in_specs=[tok(o_nsa.shape[1]), tok(o_hg.shape[1]), tok(2 * D), tok(D),
                  const(wn), const(wh), const(wo), const(fn), const(wr1), const(wr2), const(br)],
        out_specs=[tok(D), pl.BlockSpec((MG_TM, SUBLANES, D // SUBLANES), lambda i: (i, 0, 0)), tok(LANES)],
        out_shape=[jax.ShapeDtypeStruct((T, D), F32), jax.ShapeDtypeStruct((T, SUBLANES, D // SUBLANES), F32),
                   jax.ShapeDtypeStruct((T, LANES), F32)],
        compiler_params=_cparams(("parallel",)),
        name="merge",
    )(o_nsa, o_hg, mg, x2, wn, wh, wo, fn, wr1, wr2, br)


MOE_TM = 256


def _moe_kernel(te_ref, nv_ref, src_ref, srcn_ref, dst_ref, h_hbm, wg_ref, wu_ref, wd_ref, y_hbm,
                xbuf, ybuf, wgb, wub, wdb, gsem, ssem):
    i = pl.program_id(0)
    n = pl.num_programs(0)
    tm = xbuf.shape[1]
    slot = i & 1
    prv = jnp.maximum(i - 1, 0)
    nv_i = nv_ref[i]
    nv_prev = jnp.where(i > 0, nv_ref[prv], 0)
    nv_next = jnp.where(i + 1 < n, nv_ref[jnp.minimum(i + 1, n - 1)], 0)

    def gather_copy(idx_ref, r, sl):
        return pltpu.make_async_copy(h_hbm.at[idx_ref[0, 0, r]], xbuf.at[sl, r], gsem.at[sl])

    def scatter_copy(row, r, sl):
        return pltpu.make_async_copy(ybuf.at[sl, r], y_hbm.at[row], ssem.at[sl])

    def rows(count, fn):
        groups = jnp.right_shift(count, 4)

        def body16(g, c):
            for k in range(16):
                fn(g * 16 + k)
            return c

        def body1(r, c):
            fn(r)
            return c

        lax.fori_loop(0, groups, body16, 0)
        lax.fori_loop(groups * 16, count, body1, 0)

    def wait_rows(count, copy_of):
        k = tm
        while k >= 1:
            pl.when((count & k) != 0)(lambda k=k: copy_of(k).wait())
            k //= 2

    def gather_block(k, sl):
        return pltpu.make_async_copy(h_hbm.at[pl.ds(0, k)], xbuf.at[sl, pl.ds(0, k)], gsem.at[sl])

    def scatter_block(k, sl):
        return pltpu.make_async_copy(ybuf.at[sl, pl.ds(0, k)], y_hbm.at[pl.ds(0, k)], ssem.at[sl])

    @pl.when(i == 0)
    def _():
        xbuf[...] = jnp.zeros(xbuf.shape, F32)
        rows(nv_i, lambda r: gather_copy(src_ref, r, 0).start())

    rows(nv_next, lambda r: gather_copy(srcn_ref, r, 1 - slot).start())

    @pl.when((i == 0) | (te_ref[i] != te_ref[prv]))
    def _():
        wgb[...] = wg_ref[0].astype(BF16)
        wub[...] = wu_ref[0].astype(BF16)
        wdb[...] = wd_ref[0].astype(BF16)

    @pl.when(nv_i > 0)
    def _():
        wait_rows(nv_i, lambda k: gather_block(k, slot))
        x = _from_row_tiles(xbuf, (slot,)).astype(BF16)
        hg = jnp.dot(x, wgb[...], preferred_element_type=F32)
        hu = jnp.dot(x, wub[...], preferred_element_type=F32)
        hid = (hg * jax.nn.sigmoid(hg) * hu).astype(BF16)
        _to_row_tiles(ybuf, jnp.dot(hid, wdb[...], preferred_element_type=F32), (slot,))
        rows(nv_i, lambda r: scatter_copy(dst_ref[0, 0, r], r, slot).start())

    wait_rows(nv_prev, lambda k: scatter_block(k, 1 - slot))

    @pl.when(i == n - 1)
    def _():
        wait_rows(nv_i, lambda k: scatter_block(k, slot))


def _moe(h2, route, w_gate, w_up, w_down):
    T = h2.shape[0]
    E, D, F = w_gate.shape
    tm = MOE_TM
    n_tiles = 2 * T // tm + E
    i32 = jnp.int32
    ex = jnp.concatenate([route[:, 0], route[:, 1]]).astype(i32)
    order = jnp.argsort(ex, stable=True).astype(i32)
    counts = jnp.sum(ex[:, None] == jnp.arange(E, dtype=i32)[None, :], axis=0).astype(i32)
    padded = ((counts + tm - 1) // tm) * tm
    pend = jnp.cumsum(padded)
    pstart = pend - padded
    cstart = jnp.cumsum(counts) - counts
    tile_start = jnp.arange(n_tiles, dtype=i32) * tm
    te_raw = jnp.sum(tile_start[:, None] >= pend[None, :], axis=1).astype(i32)
    active = te_raw < E
    te_c = jnp.minimum(te_raw, E - 1)
    off = tile_start - pstart[te_c]
    nv = jnp.where(active, jnp.clip(counts[te_c] - off, 0, tm), 0).astype(i32)
    n_active = jnp.sum(active.astype(i32))
    te = jnp.where(active, te_c, te_c[jnp.maximum(n_active - 1, 0)])
    rank = off[:, None] + jnp.arange(tm, dtype=i32)[None, :]
    a = order[jnp.clip(cstart[te_c][:, None] + rank, 0, 2 * T - 1)]
    valid = jnp.arange(tm, dtype=i32)[None, :] < nv[:, None]
    src = jnp.where(valid, a % T, 0).reshape(n_tiles, 1, tm)
    dst = jnp.where(valid, a, 0).reshape(n_tiles, 1, tm)

    idx_spec = lambda f: pl.BlockSpec((1, 1, tm), lambda i, te, nv: (f(i), 0, 0), memory_space=pltpu.SMEM)
    wspec = lambda r, c: pl.BlockSpec((1, r, c), lambda i, te, nv: (te[i], 0, 0))
    grid_spec = pltpu.PrefetchScalarGridSpec(
        num_scalar_prefetch=2,
        grid=(n_tiles,),
        in_specs=[idx_spec(lambda i: i), idx_spec(lambda i: jnp.minimum(i + 1, n_tiles - 1)),
                  idx_spec(lambda i: i), pl.BlockSpec(memory_space=pl.ANY),
                  wspec(D, F), wspec(D, F), wspec(F, D)],
        out_specs=pl.BlockSpec(memory_space=pl.ANY),
        scratch_shapes=[pltpu.VMEM((2, tm, SUBLANES, D // SUBLANES), F32),
                        pltpu.VMEM((2, tm, SUBLANES, D // SUBLANES), F32),
                        pltpu.VMEM((D, F), BF16), pltpu.VMEM((D, F), BF16), pltpu.VMEM((F, D), BF16),
                        pltpu.SemaphoreType.DMA((2,)), pltpu.SemaphoreType.DMA((2,))],
    )
    return pl.pallas_call(
        _moe_kernel,
        grid_spec=grid_spec,
        out_shape=jax.ShapeDtypeStruct((2 * T, SUBLANES, D // SUBLANES), F32),
        compiler_params=_cparams(("arbitrary",)),
        name="moe",
    )(te, nv, src, src, dst, h2, w_gate, w_up, w_down)


FIN_TM = 256


def _final_kernel(x1_ref, y1_ref, y2_ref, rt_ref, g_ref, o_ref):
    rt = rt_ref[...]
    x = x1_ref[...] + rt[:, 2:3] * _from_row_tiles(y1_ref) + rt[:, 3:4] * _from_row_tiles(y2_ref)
    o_ref[...] = x * lax.rsqrt(jnp.mean(x * x, axis=-1, keepdims=True) + RMS_EPS) * g_ref[...]


def _final(x1, y, route, final_norm):
    T, D = x1.shape
    nb = T // FIN_TM
    ytile = (FIN_TM, SUBLANES, D // SUBLANES)
    return pl.pallas_call(
        _final_kernel,
        grid=(nb,),
        in_specs=[pl.BlockSpec((FIN_TM, D), lambda i: (i, 0)),
                  pl.BlockSpec(ytile, lambda i: (i, 0, 0)),
                  pl.BlockSpec(ytile, lambda i: (i + nb, 0, 0)),
                  pl.BlockSpec((FIN_TM, LANES), lambda i: (i, 0)),
                  pl.BlockSpec((1, D), lambda i: (0, 0))],
        out_specs=pl.BlockSpec((FIN_TM, D), lambda i: (i, 0)),
        out_shape=jax.ShapeDtypeStruct((T, D), F32),
        compiler_params=_cparams(("parallel",)),
        name="final",
    )(x1, y, y, route, final_norm.reshape(1, D))


def kernel(x, attn_norm, w_in, w_cmp_k, w_cmp_v, cmp_pos, hg_lb_logits, hg_norm, w_br_nsa, w_br_hg, w_out,
           ffn_norm, w_grp, b_grp, w_rtr, b_rtr, w_gate, w_up, w_down, final_norm):
    B, S, D = x.shape
    assert attn_norm.shape[0] == 1, "single-layer block"
    x2 = x.reshape(B * S, D)
    (kcin, vcin, hq, hf, hi, hg, mg), (qt, ksa, vst, kw, vwt, gtt) = _in_proj(x2, attn_norm[0], w_in[0], B, S)
    kc, vct = _compress(kcin, vcin, w_cmp_k[0], w_cmp_v[0], cmp_pos[0], B, S)
    o_nsa = _nsa_attn(qt, kc, vct, ksa, vst, kw, vwt, gtt, B, S).reshape(B * S, -1)
    o_hg = _hgrn(hq, hf, hi, hg, hg_lb_logits, hg_norm[0], B, S).reshape(B * S, -1)
    x1, h2, route = _merge(o_nsa, o_hg, mg, x2, w_br_nsa[0], w_br_hg[0], w_out[0], ffn_norm[0],
                           w_grp[0], b_grp[0], w_rtr[0], b_rtr[0])
    y = _moe(h2, route, w_gate[0], w_up[0], w_down[0])
    out = _final(x1, y, route, final_norm)
    return out.reshape(B, S, D)
```

```python
import functools

import numpy as np
import jax
import jax.numpy as jnp
from jax import lax
from jax.experimental import pallas as pl
from jax.experimental.pallas import tpu as pltpu

F32 = jnp.float32
BF16 = jnp.bfloat16

NSA_HEADS = 8
NSA_KV_GROUPS = 2
NSA_HEAD_DIM = 64
NSA_R = NSA_HEADS // NSA_KV_GROUPS
CMP_LEN = 32
CMP_STRIDE = 16
SLC_LEN = 64
SLC_SHIFT = 6
SLC_TOPN = 16
WINDOW = 512
Q_BLOCK = 128
ROPE_THETA = 500000.0
ROPE_DIM = NSA_HEAD_DIM // 4
HG_HEADS = 4
HG_DIM = 128
HG_CHUNK = 64
N_GROUPS = 4
EXPERTS_PER_GROUP = 8
N_EXPERTS = N_GROUPS * EXPERTS_PER_GROUP
RMS_EPS = 1e-6
NEG = -1e30
FORCE_SCORE = 1e4

LANES = 128
VMEM_LIMIT = 56 * 1024 * 1024

NT_DIMS = (((1,), (1,)), ((), ()))
TN_DIMS = (((0,), (0,)), ((), ()))


def _cparams(sem):
    return pltpu.CompilerParams(dimension_semantics=sem, vmem_limit_bytes=VMEM_LIMIT)


IN_TM = 256
IN_SEGS = (("q", 512, 512), ("kc", 128, 128), ("vc", 128, 128), ("kv", 512, 512), ("gate", 24, 128),
           ("hq", 512, 512), ("hf", 512, 512), ("hi", 512, 512), ("hg", 512, 512), ("merge", 2048, 2048))
IN_F32 = ("kc", "vc")
IN_PREP = ("q", "kv", "gate")


def _in_proj_kernel(x_ref, g_ref, w_ref, c_ref, s1_ref, s2_ref, nege_ref, *out_refs):
    n_plain = len(IN_SEGS) - len(IN_PREP)
    plain_refs, prep_refs = iter(out_refs[:n_plain]), out_refs[n_plain:]
    x = x_ref[...]
    h = x * lax.rsqrt(jnp.mean(x * x, axis=-1, keepdims=True) + RMS_EPS) * g_ref[...]
    h = h.astype(BF16)
    off, kept = 0, {}
    for name, _, wpad in IN_SEGS:
        y = jnp.dot(h, w_ref[:, off:off + wpad], preferred_element_type=F32)
        off += wpad
        if name in IN_PREP:
            kept[name] = y
            if len(kept) == len(IN_PREP):
                _prep_body(kept["q"], kept["kv"], kept["gate"], c_ref[...], s1_ref[...], s2_ref[...],
                           nege_ref[...], *prep_refs)
        else:
            o_ref = next(plain_refs)
            o_ref[...] = y.astype(o_ref.dtype)


def _in_proj(x2, attn_norm, w_in, B, S):
    T, D = x2.shape
    G, dk, QB = NSA_KV_GROUPS, NSA_HEAD_DIM, Q_BLOCK
    pieces, off = [], 0
    for _, w, wpad in IN_SEGS:
        p = w_in[:, off:off + w]
        if wpad != w:
            p = jnp.pad(p, ((0, 0), (0, wpad - w)))
        pieces.append(p)
        off += w
    wcat = jnp.concatenate(pieces, axis=1).astype(BF16)
    NP = wcat.shape[1]
    plain = [(jax.ShapeDtypeStruct((T, wpad), F32 if name in IN_F32 else BF16),
              pl.BlockSpec((IN_TM, wpad), lambda i: (i, 0)))
             for name, _, wpad in IN_SEGS if name not in IN_PREP]
    nj = S // IN_TM
    c, s1, s2 = _rope_tables(jnp.arange(S))
    key = np.arange(S)
    nege = jnp.asarray(np.where(key[:, None] // SLC_LEN == np.arange(LANES)[None, :], NEG, 0.0), BF16)
    tab = pl.BlockSpec((IN_TM, LANES), lambda i: (i % nj, 0))
    rows_out = lambda w, dt: (jax.ShapeDtypeStruct((B, G, S, w), dt),
                              pl.BlockSpec((1, G, IN_TM, w), lambda i: (i // nj, 0, i % nj, 0)))
    cols_out = lambda n, dt: (jax.ShapeDtypeStruct((B, G, n, S), dt),
                              pl.BlockSpec((1, G, n, IN_TM), lambda i: (i // nj, 0, 0, i % nj)))
    qt_out = (jax.ShapeDtypeStruct((B, G, S // QB, LANES, NSA_R * QB), BF16),
              pl.BlockSpec((1, G, IN_TM // QB, LANES, NSA_R * QB), lambda i: (i // nj, 0, i % nj, 0, 0)))
    prep = [qt_out, rows_out(2 * LANES, BF16), cols_out(dk, BF16), rows_out(LANES, BF16),
            cols_out(dk, BF16), cols_out(GATE_ROWS, F32)]
    outs = pl.pallas_call(
        _in_proj_kernel,
        grid=(T // IN_TM,),
        in_specs=[pl.BlockSpec((IN_TM, D), lambda i: (i, 0)),
                  pl.BlockSpec((1, D), lambda i: (0, 0)),
                  pl.BlockSpec((D, NP), lambda i: (0, 0)), tab, tab, tab, tab],
        out_specs=[o[1] for o in plain + prep],
        out_shape=[o[0] for o in plain + prep],
        compiler_params=_cparams(("parallel",)),
        name="in_proj",
    )(x2, attn_norm.reshape(1, D), wcat, c, s1, s2, nege)
    return outs[:len(plain)], outs[len(plain):]


def _rope_tables(pos):
    half = ROPE_DIM // 2
    d = np.arange(LANES) % NSA_HEAD_DIM
    inv_freq = ROPE_THETA ** (-jnp.arange(half, dtype=F32) / half)
    lane_freq = jnp.where(d < ROPE_DIM, inv_freq[d % half], 0.0)
    ang = pos.astype(F32)[:, None] * lane_freq[None, :]
    cos, sin = jnp.cos(ang), jnp.sin(ang)
    s1 = jnp.where(d < half, -sin, 0.0)
    s2 = jnp.where((d >= half) & (d < ROPE_DIM), sin, 0.0)
    return cos, s1, s2


def _rope(x, c, s1, s2):
    half = ROPE_DIM // 2
    return x * c + pltpu.roll(x, LANES - half, 1) * s1 + pltpu.roll(x, half, 1) * s2


GATE_ROWS = 16
LOG2E = 1.4426950408889634


def _prep_body(q, kv, gate, c, s1, s2, nege, qt_ref, ksa_ref, vst_ref, kw_ref, vwt_ref, gtt_ref):
    lane = lax.broadcasted_iota(jnp.int32, c.shape, 1)
    lo = lane < NSA_HEAD_DIM
    dk, QB = NSA_HEAD_DIM, Q_BLOCK

    def split(x):
        return (jnp.where(lo, x, 0.0), jnp.where(lo, pltpu.roll(x, dk, 1), 0.0))

    scale = (dk ** -0.5) * LOG2E
    for blk in range(NSA_HEADS // 2):
        x = q[:, blk * LANES:(blk + 1) * LANES]
        for hh, part in enumerate(split(_rope(x, c, s1, s2) * scale)):
            h = 2 * blk + hh
            g, r = h // NSA_R, h % NSA_R
            pt = part.T.astype(BF16)
            for i in range(q.shape[0] // QB):
                qt_ref[0, g, i, :, r * QB:(r + 1) * QB] = pt[:, i * QB:(i + 1) * QB]
    for g, part in enumerate(split(_rope(kv[:, 0:LANES], c, s1, s2))):
        ksa_ref[0, g] = jnp.concatenate([part.astype(BF16), nege], axis=1)
    for g, part in enumerate(split(_rope(kv[:, 2 * LANES:3 * LANES], c, s1, s2))):
        kw_ref[0, g] = part.astype(BF16)
    for blk, ref in ((1, vst_ref), (3, vwt_ref)):
        for g, part in enumerate(split(kv[:, blk * LANES:(blk + 1) * LANES])):
            ref[0, g] = part.T[0:dk, :].astype(BF16)
    sg = jax.nn.sigmoid(gate)
    gtt_ref[0, 0] = sg.T[0:GATE_ROWS, :]
    gtt_ref[0, 1] = pltpu.roll(sg, LANES - 3 * NSA_R, 1).T[0:GATE_ROWS, :]


def _compress_kernel(tk_ref, tv_ref, wk_ref, wv_ref, pos_ref, c_ref, s1_ref, s2_ref, kc_ref, vct_ref):
    nc = c_ref.shape[0]
    dk = NSA_HEAD_DIM
    lane = lax.broadcasted_iota(jnp.int32, (nc, LANES), 1)
    lo = lane < dk

    def comp(t_ref, w_ref):
        a = jnp.zeros((nc, LANES), F32)
        b = jnp.zeros((nc, LANES), F32)
        c0 = jnp.zeros((SUBLANES, LANES), F32)
        for j in range(CMP_STRIDE):
            u = t_ref[pl.ds(j, nc, stride=CMP_STRIDE), :].astype(BF16)
            a = a + jnp.dot(u, w_ref[j], preferred_element_type=F32)
            b = b + jnp.dot(u, w_ref[CMP_STRIDE + j], preferred_element_type=F32)
        for l in range(CMP_LEN):
            c0 = c0 + jnp.dot(pos_ref[l], w_ref[l], preferred_element_type=F32)
        return a + pltpu.roll(b, nc - 1, 0) + c0[0:1, :]

    def split(x):
        return (jnp.where(lo, x, 0.0), jnp.where(lo, pltpu.roll(x, dk, 1), 0.0))

    for g, part in enumerate(split(_rope(comp(tk_ref, wk_ref), c_ref[...], s1_ref[...], s2_ref[...]))):
        kc_ref[0, g] = part.astype(BF16)
    for g, part in enumerate(split(comp(tv_ref, wv_ref))):
        vct_ref[0, g] = part.T[0:dk, :].astype(BF16)


def _compress(kcin, vcin, w_cmp_k, w_cmp_v, cmp_pos, B, S):
    G, dk = NSA_KV_GROUPS, NSA_HEAD_DIM
    nc = S // CMP_STRIDE

    def block_diag(w):
        w3 = w.reshape(CMP_LEN, dk, dk)
        z = jnp.zeros_like(w3)
        return jnp.concatenate([jnp.concatenate([w3, z], axis=2),
                                jnp.concatenate([z, w3], axis=2)], axis=1).astype(BF16)

    pos = jnp.concatenate([cmp_pos, cmp_pos], axis=1)
    pos = jnp.broadcast_to(pos[:, None, :], (CMP_LEN, SUBLANES, G * dk)).astype(BF16)
    c, s1, s2 = _rope_tables(jnp.arange(nc) * CMP_STRIDE)
    tspec = pl.BlockSpec((S, LANES), lambda b: (b, 0))
    wspec = pl.BlockSpec((CMP_LEN, LANES, LANES), lambda b: (0, 0, 0))
    tab = pl.BlockSpec((nc, LANES), lambda b: (0, 0))
    return pl.pallas_call(
        _compress_kernel,
        grid=(B,),
        in_specs=[tspec, tspec, wspec, wspec,
                  pl.BlockSpec((CMP_LEN, SUBLANES, LANES), lambda b: (0, 0, 0)), tab, tab, tab],
        out_specs=[pl.BlockSpec((1, G, nc, LANES), lambda b: (b, 0, 0, 0)),
                   pl.BlockSpec((1, G, dk, nc), lambda b: (b, 0, 0, 0))],
        out_shape=[jax.ShapeDtypeStruct((B, G, nc, LANES), BF16),
                   jax.ShapeDtypeStruct((B, G, dk, nc), BF16)],
        compiler_params=_cparams(("parallel",)),
        name="compress",
    )(kcin, vcin, block_diag(w_cmp_k), block_diag(w_cmp_v), pos, c, s1, s2)


CMP_STRIDE_SHIFT = 4
CMP_ROWS = 128
CMP_ROWS_SHIFT = 7
SEL_KC = 256
SEL_KC_SHIFT = 8
SEL_UC = 2
SEL_UC_SHIFT = 1
WIN_KEYS = WINDOW + Q_BLOCK


def _col_reduce(x, op, fin):
    n = x.shape[0]
    while (n // 2) % 8 == 0 and n > 8:
        n //= 2
        x = op(x[:n], x[n:])
    return fin(x, axis=0, keepdims=True)


def _col_max(x):
    return _col_reduce(x, jnp.maximum, jnp.max)


def _col_min(x):
    return _col_reduce(x, jnp.minimum, jnp.min)


def _col_sum(x):
    return _col_reduce(x, jnp.add, jnp.sum)


def _softmax_cols(s):
    e = jnp.exp2(s - _col_max(s))
    return e, _col_sum(e)


def _nsa_kernel(qt_ref, kc_ref, vct_ref, ksa_ref, vst_ref, kw_ref, vwt_ref, gtt_ref, ovt_ref,
                o_ref, rhs_sc, m0_sc, m1_sc, l0_sc, l1_sc, a0_sc, a1_sc, sa_sc, sb_sc, ocmp_sc, imp_sc,
                *, n_sel):
    m_scs, l_scs, acc_scs = (m0_sc, m1_sc), (l0_sc, l1_sc), (a0_sc, a1_sc)
    s_bufs = (sa_sc, sb_sc)
    R, QB = NSA_R, Q_BLOCK
    cols = R * QB
    i = pl.program_id(2)
    t0 = i * QB
    qt = qt_ref[0, 0, 0]

    def per_head(x):
        return jnp.concatenate([x] * (cols // x.shape[1]), axis=1)

    def rc(n):
        return (lax.broadcasted_iota(jnp.int32, (n, QB), 0),
                t0 + lax.broadcasted_iota(jnp.int32, (n, QB), 1))

    nc = kc_ref.shape[2]

    def cmp_branch(nr):
        sc = jnp.dot(kc_ref[0, 0, 0:nr, :], qt, preferred_element_type=F32)
        n_idx, tq_c = rc(nr)
        cvis = (n_idx * CMP_STRIDE + (CMP_LEN - 1)) <= tq_c
        ec, lc = _softmax_cols(sc + per_head(jnp.where(cvis, 0.0, NEG)))
        pc = ec * per_head(jnp.where(cvis, 1.0, 0.0)) * (1.0 / lc)
        ocmp_sc[...] = jnp.dot(vct_ref[0, 0, :, 0:nr], pc.astype(BF16), preferred_element_type=F32)
        psum = pc[:, 0:QB] + pc[:, QB:2 * QB] + pc[:, 2 * QB:3 * QB] + pc[:, 3 * QB:4 * QB]
        p_hi = psum.astype(BF16)
        p_lo = (psum - p_hi.astype(F32)).astype(BF16)
        ovt = ovt_ref[:, 0:nr]
        imp_sc[...] = (jnp.dot(ovt, p_hi, preferred_element_type=F32) +
                       jnp.dot(ovt, p_lo, preferred_element_type=F32))

    n_var = nc // CMP_ROWS
    last_vis = jnp.right_shift(t0 + QB - CMP_LEN, CMP_STRIDE_SHIFT)
    var = jnp.minimum(jnp.right_shift(jnp.maximum(last_vis, 0), CMP_ROWS_SHIFT), n_var - 1)
    for k in range(n_var):
        pl.when(var == k)(functools.partial(cmp_branch, (k + 1) * CMP_ROWS))
    o_cmp = ocmp_sc[...]
    imp = imp_sc[...]

    start = pl.multiple_of(jnp.maximum(t0 - WINDOW, 0), QB)
    sw = jnp.dot(kw_ref[0, 0, pl.ds(start, WIN_KEYS), :], qt, preferred_element_type=F32)

    j = lax.broadcasted_iota(jnp.int32, (LANES, QB), 0)
    tq = t0 + lax.broadcasted_iota(jnp.int32, (LANES, QB), 1)
    force = (j == jnp.right_shift(tq, SLC_SHIFT)) | (j == 0)
    valid = (j * SLC_LEN) <= tq
    score = jnp.where(force, -jnp.inf, jnp.where(valid, imp, -1.0))
    work = jnp.where(j < n_sel, score, -jnp.inf)
    jf = j.astype(F32)
    for _ in range(min(SLC_TOPN, n_sel) - 2):
        mx = _col_max(work)
        idx = _col_min(jnp.where(work == mx, jf, float(LANES)))
        work = jnp.where(jf == idx, -jnp.inf, work)
    notsel = jnp.where(work == -jnp.inf, 0.0, 1.0)

    k_idx, tq_w = rc(WIN_KEYS)
    rel = tq_w - (start + k_idx)
    pw, lw = _softmax_cols(sw + per_head(jnp.where((rel >= 0) & (rel < WINDOW), 0.0, NEG)))
    o_win = jnp.dot(vwt_ref[0, 0, :, pl.ds(start, WIN_KEYS)], pw.astype(BF16),
                    preferred_element_type=F32) * (1.0 / lw)

    rhs_sc[0:LANES, :] = qt
    rhs_sc[LANES:2 * LANES, :] = jnp.concatenate([notsel.astype(BF16)] * R, axis=1)
    halves = tuple(zip(m_scs, l_scs, acc_scs))
    hw = cols // len(halves)
    for m_sc, l_sc, acc_sc in halves:
        m_sc[...] = jnp.full(m_sc.shape, NEG, F32)
        l_sc[...] = jnp.zeros(l_sc.shape, F32)
        acc_sc[...] = jnp.zeros(acc_sc.shape, F32)

    def sel_scores(chunk_ids):
        kas = [ksa_ref[0, 0, pl.ds(pl.multiple_of(c * SEL_KC, SEL_KC), SEL_KC), :] for c in chunk_ids]
        return [[jnp.dot(ka, rhs_sc[:, hp * hw:(hp + 1) * hw], preferred_element_type=F32)
                 for ka in kas] for hp in range(len(halves))]

    def sel_update(chunk_ids, scores, diagonal):
        k0s = [pl.multiple_of(c * SEL_KC, SEL_KC) for c in chunk_ids]
        vts = [vst_ref[0, 0, :, pl.ds(k0, SEL_KC)] for k0 in k0s]
        for hp, (m_sc, l_sc, acc_sc) in enumerate(halves):
            ss = scores[hp]
            if diagonal:
                k_idx, tq_s = rc(SEL_KC)
                ss = [s + jnp.concatenate([jnp.where(k0 + k_idx <= tq_s, 0.0, NEG)] * (hw // QB), axis=1)
                      for k0, s in zip(k0s, ss)]
            m_old = m_sc[...]
            m_new = functools.reduce(jnp.maximum, [_col_max(s) for s in ss], m_old)
            alpha = jnp.exp2(m_old - m_new)
            ps = [jnp.exp2(s - m_new) for s in ss]
            l_sc[...] = alpha * l_sc[...] + functools.reduce(jnp.add, [_col_sum(p) for p in ps])
            pv = [jnp.dot(vt, p.astype(BF16), preferred_element_type=F32) for vt, p in zip(vts, ps)]
            acc_sc[...] = alpha * acc_sc[...] + functools.reduce(jnp.add, pv)
            m_sc[...] = m_new

    n_units = jnp.right_shift(t0, SEL_KC_SHIFT + SEL_UC_SHIFT)
    nh = len(halves)

    def unit(u):
        return [SEL_UC * u + c for c in range(SEL_UC)]

    def put_scores(buf, scores):
        for hp in range(nh):
            for c in range(SEL_UC):
                buf[hp * SEL_UC + c] = scores[hp][c]

    def get_scores(buf):
        return [[buf[hp * SEL_UC + c] for c in range(SEL_UC)] for hp in range(nh)]

    put_scores(s_bufs[0], sel_scores(unit(0)))

    def sel_step(u, carry):
        for parity in range(2):
            @pl.when((u & 1) == parity)
            def _():
                put_scores(s_bufs[1 - parity], sel_scores(unit(u + 1)))
                sel_update(unit(u), get_scores(s_bufs[parity]), False)
        return carry

    lax.fori_loop(0, n_units, sel_step, 0)
    in_unit = jnp.right_shift(t0 & (SEL_UC * SEL_KC - 1), SEL_KC_SHIFT)
    for parity in range(2):
        for last in range(SEL_UC):
            @pl.when(((n_units & 1) == parity) & (in_unit == last))
            def _():
                scores = [s[:last + 1] for s in get_scores(s_bufs[parity])]
                sel_update(unit(n_units)[:last + 1], scores, True)
    o_slc = jnp.concatenate([a[...] * (1.0 / l[...]) for _, l, a in halves], axis=1)

    gtt = gtt_ref[0, 0]
    outs = []
    for r in range(R):
        cs = slice(r * QB, (r + 1) * QB)
        outs.append(gtt[3 * r:3 * r + 1, :] * o_cmp[:, cs] + gtt[3 * r + 1:3 * r + 2, :] * o_slc[:, cs] +
                    gtt[3 * r + 2:3 * r + 3, :] * o_win[:, cs])
    o_ref[0] = jnp.concatenate(outs, axis=0).T.astype(o_ref.dtype)


def _nsa_attn(qt, kc, vct, ksa, vst, kw, vwt, gtt, B, S):
    G, R, QB, dk = NSA_KV_GROUPS, NSA_R, Q_BLOCK, NSA_HEAD_DIM
    nc = S // CMP_STRIDE
    n_sel = S // SLC_LEN
    assert n_sel <= LANES and n_sel >= SLC_TOPN and S % (SEL_UC * SEL_KC) == 0 and S >= WIN_KEYS
    assert SEL_KC == 1 << SEL_KC_SHIFT and SEL_UC == 1 << SEL_UC_SHIFT and SEL_KC % QB == 0
    assert nc % CMP_ROWS == 0 and CMP_ROWS == 1 << CMP_ROWS_SHIFT and CMP_STRIDE == 1 << CMP_STRIDE_SHIFT
    cs = np.arange(nc)[None, :] * CMP_STRIDE
    ss = np.arange(LANES)[:, None] * SLC_LEN
    ovm = (cs < ss + SLC_LEN) & (cs + CMP_LEN - 1 >= ss) & (np.arange(LANES)[:, None] < n_sel) \
        & (np.arange(nc)[None, :] < nc - 1)
    ovt = jnp.asarray(ovm, BF16)
    rows_in = lambda n, w: pl.BlockSpec((1, 1, n, w), lambda b, g, i: (b, g, 0, 0))
    cols = R * QB
    return pl.pallas_call(
        functools.partial(_nsa_kernel, n_sel=n_sel),
        grid=(B, G, S // QB),
        in_specs=[pl.BlockSpec((1, 1, 1, LANES, cols), lambda b, g, i: (b, g, i, 0, 0)),
                  rows_in(nc, LANES), rows_in(dk, nc), rows_in(S, 2 * LANES), rows_in(dk, S),
                  rows_in(S, LANES), rows_in(dk, S),
                  pl.BlockSpec((1, 1, GATE_ROWS, QB), lambda b, g, i: (b, g, 0, i)),
                  pl.BlockSpec((LANES, nc), lambda b, g, i: (0, 0))],
        out_specs=pl.BlockSpec((1, QB, R * dk), lambda b, g, i: (b, i, g)),
        out_shape=jax.ShapeDtypeStruct((B, S, NSA_HEADS * dk), BF16),
        scratch_shapes=[pltpu.VMEM((2 * LANES, cols), BF16)] +
                       [pltpu.VMEM((1, cols // 2), F32)] * 4 + [pltpu.VMEM((dk, cols // 2), F32)] * 2 +
                       [pltpu.VMEM((2 * SEL_UC, SEL_KC, cols // 2), F32)] * 2 +
                       [pltpu.VMEM((dk, cols), F32), pltpu.VMEM((LANES, QB), F32)],
        compiler_params=_cparams(("parallel", "parallel", "arbitrary")),
        name="nsa_attn",
    )(qt, kc, vct, ksa, vst, kw, vwt, gtt, ovt)


HG_TB = 256
HG_LEVELS = 6


def _hgrn_consts():
    C = HG_CHUNK
    M = np.zeros((HG_LEVELS + 2, C, C), np.float32)
    RM = np.zeros((HG_LEVELS, C, 1), np.float32)
    BM = np.zeros((HG_LEVELS + 1, C, C), np.float32)
    for lv in range(HG_LEVELS):
        m = 1 << lv
        for t in range(C):
            bs = (t // (2 * m)) * (2 * m)
            mid = bs + m - 1
            if t - bs >= m:
                RM[lv, t, 0] = 1.0
                M[lv, t, mid + 1:t + 1] = 1.0
            else:
                M[lv, t, t + 1:mid + 1] = 1.0
            BM[lv, t, bs:bs + 2 * m] = 1.0
    BM[HG_LEVELS] = np.eye(C, dtype=np.float32)
    M[HG_LEVELS] = np.tril(np.ones((C, C), np.float32))
    M[HG_LEVELS + 1] = np.triu(np.ones((C, C), np.float32), 1)
    mm = M.reshape(-1, C)
    mm3 = np.concatenate([mm, mm, mm], axis=1)
    rm = np.broadcast_to(RM, (HG_LEVELS, C, HG_HEADS * HG_DIM)).copy()
    return jnp.asarray(mm3, BF16), jnp.asarray(rm, F32), jnp.asarray(BM, F32)


def _hgrn_kernel(q_ref, f_ref, i_ref, g_ref, lbl_ref, gn_ref, mm_ref, rm_ref, bm_ref, o_ref, st_sc):
    C, D = HG_CHUNK, HG_DIM

    @pl.when(pl.program_id(1) == 0)
    def _():
        st_sc[...] = jnp.zeros(st_sc.shape, F32)

    lbl = lbl_ref[...]
    mxl = jnp.max(lbl, axis=0, keepdims=True)
    el = jnp.exp(lbl - mxl)
    lb_all = el[0:1, :] / jnp.sum(el, axis=0, keepdims=True)
    mm = mm_ref[...]
    gn = gn_ref[...]
    heads = [slice(h * D, (h + 1) * D) for h in range(HG_HEADS)]
    nt = lambda a, b: lax.dot_general(a, b, NT_DIMS, preferred_element_type=F32)
    for ch in range(HG_TB // C):
        rs = slice(ch * C, (ch + 1) * C)
        q = q_ref[0, rs, :].astype(F32)
        v = i_ref[0, rs, :]
        gg = g_ref[0, rs, :].astype(F32)
        f = lb_all + (1.0 - lb_all) * jax.nn.sigmoid(f_ref[0, rs, :].astype(F32))
        lg = jnp.log(f)
        kk = 1.0 - f
        g1 = lg.astype(BF16)
        r1 = lg - g1.astype(F32)
        g2 = r1.astype(BF16)
        g3 = (r1 - g2.astype(F32)).astype(BF16)
        ez = jnp.exp(jnp.dot(mm, jnp.concatenate([g1, g2, g3], axis=0), preferred_element_type=F32))
        qb, kb = q.astype(BF16), kk.astype(BF16)
        attn = [bm_ref[HG_LEVELS] * nt(qb[:, hs], kb[:, hs]) for hs in heads]
        for lv in range(HG_LEVELS):
            zl = ez[lv * C:(lv + 1) * C]
            rm = rm_ref[lv]
            ql = (q * zl * rm).astype(BF16)
            kl = (kk * zl * (1.0 - rm)).astype(BF16)
            attn = [a + bm_ref[lv] * nt(ql[:, hs], kl[:, hs]) for a, hs in zip(attn, heads)]
        zc = ez[HG_LEVELS * C:(HG_LEVELS + 1) * C]
        zs = ez[(HG_LEVELS + 1) * C:(HG_LEVELS + 2) * C]
        qz = (q * zc).astype(BF16)
        kz = (kk * zs).astype(BF16)
        sts = [st_sc[h] for h in range(HG_HEADS)]
        o_intra = [jnp.dot(a.astype(BF16), v[:, hs], preferred_element_type=F32) for a, hs in zip(attn, heads)]
        o_inter = [nt(qz[:, hs], st.astype(BF16)) for st, hs in zip(sts, heads)]
        upd = [lax.dot_general(v[:, hs], kz[:, hs], TN_DIMS, preferred_element_type=F32) for hs in heads]
        for h, hs in enumerate(heads):
            st_sc[h] = zc[C - 1:C, hs] * sts[h] + upd[h]
            o = o_intra[h] + o_inter[h]
            y = o * lax.rsqrt(jnp.mean(o * o, axis=-1, keepdims=True) + RMS_EPS) * gn
            g = gg[:, hs]
            o_ref[0, rs, hs] = (y * (g * jax.nn.sigmoid(g))).astype(o_ref.dtype)


def _hgrn(hq, hf, hi, hg, lb_logits, hg_norm, B, S):
    W = HG_HEADS * HG_DIM
    mm3, rm, bm = _hgrn_consts()
    r3 = lambda a: a.reshape(B, S, W)
    tok = pl.BlockSpec((1, HG_TB, W), lambda b, j: (b, j, 0))
    const = lambda a: pl.BlockSpec(a.shape, lambda b, j: (0,) * a.ndim)
    gn = hg_norm.reshape(1, HG_DIM)
    return pl.pallas_call(
        _hgrn_kernel,
        grid=(B, S // HG_TB),
        in_specs=[tok, tok, tok, tok, const(lb_logits), const(gn), const(mm3), const(rm), const(bm)],
        out_specs=tok,
        out_shape=jax.ShapeDtypeStruct((B, S, W), BF16),
        scratch_shapes=[pltpu.VMEM((HG_HEADS, HG_DIM, HG_DIM), F32)],
        compiler_params=_cparams(("parallel", "arbitrary")),
        name="hgrn2",
    )(r3(hq), r3(hf), r3(hi), r3(hg), lb_logits, gn, mm3, rm, bm)


MG_TM = 256


SUBLANES = 8


def _to_row_tiles(ref, x, lead=()):
    for s in range(SUBLANES):
        ref[lead + (slice(None), s, slice(None))] = x[:, s * LANES:(s + 1) * LANES]


def _from_row_tiles(ref, lead=()):
    return jnp.concatenate([ref[lead + (slice(None), s, slice(None))] for s in range(SUBLANES)], axis=1)


def _first_lane(cond, lane_f):
    return jnp.min(jnp.where(cond, lane_f, float(LANES)), axis=1, keepdims=True)


def _merge_kernel(on_ref, oh_ref, mg_ref, x_ref, wn_ref, wh_ref, wo_ref, fn_ref, wr1_ref, wr2_ref, br_ref,
                  x1_ref, h2_ref, rt_ref):
    D = x_ref.shape[1]
    a = jnp.dot(on_ref[...], wn_ref[...], preferred_element_type=F32)
    b = jnp.dot(oh_ref[...], wh_ref[...], preferred_element_type=F32)
    mixed = (jax.nn.sigmoid(mg_ref[:, 0:D].astype(F32)) * a +
             jax.nn.sigmoid(mg_ref[:, D:2 * D].astype(F32)) * b)
    x1 = x_ref[...] + jnp.dot(mixed.astype(BF16), wo_ref[...], preferred_element_type=F32)
    x1_ref[...] = x1
    h2 = x1 * lax.rsqrt(jnp.mean(x1 * x1, axis=-1, keepdims=True) + RMS_EPS) * fn_ref[...]
    _to_row_tiles(h2_ref, h2)
    h_hi = h2.astype(BF16)
    h_lo = (h2 - h_hi.astype(F32)).astype(BF16)
    logits = (jnp.dot(h_lo, wr1_ref[...], preferred_element_type=F32) +
              jnp.dot(h_hi, wr2_ref[...], preferred_element_type=F32) +
              jnp.dot(h_hi, wr1_ref[...], preferred_element_type=F32)) + br_ref[...]
    lane = lax.broadcasted_iota(jnp.int32, logits.shape, 1)
    ninf = -jnp.inf
    gl = jnp.where(lane < N_GROUPS, logits, ninf)
    gmax = jnp.max(gl, axis=1, keepdims=True)
    lane_f = lane.astype(F32)
    gsel = _first_lane(gl == gmax, lane_f)
    gprob = 1.0 / jnp.sum(jnp.exp(gl - gmax), axis=1, keepdims=True)
    lo = N_GROUPS + gsel * EXPERTS_PER_GROUP
    el = jnp.where((lane_f >= lo) & (lane_f < lo + EXPERTS_PER_GROUP), logits, ninf)
    v1 = jnp.max(el, axis=1, keepdims=True)
    i1 = _first_lane(el == v1, lane_f)
    el2 = jnp.where(lane_f == i1, ninf, el)
    v2 = jnp.max(el2, axis=1, keepdims=True)
    i2 = _first_lane(el2 == v2, lane_f)
    e2 = jnp.exp(v2 - v1)
    w1 = gprob / (1.0 + e2)
    w2 = gprob * e2 / (1.0 + e2)
    rt = jnp.where(lane == 0, i1 - N_GROUPS,
                   jnp.where(lane == 1, i2 - N_GROUPS,
                             jnp.where(lane == 2, w1, jnp.where(lane == 3, w2, 0.0))))
    rt_ref[...] = rt


def _merge(o_nsa, o_hg, mg, x2, w_br_nsa, w_br_hg, w_out, ffn_norm, w_grp, b_grp, w_rtr, b_rtr):
    T, D = x2.shape
    nr = N_GROUPS + N_EXPERTS
    wr = jnp.pad(jnp.concatenate([w_grp, w_rtr], axis=1), ((0, 0), (0, LANES - nr)))
    wr1 = wr.astype(BF16)
    wr2 = (wr - wr1.astype(F32)).astype(BF16)
    br = jnp.pad(jnp.concatenate([b_grp, b_rtr]), (0, LANES - nr)).reshape(1, LANES)
    tok = lambda w: pl.BlockSpec((MG_TM, w), lambda i: (i, 0))
    const = lambda a: pl.BlockSpec(a.shape, lambda i: (0,) * a.ndim)
    wn, wh, wo = w_br_nsa.astype(BF16), w_br_hg.astype(BF16), w_out.astype(BF16)
    fn = ffn_norm.reshape(1, D)
    return pl.pallas_call(
        _merge_kernel,
        grid=(T // MG_TM,),
        in_specs=[tok(o_nsa.shape[1]), tok(o_hg.shape[1]), tok(2 * D), tok(D),
                  const(wn), const(wh), const(wo), const(fn), const(wr1), const(wr2), const(br)],
        out_specs=[tok(D), pl.BlockSpec((MG_TM, SUBLANES, D // SUBLANES), lambda i: (i, 0, 0)), tok(LANES)],
        out_shape=[jax.ShapeDtypeStruct((T, D), F32), jax.ShapeDtypeStruct((T, SUBLANES, D // SUBLANES), F32),
                   jax.ShapeDtypeStruct((T, LANES), F32)],
        compiler_params=_cparams(("parallel",)),
        name="merge",
    )(o_nsa, o_hg, mg, x2, wn, wh, wo, fn, wr1, wr2, br)


MOE_TM = 512


def _moe_kernel(te_ref, nv_ref, src_ref, srcn_ref, dst_ref, h_hbm, wg_ref, wu_ref, wd_ref, y_hbm,
                xbuf, ybuf, wgb, wub, wdb, gsem, ssem):
    i = pl.program_id(0)
    n = pl.num_programs(0)
    tm = xbuf.shape[1]
    slot = i & 1
    prv = jnp.maximum(i - 1, 0)
    nv_i = nv_ref[i]
    nv_prev = jnp.where(i > 0, nv_ref[prv], 0)
    nv_next = jnp.where(i + 1 < n, nv_ref[jnp.minimum(i + 1, n - 1)], 0)

    def gather_copy(idx_ref, r, sl):
        return pltpu.make_async_copy(h_hbm.at[idx_ref[0, 0, r]], xbuf.at[sl, r], gsem.at[sl])

    def scatter_copy(row, r, sl):
        return pltpu.make_async_copy(ybuf.at[sl, r], y_hbm.at[row], ssem.at[sl])

    def rows(count, fn):
        groups = jnp.right_shift(count, 4)

        def body16(g, c):
            for k in range(16):
                fn(g * 16 + k)
            return c

        def body1(r, c):
            fn(r)
            return c

        lax.fori_loop(0, groups, body16, 0)
        lax.fori_loop(groups * 16, count, body1, 0)

    def wait_rows(count, copy_of):
        k = tm
        while k >= 1:
            pl.when((count & k) != 0)(lambda k=k: copy_of(k).wait())
            k //= 2

    def gather_block(k, sl):
        return pltpu.make_async_copy(h_hbm.at[pl.ds(0, k)], xbuf.at[sl, pl.ds(0, k)], gsem.at[sl])

    def scatter_block(k, sl):
        return pltpu.make_async_copy(ybuf.at[sl, pl.ds(0, k)], y_hbm.at[pl.ds(0, k)], ssem.at[sl])

    @pl.when(i == 0)
    def _():
        xbuf[...] = jnp.zeros(xbuf.shape, F32)
        rows(nv_i, lambda r: gather_copy(src_ref, r, 0).start())

    rows(nv_next, lambda r: gather_copy(srcn_ref, r, 1 - slot).start())

    @pl.when((i == 0) | (te_ref[i] != te_ref[prv]))
    def _():
        wgb[...] = wg_ref[0].astype(BF16)
        wub[...] = wu_ref[0].astype(BF16)
        wdb[...] = wd_ref[0].astype(BF16)

    @pl.when(nv_i > 0)
    def _():
        wait_rows(nv_i, lambda k: gather_block(k, slot))
        x = _from_row_tiles(xbuf, (slot,)).astype(BF16)
        hg = jnp.dot(x, wgb[...], preferred_element_type=F32)
        hu = jnp.dot(x, wub[...], preferred_element_type=F32)
        hid = (hg * jax.nn.sigmoid(hg) * hu).astype(BF16)
        _to_row_tiles(ybuf, jnp.dot(hid, wdb[...], preferred_element_type=F32), (slot,))
        rows(nv_i, lambda r: scatter_copy(dst_ref[0, 0, r], r, slot).start())

    wait_rows(nv_prev, lambda k: scatter_block(k, 1 - slot))

    @pl.when(i == n - 1)
    def _():
        wait_rows(nv_i, lambda k: scatter_block(k, slot))


def _moe(h2, route, w_gate, w_up, w_down):
    T = h2.shape[0]
    E, D, F = w_gate.shape
    tm = MOE_TM
    n_tiles = 2 * T // tm + E
    i32 = jnp.int32
    ex = jnp.concatenate([route[:, 0], route[:, 1]]).astype(i32)
    order = jnp.argsort(ex, stable=True).astype(i32)
    counts = jnp.sum(ex[:, None] == jnp.arange(E, dtype=i32)[None, :], axis=0).astype(i32)
    padded = ((counts + tm - 1) // tm) * tm
    pend = jnp.cumsum(padded)
    pstart = pend - padded
    cstart = jnp.cumsum(counts) - counts
    tile_start = jnp.arange(n_tiles, dtype=i32) * tm
    te_raw = jnp.sum(tile_start[:, None] >= pend[None, :], axis=1).astype(i32)
    active = te_raw < E
    te_c = jnp.minimum(te_raw, E - 1)
    off = tile_start - pstart[te_c]
    nv = jnp.where(active, jnp.clip(counts[te_c] - off, 0, tm), 0).astype(i32)
    n_active = jnp.sum(active.astype(i32))
    te = jnp.where(active, te_c, te_c[jnp.maximum(n_active - 1, 0)])
    rank = off[:, None] + jnp.arange(tm, dtype=i32)[None, :]
    a = order[jnp.clip(cstart[te_c][:, None] + rank, 0, 2 * T - 1)]
    valid = jnp.arange(tm, dtype=i32)[None, :] < nv[:, None]
    src = jnp.where(valid, a % T, 0).reshape(n_tiles, 1, tm)
    dst = jnp.where(valid, a, 0).reshape(n_tiles, 1, tm)

    idx_spec = lambda f: pl.BlockSpec((1, 1, tm), lambda i, te, nv: (f(i), 0, 0), memory_space=pltpu.SMEM)
    wspec = lambda r, c: pl.BlockSpec((1, r, c), lambda i, te, nv: (te[i], 0, 0))
    grid_spec = pltpu.PrefetchScalarGridSpec(
        num_scalar_prefetch=2,
        grid=(n_tiles,),
        in_specs=[idx_spec(lambda i: i), idx_spec(lambda i: jnp.minimum(i + 1, n_tiles - 1)),
                  idx_spec(lambda i: i), pl.BlockSpec(memory_space=pl.ANY),
                  wspec(D, F), wspec(D, F), wspec(F, D)],
        out_specs=pl.BlockSpec(memory_space=pl.ANY),
        scratch_shapes=[pltpu.VMEM((2, tm, SUBLANES, D // SUBLANES), F32),
                        pltpu.VMEM((2, tm, SUBLANES, D // SUBLANES), F32),
                        pltpu.VMEM((D, F), BF16), pltpu.VMEM((D, F), BF16), pltpu.VMEM((F, D), BF16),
                        pltpu.SemaphoreType.DMA((2,)), pltpu.SemaphoreType.DMA((2,))],
    )
    return pl.pallas_call(
        _moe_kernel,
        grid_spec=grid_spec,
        out_shape=jax.ShapeDtypeStruct((2 * T, SUBLANES, D // SUBLANES), F32),
        compiler_params=_cparams(("arbitrary",)),
        name="moe",
    )(te, nv, src, src, dst, h2, w_gate, w_up, w_down)


FIN_TM = 256


def _final_kernel(x1_ref, y1_ref, y2_ref, rt_ref, g_ref, o_ref):
    rt = rt_ref[...]
    x = x1_ref[...] + rt[:, 2:3] * _from_row_tiles(y1_ref) + rt[:, 3:4] * _from_row_tiles(y2_ref)
    o_ref[...] = x * lax.rsqrt(jnp.mean(x * x, axis=-1, keepdims=True) + RMS_EPS) * g_ref[...]


def _final(x1, y, route, final_norm):
    T, D = x1.shape
    nb = T // FIN_TM
    ytile = (FIN_TM, SUBLANES, D // SUBLANES)
    return pl.pallas_call(
        _final_kernel,
        grid=(nb,),
        in_specs=[pl.BlockSpec((FIN_TM, D), lambda i: (i, 0)),
                  pl.BlockSpec(ytile, lambda i: (i, 0, 0)),
                  pl.BlockSpec(ytile, lambda i: (i + nb, 0, 0)),
                  pl.BlockSpec((FIN_TM, LANES), lambda i: (i, 0)),
                  pl.BlockSpec((1, D), lambda i: (0, 0))],
        out_specs=pl.BlockSpec((FIN_TM, D), lambda i: (i, 0)),
        out_shape=jax.ShapeDtypeStruct((T, D), F32),
        compiler_params=_cparams(("parallel",)),
        name="final",
    )(x1, y, y, route, final_norm.reshape(1, D))


def kernel(x, attn_norm, w_in, w_cmp_k, w_cmp_v, cmp_pos, hg_lb_logits, hg_norm, w_br_nsa, w_br_hg, w_out,
           ffn_norm, w_grp, b_grp, w_rtr, b_rtr, w_gate, w_up, w_down, final_norm):
    B, S, D = x.shape
    assert attn_norm.shape[0] == 1, "single-layer block"
    x2 = x.reshape(B * S, D)
    (kcin, vcin, hq, hf, hi, hg, mg), (qt, ksa, vst, kw, vwt, gtt) = _in_proj(x2, attn_norm[0], w_in[0], B, S)
    kc, vct = _compress(kcin, vcin, w_cmp_k[0], w_cmp_v[0], cmp_pos[0], B, S)
    o_nsa = _nsa_attn(qt, kc, vct, ksa, vst, kw, vwt, gtt, B, S).reshape(B * S, -1)
    o_hg = _hgrn(hq, hf, hi, hg, hg_lb_logits, hg_norm[0], B, S).reshape(B * S, -1)
    x1, h2, route = _merge(o_nsa, o_hg, mg, x2, w_br_nsa[0], w_br_hg[0], w_out[0], ffn_norm[0],
                           w_grp[0], b_grp[0], w_rtr[0], b_rtr[0])
    y = _moe(h2, route, w_gate[0], w_up[0], w_down[0])
    out = _final(x1, y, route, final_norm)
    return out.reshape(B, S, D)
```

```python
import functools

import numpy as np
import jax
import jax.numpy as jnp
from jax import lax
from jax.experimental import pallas as pl
from jax.experimental.pallas import tpu as pltpu

F32 = jnp.float32
BF16 = jnp.bfloat16

NSA_HEADS = 8
NSA_KV_GROUPS = 2
NSA_HEAD_DIM = 64
NSA_R = NSA_HEADS // NSA_KV_GROUPS
CMP_LEN = 32
CMP_STRIDE = 16
SLC_LEN = 64
SLC_SHIFT = 6
SLC_TOPN = 16
WINDOW = 512
Q_BLOCK = 128
ROPE_THETA = 500000.0
ROPE_DIM = NSA_HEAD_DIM // 4
HG_HEADS = 4
HG_DIM = 128
HG_CHUNK = 64
N_GROUPS = 4
EXPERTS_PER_GROUP = 8
N_EXPERTS = N_GROUPS * EXPERTS_PER_GROUP
RMS_EPS = 1e-6
NEG = -1e30
FORCE_SCORE = 1e4

LANES = 128
VMEM_LIMIT = 56 * 1024 * 1024

NT_DIMS = (((1,), (1,)), ((), ()))
TN_DIMS = (((0,), (0,)), ((), ()))


def _cparams(sem):
    return pltpu.CompilerParams(dimension_semantics=sem, vmem_limit_bytes=VMEM_LIMIT)


IN_TM = 512
IN_SEGS = (("q", 512, 512), ("kc", 128, 128), ("vc", 128, 128), ("kv", 512, 512), ("gate", 24, 128),
           ("hq", 512, 512), ("hf", 512, 512), ("hi", 512, 512), ("hg", 512, 512), ("merge", 2048, 2048))
IN_F32 = ("kc", "vc")
IN_PREP = ("q", "kv", "gate")


def _in_proj_kernel(x_ref, g_ref, w_ref, c_ref, s1_ref, s2_ref, nege_ref, *out_refs):
    n_plain = len(IN_SEGS) - len(IN_PREP)
    plain_refs, prep_refs = iter(out_refs[:n_plain]), out_refs[n_plain:]
    x = x_ref[...]
    h = x * lax.rsqrt(jnp.mean(x * x, axis=-1, keepdims=True) + RMS_EPS) * g_ref[...]
    h = h.astype(BF16)
    off, kept = 0, {}
    for name, _, wpad in IN_SEGS:
        y = jnp.dot(h, w_ref[:, off:off + wpad], preferred_element_type=F32)
        off += wpad
        if name in IN_PREP:
            kept[name] = y
            if len(kept) == len(IN_PREP):
                _prep_body(kept["q"], kept["kv"], kept["gate"], c_ref[...], s1_ref[...], s2_ref[...],
                           nege_ref[...], *prep_refs)
        else:
            o_ref = next(plain_refs)
            o_ref[...] = y.astype(o_ref.dtype)


def _in_proj(x2, attn_norm, w_in, B, S):
    T, D = x2.shape
    G, dk, QB = NSA_KV_GROUPS, NSA_HEAD_DIM, Q_BLOCK
    pieces, off = [], 0
    for _, w, wpad in IN_SEGS:
        p = w_in[:, off:off + w]
        if wpad != w:
            p = jnp.pad(p, ((0, 0), (0, wpad - w)))
        pieces.append(p)
        off += w
    wcat = jnp.concatenate(pieces, axis=1).astype(BF16)
    NP = wcat.shape[1]
    plain = [(jax.ShapeDtypeStruct((T, wpad), F32 if name in IN_F32 else BF16),
              pl.BlockSpec((IN_TM, wpad), lambda i: (i, 0)))
             for name, _, wpad in IN_SEGS if name not in IN_PREP]
    nj = S // IN_TM
    c, s1, s2 = _rope_tables(jnp.arange(S))
    key = np.arange(S)
    nege = jnp.asarray(np.where(key[:, None] // SLC_LEN == np.arange(LANES)[None, :], NEG, 0.0), BF16)
    tab = pl.BlockSpec((IN_TM, LANES), lambda i: (i % nj, 0))
    rows_out = lambda w, dt: (jax.ShapeDtypeStruct((B, G, S, w), dt),
                              pl.BlockSpec((1, G, IN_TM, w), lambda i: (i // nj, 0, i % nj, 0)))
    cols_out = lambda n, dt: (jax.ShapeDtypeStruct((B, G, n, S), dt),
                              pl.BlockSpec((1, G, n, IN_TM), lambda i: (i // nj, 0, 0, i % nj)))
    qt_out = (jax.ShapeDtypeStruct((B, G, S // QB, LANES, NSA_R * QB), BF16),
              pl.BlockSpec((1, G, IN_TM // QB, LANES, NSA_R * QB), lambda i: (i // nj, 0, i % nj, 0, 0)))
    prep = [qt_out, rows_out(2 * LANES, BF16), cols_out(dk, BF16), rows_out(LANES, BF16),
            cols_out(dk, BF16), cols_out(GATE_ROWS, F32)]
    outs = pl.pallas_call(
        _in_proj_kernel,
        grid=(T // IN_TM,),
        in_specs=[pl.BlockSpec((IN_TM, D), lambda i: (i, 0)),
                  pl.BlockSpec((1, D), lambda i: (0, 0)),
                  pl.BlockSpec((D, NP), lambda i: (0, 0)), tab, tab, tab, tab],
        out_specs=[o[1] for o in plain + prep],
        out_shape=[o[0] for o in plain + prep],
        compiler_params=_cparams(("parallel",)),
        name="in_proj",
    )(x2, attn_norm.reshape(1, D), wcat, c, s1, s2, nege)
    return outs[:len(plain)], outs[len(plain):]


def _rope_tables(pos):
    half = ROPE_DIM // 2
    d = np.arange(LANES) % NSA_HEAD_DIM
    inv_freq = ROPE_THETA ** (-jnp.arange(half, dtype=F32) / half)
    lane_freq = jnp.where(d < ROPE_DIM, inv_freq[d % half], 0.0)
    ang = pos.astype(F32)[:, None] * lane_freq[None, :]
    cos, sin = jnp.cos(ang), jnp.sin(ang)
    s1 = jnp.where(d < half, -sin, 0.0)
    s2 = jnp.where((d >= half) & (d < ROPE_DIM), sin, 0.0)
    return cos, s1, s2


def _rope(x, c, s1, s2):
    half = ROPE_DIM // 2
    return x * c + pltpu.roll(x, LANES - half, 1) * s1 + pltpu.roll(x, half, 1) * s2


GATE_ROWS = 16
LOG2E = 1.4426950408889634


def _prep_body(q, kv, gate, c, s1, s2, nege, qt_ref, ksa_ref, vst_ref, kw_ref, vwt_ref, gtt_ref):
    lane = lax.broadcasted_iota(jnp.int32, c.shape, 1)
    lo = lane < NSA_HEAD_DIM
    dk, QB = NSA_HEAD_DIM, Q_BLOCK

    def split(x):
        return (jnp.where(lo, x, 0.0), jnp.where(lo, pltpu.roll(x, dk, 1), 0.0))

    scale = (dk ** -0.5) * LOG2E
    for blk in range(NSA_HEADS // 2):
        x = q[:, blk * LANES:(blk + 1) * LANES]
        for hh, part in enumerate(split(_rope(x, c, s1, s2) * scale)):
            h = 2 * blk + hh
            g, r = h // NSA_R, h % NSA_R
            pt = part.T.astype(BF16)
            for i in range(q.shape[0] // QB):
                qt_ref[0, g, i, :, r * QB:(r + 1) * QB] = pt[:, i * QB:(i + 1) * QB]
    for g, part in enumerate(split(_rope(kv[:, 0:LANES], c, s1, s2))):
        ksa_ref[0, g] = jnp.concatenate([part.astype(BF16), nege], axis=1)
    for g, part in enumerate(split(_rope(kv[:, 2 * LANES:3 * LANES], c, s1, s2))):
        kw_ref[0, g] = part.astype(BF16)
    for blk, ref in ((1, vst_ref), (3, vwt_ref)):
        for g, part in enumerate(split(kv[:, blk * LANES:(blk + 1) * LANES])):
            ref[0, g] = part.T[0:dk, :].astype(BF16)
    sg = jax.nn.sigmoid(gate)
    gtt_ref[0, 0] = sg.T[0:GATE_ROWS, :]
    gtt_ref[0, 1] = pltpu.roll(sg, LANES - 3 * NSA_R, 1).T[0:GATE_ROWS, :]


def _compress_kernel(tk_ref, tv_ref, wk_ref, wv_ref, pos_ref, c_ref, s1_ref, s2_ref, kc_ref, vct_ref):
    nc = c_ref.shape[0]
    dk = NSA_HEAD_DIM
    lane = lax.broadcasted_iota(jnp.int32, (nc, LANES), 1)
    lo = lane < dk

    def comp(t_ref, w_ref):
        a = jnp.zeros((nc, LANES), F32)
        b = jnp.zeros((nc, LANES), F32)
        c0 = jnp.zeros((SUBLANES, LANES), F32)
        for j in range(CMP_STRIDE):
            u = t_ref[pl.ds(j, nc, stride=CMP_STRIDE), :].astype(BF16)
            a = a + jnp.dot(u, w_ref[j], preferred_element_type=F32)
            b = b + jnp.dot(u, w_ref[CMP_STRIDE + j], preferred_element_type=F32)
        for l in range(CMP_LEN):
            c0 = c0 + jnp.dot(pos_ref[l], w_ref[l], preferred_element_type=F32)
        return a + pltpu.roll(b, nc - 1, 0) + c0[0:1, :]

    def split(x):
        return (jnp.where(lo, x, 0.0), jnp.where(lo, pltpu.roll(x, dk, 1), 0.0))

    for g, part in enumerate(split(_rope(comp(tk_ref, wk_ref), c_ref[...], s1_ref[...], s2_ref[...]))):
        kc_ref[0, g] = part.astype(BF16)
    for g, part in enumerate(split(comp(tv_ref, wv_ref))):
        vct_ref[0, g] = part.T[0:dk, :].astype(BF16)


def _compress(kcin, vcin, w_cmp_k, w_cmp_v, cmp_pos, B, S):
    G, dk = NSA_KV_GROUPS, NSA_HEAD_DIM
    nc = S // CMP_STRIDE

    def block_diag(w):
        w3 = w.reshape(CMP_LEN, dk, dk)
        z = jnp.zeros_like(w3)
        return jnp.concatenate([jnp.concatenate([w3, z], axis=2),
                                jnp.concatenate([z, w3], axis=2)], axis=1).astype(BF16)

    pos = jnp.concatenate([cmp_pos, cmp_pos], axis=1)
    pos = jnp.broadcast_to(pos[:, None, :], (CMP_LEN, SUBLANES, G * dk)).astype(BF16)
    c, s1, s2 = _rope_tables(jnp.arange(nc) * CMP_STRIDE)
    tspec = pl.BlockSpec((S, LANES), lambda b: (b, 0))
    wspec = pl.BlockSpec((CMP_LEN, LANES, LANES), lambda b: (0, 0, 0))
    tab = pl.BlockSpec((nc, LANES), lambda b: (0, 0))
    return pl.pallas_call(
        _compress_kernel,
        grid=(B,),
        in_specs=[tspec, tspec, wspec, wspec,
                  pl.BlockSpec((CMP_LEN, SUBLANES, LANES), lambda b: (0, 0, 0)), tab, tab, tab],
        out_specs=[pl.BlockSpec((1, G, nc, LANES), lambda b: (b, 0, 0, 0)),
                   pl.BlockSpec((1, G, dk, nc), lambda b: (b, 0, 0, 0))],
        out_shape=[jax.ShapeDtypeStruct((B, G, nc, LANES), BF16),
                   jax.ShapeDtypeStruct((B, G, dk, nc), BF16)],
        compiler_params=_cparams(("parallel",)),
        name="compress",
    )(kcin, vcin, block_diag(w_cmp_k), block_diag(w_cmp_v), pos, c, s1, s2)


CMP_STRIDE_SHIFT = 4
CMP_ROWS = 128
CMP_ROWS_SHIFT = 7
SEL_KC = 256
SEL_KC_SHIFT = 8
SEL_UC = 2
SEL_UC_SHIFT = 1
WIN_KEYS = WINDOW + Q_BLOCK


def _col_reduce(x, op, fin):
    n = x.shape[0]
    while (n // 2) % 8 == 0 and n > 8:
        n //= 2
        x = op(x[:n], x[n:])
    return fin(x, axis=0, keepdims=True)


def _col_max(x):
    return _col_reduce(x, jnp.maximum, jnp.max)


def _col_min(x):
    return _col_reduce(x, jnp.minimum, jnp.min)


def _col_sum(x):
    return _col_reduce(x, jnp.add, jnp.sum)


def _softmax_cols(s):
    e = jnp.exp2(s - _col_max(s))
    return e, _col_sum(e)


def _nsa_kernel(qt_ref, kc_ref, vct_ref, ksa_ref, vst_ref, kw_ref, vwt_ref, gtt_ref, ovt_ref,
                o_ref, rhs_sc, m0_sc, m1_sc, l0_sc, l1_sc, a0_sc, a1_sc, sa_sc, sb_sc, ocmp_sc, imp_sc,
                *, n_sel):
    m_scs, l_scs, acc_scs = (m0_sc, m1_sc), (l0_sc, l1_sc), (a0_sc, a1_sc)
    s_bufs = (sa_sc, sb_sc)
    R, QB = NSA_R, Q_BLOCK
    cols = R * QB
    i = pl.program_id(2)
    t0 = i * QB
    qt = qt_ref[0, 0, 0]

    def per_head(x):
        return jnp.concatenate([x] * (cols // x.shape[1]), axis=1)

    def rc(n):
        return (lax.broadcasted_iota(jnp.int32, (n, QB), 0),
                t0 + lax.broadcasted_iota(jnp.int32, (n, QB), 1))

    nc = kc_ref.shape[2]

    def cmp_branch(nr):
        sc = jnp.dot(kc_ref[0, 0, 0:nr, :], qt, preferred_element_type=F32)
        n_idx, tq_c = rc(nr)
        cvis = (n_idx * CMP_STRIDE + (CMP_LEN - 1)) <= tq_c
        ec, lc = _softmax_cols(sc + per_head(jnp.where(cvis, 0.0, NEG)))
        pc = ec * per_head(jnp.where(cvis, 1.0, 0.0)) * (1.0 / lc)
        ocmp_sc[...] = jnp.dot(vct_ref[0, 0, :, 0:nr], pc.astype(BF16), preferred_element_type=F32)
        psum = pc[:, 0:QB] + pc[:, QB:2 * QB] + pc[:, 2 * QB:3 * QB] + pc[:, 3 * QB:4 * QB]
        p_hi = psum.astype(BF16)
        p_lo = (psum - p_hi.astype(F32)).astype(BF16)
        ovt = ovt_ref[:, 0:nr]
        imp_sc[...] = (jnp.dot(ovt, p_hi, preferred_element_type=F32) +
                       jnp.dot(ovt, p_lo, preferred_element_type=F32))

    n_var = nc // CMP_ROWS
    last_vis = jnp.right_shift(t0 + QB - CMP_LEN, CMP_STRIDE_SHIFT)
    var = jnp.minimum(jnp.right_shift(jnp.maximum(last_vis, 0), CMP_ROWS_SHIFT), n_var - 1)
    for k in range(n_var):
        pl.when(var == k)(functools.partial(cmp_branch, (k + 1) * CMP_ROWS))
    o_cmp = ocmp_sc[...]
    imp = imp_sc[...]

    start = pl.multiple_of(jnp.maximum(t0 - WINDOW, 0), QB)
    sw = jnp.dot(kw_ref[0, 0, pl.ds(start, WIN_KEYS), :], qt, preferred_element_type=F32)

    j = lax.broadcasted_iota(jnp.int32, (LANES, QB), 0)
    tq = t0 + lax.broadcasted_iota(jnp.int32, (LANES, QB), 1)
    force = (j == jnp.right_shift(tq, SLC_SHIFT)) | (j == 0)
    valid = (j * SLC_LEN) <= tq
    score = jnp.where(force, -jnp.inf, jnp.where(valid, imp, -1.0))
    work = jnp.where(j < n_sel, score, -jnp.inf)
    jf = j.astype(F32)
    for _ in range(min(SLC_TOPN, n_sel) - 2):
        mx = _col_max(work)
        idx = _col_min(jnp.where(work == mx, jf, float(LANES)))
        work = jnp.where(jf == idx, -jnp.inf, work)
    notsel = jnp.where(work == -jnp.inf, 0.0, 1.0)

    k_idx, tq_w = rc(WIN_KEYS)
    rel = tq_w - (start + k_idx)
    pw, lw = _softmax_cols(sw + per_head(jnp.where((rel >= 0) & (rel < WINDOW), 0.0, NEG)))
    o_win = jnp.dot(vwt_ref[0, 0, :, pl.ds(start, WIN_KEYS)], pw.astype(BF16),
                    preferred_element_type=F32) * (1.0 / lw)

    rhs_sc[0:LANES, :] = qt
    rhs_sc[LANES:2 * LANES, :] = jnp.concatenate([notsel.astype(BF16)] * R, axis=1)
    halves = tuple(zip(m_scs, l_scs, acc_scs))
    hw = cols // len(halves)
    for m_sc, l_sc, acc_sc in halves:
        m_sc[...] = jnp.full(m_sc.shape, NEG, F32)
        l_sc[...] = jnp.zeros(l_sc.shape, F32)
        acc_sc[...] = jnp.zeros(acc_sc.shape, F32)

    def sel_scores(chunk_ids):
        kas = [ksa_ref[0, 0, pl.ds(pl.multiple_of(c * SEL_KC, SEL_KC), SEL_KC), :] for c in chunk_ids]
        return [[jnp.dot(ka, rhs_sc[:, hp * hw:(hp + 1) * hw], preferred_element_type=F32)
                 for ka in kas] for hp in range(len(halves))]

    def sel_update(chunk_ids, scores, diagonal):
        k0s = [pl.multiple_of(c * SEL_KC, SEL_KC) for c in chunk_ids]
        vts = [vst_ref[0, 0, :, pl.ds(k0, SEL_KC)] for k0 in k0s]
        for hp, (m_sc, l_sc, acc_sc) in enumerate(halves):
            ss = scores[hp]
            if diagonal:
                k_idx, tq_s = rc(SEL_KC)
                ss = [s + jnp.concatenate([jnp.where(k0 + k_idx <= tq_s, 0.0, NEG)] * (hw // QB), axis=1)
                      for k0, s in zip(k0s, ss)]
            m_old = m_sc[...]
            m_new = functools.reduce(jnp.maximum, [_col_max(s) for s in ss], m_old)
            alpha = jnp.exp2(m_old - m_new)
            ps = [jnp.exp2(s - m_new) for s in ss]
            l_sc[...] = alpha * l_sc[...] + functools.reduce(jnp.add, [_col_sum(p) for p in ps])
            pv = [jnp.dot(vt, p.astype(BF16), preferred_element_type=F32) for vt, p in zip(vts, ps)]
            acc_sc[...] = alpha * acc_sc[...] + functools.reduce(jnp.add, pv)
            m_sc[...] = m_new

    n_units = jnp.right_shift(t0, SEL_KC_SHIFT + SEL_UC_SHIFT)
    nh = len(halves)

    def unit(u):
        return [SEL_UC * u + c for c in range(SEL_UC)]

    def put_scores(buf, scores):
        for hp in range(nh):
            for c in range(SEL_UC):
                buf[hp * SEL_UC + c] = scores[hp][c]

    def get_scores(buf):
        return [[buf[hp * SEL_UC + c] for c in range(SEL_UC)] for hp in range(nh)]

    put_scores(s_bufs[0], sel_scores(unit(0)))

    def sel_step(u, carry):
        for parity in range(2):
            @pl.when((u & 1) == parity)
            def _():
                put_scores(s_bufs[1 - parity], sel_scores(unit(u + 1)))
                sel_update(unit(u), get_scores(s_bufs[parity]), False)
        return carry

    lax.fori_loop(0, n_units, sel_step, 0)
    in_unit = jnp.right_shift(t0 & (SEL_UC * SEL_KC - 1), SEL_KC_SHIFT)
    for parity in range(2):
        for last in range(SEL_UC):
            @pl.when(((n_units & 1) == parity) & (in_unit == last))
            def _():
                scores = [s[:last + 1] for s in get_scores(s_bufs[parity])]
                sel_update(unit(n_units)[:last + 1], scores, True)
    o_slc = jnp.concatenate([a[...] * (1.0 / l[...]) for _, l, a in halves], axis=1)

    gtt = gtt_ref[0, 0]
    outs = []
    for r in range(R):
        cs = slice(r * QB, (r + 1) * QB)
        outs.append(gtt[3 * r:3 * r + 1, :] * o_cmp[:, cs] + gtt[3 * r + 1:3 * r + 2, :] * o_slc[:, cs] +
                    gtt[3 * r + 2:3 * r + 3, :] * o_win[:, cs])
    o_ref[0] = jnp.concatenate(outs, axis=0).T.astype(o_ref.dtype)


def _nsa_attn(qt, kc, vct, ksa, vst, kw, vwt, gtt, B, S):
    G, R, QB, dk = NSA_KV_GROUPS, NSA_R, Q_BLOCK, NSA_HEAD_DIM
    nc = S // CMP_STRIDE
    n_sel = S // SLC_LEN
    assert n_sel <= LANES and n_sel >= SLC_TOPN and S % (SEL_UC * SEL_KC) == 0 and S >= WIN_KEYS
    assert SEL_KC == 1 << SEL_KC_SHIFT and SEL_UC == 1 << SEL_UC_SHIFT and SEL_KC % QB == 0
    assert nc % CMP_ROWS == 0 and CMP_ROWS == 1 << CMP_ROWS_SHIFT and CMP_STRIDE == 1 << CMP_STRIDE_SHIFT
    cs = np.arange(nc)[None, :] * CMP_STRIDE
    ss = np.arange(LANES)[:, None] * SLC_LEN
    ovm = (cs < ss + SLC_LEN) & (cs + CMP_LEN - 1 >= ss) & (np.arange(LANES)[:, None] < n_sel) \
        & (np.arange(nc)[None, :] < nc - 1)
    ovt = jnp.asarray(ovm, BF16)
    rows_in = lambda n, w: pl.BlockSpec((1, 1, n, w), lambda b, g, i: (b, g, 0, 0))
    cols = R * QB
    return pl.pallas_call(
        functools.partial(_nsa_kernel, n_sel=n_sel),
        grid=(B, G, S // QB),
        in_specs=[pl.BlockSpec((1, 1, 1, LANES, cols), lambda b, g, i: (b, g, i, 0, 0)),
                  rows_in(nc, LANES), rows_in(dk, nc), rows_in(S, 2 * LANES), rows_in(dk, S),
                  rows_in(S, LANES), rows_in(dk, S),
                  pl.BlockSpec((1, 1, GATE_ROWS, QB), lambda b, g, i: (b, g, 0, i)),
                  pl.BlockSpec((LANES, nc), lambda b, g, i: (0, 0))],
        out_specs=pl.BlockSpec((1, QB, R * dk), lambda b, g, i: (b, i, g)),
        out_shape=jax.ShapeDtypeStruct((B, S, NSA_HEADS * dk), BF16),
        scratch_shapes=[pltpu.VMEM((2 * LANES, cols), BF16)] +
                       [pltpu.VMEM((1, cols // 2), F32)] * 4 + [pltpu.VMEM((dk, cols // 2), F32)] * 2 +
                       [pltpu.VMEM((2 * SEL_UC, SEL_KC, cols // 2), F32)] * 2 +
                       [pltpu.VMEM((dk, cols), F32), pltpu.VMEM((LANES, QB), F32)],
        compiler_params=_cparams(("parallel", "parallel", "arbitrary")),
        name="nsa_attn",
    )(qt, kc, vct, ksa, vst, kw, vwt, gtt, ovt)


HG_TB = 256
HG_LEVELS = 6


def _hgrn_consts():
    C = HG_CHUNK
    M = np.zeros((HG_LEVELS + 2, C, C), np.float32)
    RM = np.zeros((HG_LEVELS, C, 1), np.float32)
    BM = np.zeros((HG_LEVELS + 1, C, C), np.float32)
    for lv in range(HG_LEVELS):
        m = 1 << lv
        for t in range(C):
            bs = (t // (2 * m)) * (2 * m)
            mid = bs + m - 1
            if t - bs >= m:
                RM[lv, t, 0] = 1.0
                M[lv, t, mid + 1:t + 1] = 1.0
            else:
                M[lv, t, t + 1:mid + 1] = 1.0
            BM[lv, t, bs:bs + 2 * m] = 1.0
    BM[HG_LEVELS] = np.eye(C, dtype=np.float32)
    M[HG_LEVELS] = np.tril(np.ones((C, C), np.float32))
    M[HG_LEVELS + 1] = np.triu(np.ones((C, C), np.float32), 1)
    mm = M.reshape(-1, C)
    mm3 = np.concatenate([mm, mm, mm], axis=1)
    rm = np.broadcast_to(RM, (HG_LEVELS, C, HG_HEADS * HG_DIM)).copy()
    return jnp.asarray(mm3, BF16), jnp.asarray(rm, F32), jnp.asarray(BM, F32)


def _hgrn_kernel(q_ref, f_ref, i_ref, g_ref, lbl_ref, gn_ref, mm_ref, rm_ref, bm_ref, o_ref, st_sc):
    C, D = HG_CHUNK, HG_DIM

    @pl.when(pl.program_id(1) == 0)
    def _():
        st_sc[...] = jnp.zeros(st_sc.shape, F32)

    lbl = lbl_ref[...]
    mxl = jnp.max(lbl, axis=0, keepdims=True)
    el = jnp.exp(lbl - mxl)
    lb_all = el[0:1, :] / jnp.sum(el, axis=0, keepdims=True)
    mm = mm_ref[...]
    gn = gn_ref[...]
    heads = [slice(h * D, (h + 1) * D) for h in range(HG_HEADS)]
    nt = lambda a, b: lax.dot_general(a, b, NT_DIMS, preferred_element_type=F32)
    for ch in range(HG_TB // C):
        rs = slice(ch * C, (ch + 1) * C)
        q = q_ref[0, rs, :].astype(F32)
        v = i_ref[0, rs, :]
        gg = g_ref[0, rs, :].astype(F32)
        f = lb_all + (1.0 - lb_all) * jax.nn.sigmoid(f_ref[0, rs, :].astype(F32))
        lg = jnp.log(f)
        kk = 1.0 - f
        g1 = lg.astype(BF16)
        r1 = lg - g1.astype(F32)
        g2 = r1.astype(BF16)
        g3 = (r1 - g2.astype(F32)).astype(BF16)
        ez = jnp.exp(jnp.dot(mm, jnp.concatenate([g1, g2, g3], axis=0), preferred_element_type=F32))
        qb, kb = q.astype(BF16), kk.astype(BF16)
        attn = [bm_ref[HG_LEVELS] * nt(qb[:, hs], kb[:, hs]) for hs in heads]
        for lv in range(HG_LEVELS):
            zl = ez[lv * C:(lv + 1) * C]
            rm = rm_ref[lv]
            ql = (q * zl * rm).astype(BF16)
            kl = (kk * zl * (1.0 - rm)).astype(BF16)
            attn = [a + bm_ref[lv] * nt(ql[:, hs], kl[:, hs]) for a, hs in zip(attn, heads)]
        zc = ez[HG_LEVELS * C:(HG_LEVELS + 1) * C]
        zs = ez[(HG_LEVELS + 1) * C:(HG_LEVELS + 2) * C]
        qz = (q * zc).astype(BF16)
        kz = (kk * zs).astype(BF16)
        sts = [st_sc[h] for h in range(HG_HEADS)]
        o_intra = [jnp.dot(a.astype(BF16), v[:, hs], preferred_element_type=F32) for a, hs in zip(attn, heads)]
        o_inter = [nt(qz[:, hs], st.astype(BF16)) for st, hs in zip(sts, heads)]
        upd = [lax.dot_general(v[:, hs], kz[:, hs], TN_DIMS, preferred_element_type=F32) for hs in heads]
        for h, hs in enumerate(heads):
            st_sc[h] = zc[C - 1:C, hs] * sts[h] + upd[h]
            o = o_intra[h] + o_inter[h]
            y = o * lax.rsqrt(jnp.mean(o * o, axis=-1, keepdims=True) + RMS_EPS) * gn
            g = gg[:, hs]
            o_ref[0, rs, hs] = (y * (g * jax.nn.sigmoid(g))).astype(o_ref.dtype)


def _hgrn(hq, hf, hi, hg, lb_logits, hg_norm, B, S):
    W = HG_HEADS * HG_DIM
    mm3, rm, bm = _hgrn_consts()
    r3 = lambda a: a.reshape(B, S, W)
    tok = pl.BlockSpec((1, HG_TB, W), lambda b, j: (b, j, 0))
    const = lambda a: pl.BlockSpec(a.shape, lambda b, j: (0,) * a.ndim)
    gn = hg_norm.reshape(1, HG_DIM)
    return pl.pallas_call(
        _hgrn_kernel,
        grid=(B, S // HG_TB),
        in_specs=[tok, tok, tok, tok, const(lb_logits), const(gn), const(mm3), const(rm), const(bm)],
        out_specs=tok,
        out_shape=jax.ShapeDtypeStruct((B, S, W), BF16),
        scratch_shapes=[pltpu.VMEM((HG_HEADS, HG_DIM, HG_DIM), F32)],
        compiler_params=_cparams(("parallel", "arbitrary")),
        name="hgrn2",
    )(r3(hq), r3(hf), r3(hi), r3(hg), lb_logits, gn, mm3, rm, bm)


MG_TM = 512


SUBLANES = 8


def _to_row_tiles(ref, x, lead=()):
    for s in range(SUBLANES):
        ref[lead + (slice(None), s, slice(None))] = x[:, s * LANES:(s + 1) * LANES]


def _from_row_tiles(ref, lead=()):
    return jnp.concatenate([ref[lead + (slice(None), s, slice(None))] for s in range(SUBLANES)], axis=1)


def _first_lane(cond, lane_f):
    return jnp.min(jnp.where(cond, lane_f, float(LANES)), axis=1, keepdims=True)


def _merge_kernel(on_ref, oh_ref, mg_ref, x_ref, wn_ref, wh_ref, wo_ref, fn_ref, wr1_ref, wr2_ref, br_ref,
                  x1_ref, h2_ref, rt_ref):
    D = x_ref.shape[1]
    a = jnp.dot(on_ref[...], wn_ref[...], preferred_element_type=F32)
    b = jnp.dot(oh_ref[...], wh_ref[...], preferred_element_type=F32)
    mixed = (jax.nn.sigmoid(mg_ref[:, 0:D].astype(F32)) * a +
             jax.nn.sigmoid(mg_ref[:, D:2 * D].astype(F32)) * b)
    x1 = x_ref[...] + jnp.dot(mixed.astype(BF16), wo_ref[...], preferred_element_type=F32)
    x1_ref[...] = x1
    h2 = x1 * lax.rsqrt(jnp.mean(x1 * x1, axis=-1, keepdims=True) + RMS_EPS) * fn_ref[...]
    _to_row_tiles(h2_ref, h2)
    h_hi = h2.astype(BF16)
    h_lo = (h2 - h_hi.astype(F32)).astype(BF16)
    logits = (jnp.dot(h_lo, wr1_ref[...], preferred_element_type=F32) +
              jnp.dot(h_hi, wr2_ref[...], preferred_element_type=F32) +
              jnp.dot(h_hi, wr1_ref[...], preferred_element_type=F32)) + br_ref[...]
    lane = lax.broadcasted_iota(jnp.int32, logits.shape, 1)
    ninf = -jnp.inf
    gl = jnp.where(lane < N_GROUPS, logits, ninf)
    gmax = jnp.max(gl, axis=1, keepdims=True)
    lane_f = lane.astype(F32)
    gsel = _first_lane(gl == gmax, lane_f)
    gprob = 1.0 / jnp.sum(jnp.exp(gl - gmax), axis=1, keepdims=True)
    lo = N_GROUPS + gsel * EXPERTS_PER_GROUP
    el = jnp.where((lane_f >= lo) & (lane_f < lo + EXPERTS_PER_GROUP), logits, ninf)
    v1 = jnp.max(el, axis=1, keepdims=True)
    i1 = _first_lane(el == v1, lane_f)
    el2 = jnp.where(lane_f == i1, ninf, el)
    v2 = jnp.max(el2, axis=1, keepdims=True)
    i2 = _first_lane(el2 == v2, lane_f)
    e2 = jnp.exp(v2 - v1)
    w1 = gprob / (1.0 + e2)
    w2 = gprob * e2 / (1.0 + e2)
    rt = jnp.where(lane == 0, i1 - N_GROUPS,
                   jnp.where(lane == 1, i2 - N_GROUPS,
                             jnp.where(lane == 2, w1, jnp.where(lane == 3, w2, 0.0))))
    rt_ref[...] = rt


def _merge(o_nsa, o_hg, mg, x2, w_br_nsa, w_br_hg, w_out, ffn_norm, w_grp, b_grp, w_rtr, b_rtr):
    T, D = x2.shape
    nr = N_GROUPS + N_EXPERTS
    wr = jnp.pad(jnp.concatenate([w_grp, w_rtr], axis=1), ((0, 0), (0, LANES - nr)))
    wr1 = wr.astype(BF16)
    wr2 = (wr - wr1.astype(F32)).astype(BF16)
    br = jnp.pad(jnp.concatenate([b_grp, b_rtr]), (0, LANES - nr)).reshape(1, LANES)
    tok = lambda w: pl.BlockSpec((MG_TM, w), lambda i: (i, 0))
    const = lambda a: pl.BlockSpec(a.shape, lambda i: (0,) * a.ndim)
    wn, wh, wo = w_br_nsa.astype(BF16), w_br_hg.astype(BF16), w_out.astype(BF16)
    fn = ffn_norm.reshape(1, D)
    return pl.pallas_call(
        _merge_kernel,
        grid=(T // MG_TM,),
        in_specs=[tok(o_nsa.shape[1]), tok(o_hg.shape[1]), tok(2 * D), tok(D),
                  const(wn), const(wh), const(wo), const(fn), const(wr1), const(wr2), const(br)],
        out_specs=[tok(D), pl.BlockSpec((MG_TM, SUBLANES, D // SUBLANES), lambda i: (i, 0, 0)), tok(LANES)],
        out_shape=[jax.ShapeDtypeStruct((T, D), F32), jax.ShapeDtypeStruct((T, SUBLANES, D // SUBLANES), F32),
                   jax.ShapeDtypeStruct((T, LANES), F32)],
        compiler_params=_cparams(("parallel",)),
        name="merge",
    )(o_nsa, o_hg, mg, x2, wn, wh, wo, fn, wr1, wr2, br)


MOE_TM = 512


def _moe_kernel(te_ref, nv_ref, src_ref, srcn_ref, dst_ref, h_hbm, wg_ref, wu_ref, wd_ref, y_hbm,
                xbuf, ybuf, wgb, wub, wdb, gsem, ssem):
    i = pl.program_id(0)
    n = pl.num_programs(0)
    tm = xbuf.shape[1]
    slot = i & 1
    prv = jnp.maximum(i - 1, 0)
    nv_i = nv_ref[i]
    nv_prev = jnp.where(i > 0, nv_ref[prv], 0)
    nv_next = jnp.where(i + 1 < n, nv_ref[jnp.minimum(i + 1, n - 1)], 0)

    def gather_copy(idx_ref, r, sl):
        return pltpu.make_async_copy(h_hbm.at[idx_ref[0, 0, r]], xbuf.at[sl, r], gsem.at[sl])

    def scatter_copy(row, r, sl):
        return pltpu.make_async_copy(ybuf.at[sl, r], y_hbm.at[row], ssem.at[sl])

    def rows(count, fn):
        groups = jnp.right_shift(count, 4)

        def body16(g, c):
            for k in range(16):
                fn(g * 16 + k)
            return c

        def body1(r, c):
            fn(r)
            return c

        lax.fori_loop(0, groups, body16, 0)
        lax.fori_loop(groups * 16, count, body1, 0)

    def wait_rows(count, copy_of):
        k = tm
        while k >= 1:
            pl.when((count & k) != 0)(lambda k=k: copy_of(k).wait())
            k //= 2

    def gather_block(k, sl):
        return pltpu.make_async_copy(h_hbm.at[pl.ds(0, k)], xbuf.at[sl, pl.ds(0, k)], gsem.at[sl])

    def scatter_block(k, sl):
        return pltpu.make_async_copy(ybuf.at[sl, pl.ds(0, k)], y_hbm.at[pl.ds(0, k)], ssem.at[sl])

    @pl.when(i == 0)
    def _():
        xbuf[...] = jnp.zeros(xbuf.shape, F32)
        rows(nv_i, lambda r: gather_copy(src_ref, r, 0).start())

    rows(nv_next, lambda r: gather_copy(srcn_ref, r, 1 - slot).start())

    @pl.when((i == 0) | (te_ref[i] != te_ref[prv]))
    def _():
        wgb[...] = wg_ref[0].astype(BF16)
        wub[...] = wu_ref[0].astype(BF16)
        wdb[...] = wd_ref[0].astype(BF16)

    @pl.when(nv_i > 0)
    def _():
        wait_rows(nv_i, lambda k: gather_block(k, slot))
        x = _from_row_tiles(xbuf, (slot,)).astype(BF16)
        hg = jnp.dot(x, wgb[...], preferred_element_type=F32)
        hu = jnp.dot(x, wub[...], preferred_element_type=F32)
        hid = (hg * jax.nn.sigmoid(hg) * hu).astype(BF16)
        _to_row_tiles(ybuf, jnp.dot(hid, wdb[...], preferred_element_type=F32), (slot,))
        rows(nv_i, lambda r: scatter_copy(dst_ref[0, 0, r], r, slot).start())

    wait_rows(nv_prev, lambda k: scatter_block(k, 1 - slot))

    @pl.when(i == n - 1)
    def _():
        wait_rows(nv_i, lambda k: scatter_block(k, slot))


def _moe(h2, route, w_gate, w_up, w_down):
    T = h2.shape[0]
    E, D, F = w_gate.shape
    tm = MOE_TM
    n_tiles = 2 * T // tm + E
    i32 = jnp.int32
    ex = jnp.concatenate([route[:, 0], route[:, 1]]).astype(i32)
    order = jnp.argsort(ex, stable=True).astype(i32)
    counts = jnp.sum(ex[:, None] == jnp.arange(E, dtype=i32)[None, :], axis=0).astype(i32)
    padded = ((counts + tm - 1) // tm) * tm
    pend = jnp.cumsum(padded)
    pstart = pend - padded
    cstart = jnp.cumsum(counts) - counts
    tile_start = jnp.arange(n_tiles, dtype=i32) * tm
    te_raw = jnp.sum(tile_start[:, None] >= pend[None, :], axis=1).astype(i32)
    active = te_raw < E
    te_c = jnp.minimum(te_raw, E - 1)
    off = tile_start - pstart[te_c]
    nv = jnp.where(active, jnp.clip(counts[te_c] - off, 0, tm), 0).astype(i32)
    n_active = jnp.sum(active.astype(i32))
    te = jnp.where(active, te_c, te_c[jnp.maximum(n_active - 1, 0)])
    rank = off[:, None] + jnp.arange(tm, dtype=i32)[None, :]
    a = order[jnp.clip(cstart[te_c][:, None] + rank, 0, 2 * T - 1)]
    valid = jnp.arange(tm, dtype=i32)[None, :] < nv[:, None]
    src = jnp.where(valid, a % T, 0).reshape(n_tiles, 1, tm)
    dst = jnp.where(valid, a, 0).reshape(n_tiles, 1, tm)

    idx_spec = lambda f: pl.BlockSpec((1, 1, tm), lambda i, te, nv: (f(i), 0, 0), memory_space=pltpu.SMEM)
    wspec = lambda r, c: pl.BlockSpec((1, r, c), lambda i, te, nv: (te[i], 0, 0))
    grid_spec = pltpu.PrefetchScalarGridSpec(
        num_scalar_prefetch=2,
        grid=(n_tiles,),
        in_specs=[idx_spec(lambda i: i), idx_spec(lambda i: jnp.minimum(i + 1, n_tiles - 1)),
                  idx_spec(lambda i: i), pl.BlockSpec(memory_space=pl.ANY),
                  wspec(D, F), wspec(D, F), wspec(F, D)],
        out_specs=pl.BlockSpec(memory_space=pl.ANY),
        scratch_shapes=[pltpu.VMEM((2, tm, SUBLANES, D // SUBLANES), F32),
                        pltpu.VMEM((2, tm, SUBLANES, D // SUBLANES), F32),
                        pltpu.VMEM((D, F), BF16), pltpu.VMEM((D, F), BF16), pltpu.VMEM((F, D), BF16),
                        pltpu.SemaphoreType.DMA((2,)), pltpu.SemaphoreType.DMA((2,))],
    )
    return pl.pallas_call(
        _moe_kernel,
        grid_spec=grid_spec,
        out_shape=jax.ShapeDtypeStruct((2 * T, SUBLANES, D // SUBLANES), F32),
        compiler_params=_cparams(("arbitrary",)),
        name="moe",
    )(te, nv, src, src, dst, h2, w_gate, w_up, w_down)


FIN_TM = 512


def _final_kernel(x1_ref, y1_ref, y2_ref, rt_ref, g_ref, o_ref):
    rt = rt_ref[...]
    x = x1_ref[...] + rt[:, 2:3] * _from_row_tiles(y1_ref) + rt[:, 3:4] * _from_row_tiles(y2_ref)
    o_ref[...] = x * lax.rsqrt(jnp.mean(x * x, axis=-1, keepdims=True) + RMS_EPS) * g_ref[...]


def _final(x1, y, route, final_norm):
    T, D = x1.shape
    nb = T // FIN_TM
    ytile = (FIN_TM, SUBLANES, D // SUBLANES)
    return pl.pallas_call(
        _final_kernel,
        grid=(nb,),
        in_specs=[pl.BlockSpec((FIN_TM, D), lambda i: (i, 0)),
                  pl.BlockSpec(ytile, lambda i: (i, 0, 0)),
                  pl.BlockSpec(ytile, lambda i: (i + nb, 0, 0)),
                  pl.BlockSpec((FIN_TM, LANES), lambda i: (i, 0)),
                  pl.BlockSpec((1, D), lambda i: (0, 0))],
        out_specs=pl.BlockSpec((FIN_TM, D), lambda i: (i, 0)),
        out_shape=jax.ShapeDtypeStruct((T, D), F32),
        compiler_params=_cparams(("parallel",)),
        name="final",
    )(x1, y, y, route, final_norm.reshape(1, D))


def kernel(x, attn_norm, w_in, w_cmp_k, w_cmp_v, cmp_pos, hg_lb_logits, hg_norm, w_br_nsa, w_br_hg, w_out,
           ffn_norm, w_grp, b_grp, w_rtr, b_rtr, w_gate, w_up, w_down, final_norm):
    B, S, D = x.shape
    assert attn_norm.shape[0] == 1, "single-layer block"
    x2 = x.reshape(B * S, D)
    (kcin, vcin, hq, hf, hi, hg, mg), (qt, ksa, vst, kw, vwt, gtt) = _in_proj(x2, attn_norm[0], w_in[0], B, S)
    kc, vct = _compress(kcin, vcin, w_cmp_k[0], w_cmp_v[0], cmp_pos[0], B, S)
    o_nsa = _nsa_attn(qt, kc, vct, ksa, vst, kw, vwt, gtt, B, S).reshape(B * S, -1)
    o_hg = _hgrn(hq, hf, hi, hg, hg_lb_logits, hg_norm[0], B, S).reshape(B * S, -1)
    x1, h2, route = _merge(o_nsa, o_hg, mg, x2, w_br_nsa[0], w_br_hg[0], w_out[0], ffn_norm[0],
                           w_grp[0], b_grp[0], w_rtr[0], b_rtr[0])
    y = _moe(h2, route, w_gate[0], w_up[0], w_down[0])
    out = _final(x1, y, route, final_norm)
    return out.reshape(B, S, D)
```
